```python
import math
import jax, jax.numpy as jnp
from jax import lax
import numpy as np

D_MODEL = 1024
BATCH = 8
SEQ = 2048
DEPTH = 2
DEC_BATCH = 128
DEC_SEQ = 4
PAST_LEN = 16384
PAGE_SIZE = 128

N_EVEN = (DEPTH + 1) // 2
N_ODD = DEPTH // 2
EPS = 1e-6
CHUNK = 128
A_GROUPS = 8
A_DIM = D_MODEL
A_HEAD = A_DIM // A_GROUPS
SSM_HEAD_DIM = 64
SSM_DIM = D_MODEL
SSM_HEADS = SSM_DIM // SSM_HEAD_DIM
SSM_GROUPS = 2
SSM_HPG = SSM_HEADS // SSM_GROUPS
SSM_STATE = 128
SSM_CONV = 4
SSM_CONV_DIM = SSM_DIM + 2 * SSM_GROUPS * SSM_STATE
PROJ_AB = 2 * A_DIM + SSM_DIM + SSM_CONV_DIM + SSM_HEADS
SC_DIM = D_MODEL
SC_CONV = 3
N_MEM = 256
MEM_HEADS = 4
MEM_HEAD_DIM = D_MODEL // MEM_HEADS
D_FF = 2816
N_EXPERTS = 8
TOP_K = 2
D_EXP = 2816

kernel_name = 'hybrid_sgmlp_ssd_shortconv_step'


def rmsnorm(x, g):
    xf = x.astype(jnp.float32)
    y = xf * lax.rsqrt(jnp.mean(xf * xf, axis=-1, keepdims=True) + EPS)
    return (y * g.astype(jnp.float32)).astype(x.dtype)


def causal_dwconv(x, buf, w):
    K = w.shape[0]
    L = x.shape[1]
    xp = jnp.concatenate([buf.astype(x.dtype), x], axis=1)
    y = xp[:, 0:L] * w[0]
    for k in range(1, K):
        y = y + xp[:, k:k + L] * w[k]
    return y, xp[:, xp.shape[1] - (K - 1):]


def spatial_gate(u, v, w_s, b_s, T):
    bsz, L, G, d = v.shape
    vc = v.reshape(bsz, L // T, T, G, d)
    mask = jnp.tril(jnp.ones((T, T), bool))
    w = jnp.where(mask[None], w_s[:, :T, :T], 0)
    s = jnp.einsum('gts,bcsgd->bctgd', w, vc) + b_s[:, :T].T[None, None, :, :, None]
    return u * s.reshape(bsz, L, G, d)


def ssd_chunk_scan(x, dt, a, bm, cm, h0, Q):
    bsz, L = x.shape[0], x.shape[1]
    nc = L // Q

    def to_chunks(t):
        return jnp.moveaxis(t.reshape((bsz, nc, Q) + t.shape[2:]), 1, 0)

    causal = jnp.tril(jnp.ones((Q, Q), bool))[None, :, :, None, None]

    def step(h, inp):
        xc, dtc, bc, cc = inp
        la = jnp.cumsum(dtc * a, axis=1)
        seg = la[:, :, None] - la[:, None, :]
        decay = jnp.exp(jnp.where(causal, seg, -jnp.inf))
        cb = jnp.einsum('btgn,bsgn->btsg', cc, bc).astype(jnp.float32)
        wts = cb[..., None] * decay * dtc[:, None]
        y = jnp.einsum('btsgh,bsghp->btghp', wts, xc.astype(jnp.float32))
        y = y + jnp.einsum('btgn,bghpn->btghp', cc.astype(jnp.float32), h) * jnp.exp(la)[..., None]
        last = la[:, -1]
        wend = jnp.exp(last[:, None] - la) * dtc
        h_new = h * jnp.exp(last)[..., None, None] + jnp.einsum('bsgh,bsgn,bsghp->bghpn', wend, bc.astype(jnp.float32), xc.astype(jnp.float32))
        return h_new, y

    h, ys = lax.scan(step, h0.astype(jnp.float32), (to_chunks(x), to_chunks(dt), to_chunks(bm), to_chunks(cm)))
    y = jnp.moveaxis(ys, 0, 1).reshape(x.shape)
    return y, h.astype(h0.dtype)


def mixer_ab(h, conv_buf, ssm_h, w_in, w_s, b_s, conv_w, conv_b, dt_bias, a_log, d_skip, g_norm, w_out, T):
    bsz, L, _ = h.shape
    proj = h @ w_in
    cuts = [A_DIM, 2 * A_DIM, 2 * A_DIM + SSM_DIM, 2 * A_DIM + SSM_DIM + SSM_CONV_DIM]
    u, v, z, xbc, dt_raw = jnp.split(proj, cuts, axis=-1)
    u = jax.nn.gelu(u)
    v = jax.nn.gelu(v)
    y_a = spatial_gate(u.reshape(bsz, L, A_GROUPS, A_HEAD), v.reshape(bsz, L, A_GROUPS, A_HEAD), w_s, b_s, T).reshape(bsz, L, A_DIM)
    xbc, conv_buf_new = causal_dwconv(xbc, conv_buf, conv_w)
    xbc = jax.nn.silu(xbc + conv_b)
    xs, bm, cm = jnp.split(xbc, [SSM_DIM, SSM_DIM + SSM_GROUPS * SSM_STATE], axis=-1)
    xs = xs.reshape(bsz, L, SSM_GROUPS, SSM_HPG, SSM_HEAD_DIM)
    bm = bm.reshape(bsz, L, SSM_GROUPS, SSM_STATE)
    cm = cm.reshape(bsz, L, SSM_GROUPS, SSM_STATE)
    dt = jax.nn.softplus(dt_raw.astype(jnp.float32) + dt_bias.astype(jnp.float32)).reshape(bsz, L, SSM_GROUPS, SSM_HPG)
    a = -jnp.exp(a_log.astype(jnp.float32)).reshape(SSM_GROUPS, SSM_HPG)
    h0 = ssm_h.reshape(bsz, SSM_GROUPS, SSM_HPG, SSM_HEAD_DIM, SSM_STATE)
    ys, h_new = ssd_chunk_scan(xs, dt, a, bm, cm, h0, T)
    ys = (ys + xs.astype(jnp.float32) * d_skip.astype(jnp.float32).reshape(SSM_GROUPS, SSM_HPG)[..., None]).astype(h.dtype)
    gated = (ys.reshape(bsz, L, SSM_DIM) * jax.nn.silu(z)).reshape(bsz, L, SSM_GROUPS, SSM_DIM // SSM_GROUPS)
    y_b = rmsnorm(gated, g_norm.reshape(SSM_GROUPS, SSM_DIM // SSM_GROUPS)).reshape(bsz, L, SSM_DIM)
    out = jnp.concatenate([y_a, y_b], axis=-1) @ w_out
    return out, conv_buf_new, h_new.reshape(bsz, SSM_HEADS, SSM_HEAD_DIM, SSM_STATE), v


def mixer_c(h, buf, w_in, conv_w, w_out):
    bg, cg, hx = jnp.split(h @ w_in, 3, axis=-1)
    y, buf_new = causal_dwconv(cg * hx, buf, conv_w)
    return (bg * y) @ w_out, buf_new


def mem_attend(h, mk, mv, wq, wo):
    bsz, L, _ = h.shape
    q = (h @ wq).reshape(bsz, L, MEM_HEADS, MEM_HEAD_DIM)
    s = jnp.einsum('blhe,bmhe->bhlm', q, mk).astype(jnp.float32) * (MEM_HEAD_DIM ** -0.5)
    pr = jax.nn.softmax(s, axis=-1).astype(h.dtype)
    o = jnp.einsum('bhlm,bmhe->blhe', pr, mv).reshape(bsz, L, D_MODEL)
    return o @ wo


def swiglu(h, wg, wu, wd):
    return (jax.nn.silu(h @ wg) * (h @ wu)) @ wd


def moe_swiglu(h, w_router, wg, wu, wd):
    logits = (h @ w_router).astype(jnp.float32)
    top_v, top_i = lax.top_k(logits, TOP_K)
    gates = jax.nn.softmax(top_v, axis=-1)
    combine = jnp.sum(jax.nn.one_hot(top_i, N_EXPERTS, dtype=jnp.float32) * gates[..., None], axis=-2).astype(h.dtype)
    out = jnp.zeros_like(h)
    for e in range(N_EXPERTS):
        out = out + combine[..., e:e + 1] * swiglu(h, wg[e], wu[e], wd[e])
    return out


def trunk(x, mem_k, mem_v, ssm_h, ssm_conv, sc_conv, T, p):
    ssm_out, ssm_conv_out, sc_out, v_out = [], [], [], []
    for l in range(DEPTH):
        i = l // 2
        h = rmsnorm(x, p['g_mix'][l])
        if l % 2 == 0:
            mix, cbuf, hs, v = mixer_ab(h, ssm_conv[i], ssm_h[i], p['w_in_ab'][i], p['w_spatial'][i], p['b_spatial'][i],
                                        p['conv_w_ssm'][i], p['conv_b_ssm'][i], p['dt_bias'][i], p['a_log'][i],
                                        p['d_skip'][i], p['g_ssm_norm'][i], p['w_out_ab'][i], T)
            ssm_out.append(hs)
            ssm_conv_out.append(cbuf)
            v_out.append(v)
        else:
            mix, sbuf = mixer_c(h, sc_conv[i], p['w_in_c'][i], p['conv_w_c'][i], p['w_out_c'][i])
            sc_out.append(sbuf)
        x = x + mix
        x = x + mem_attend(rmsnorm(x, p['g_mem'][l]), mem_k[l], mem_v[l], p['w_mem_q'][l], p['w_mem_o'][l])
        hf = rmsnorm(x, p['g_ffn'][l])
        if l % 2 == 0:
            x = x + swiglu(hf, p['w_ffn_gate'][i], p['w_ffn_up'][i], p['w_ffn_down'][i])
        else:
            x = x + moe_swiglu(hf, p['w_router'][i], p['w_exp_gate'][i], p['w_exp_up'][i], p['w_exp_down'][i])
    return rmsnorm(x, p['g_final']), jnp.stack(ssm_out), jnp.stack(ssm_conv_out), jnp.stack(sc_out), jnp.stack(v_out)


def setup_inputs(seed: int = 0) -> dict:
    key = jax.random.key(seed)
    ks = iter(jax.random.split(key, 48))

    def nrm(shape, scale):
        return jax.random.normal(next(ks), shape, jnp.float32) * scale

    def gain(shape):
        return 1.0 + nrm(shape, 0.02)

    dt0 = jnp.exp(jax.random.uniform(next(ks), (N_EVEN, SSM_HEADS), jnp.float32, math.log(1e-3), math.log(1e-1)))
    return {
        'x_prompt': nrm((BATCH, SEQ, D_MODEL), 1.0),
        'x_sample': nrm((DEC_BATCH, DEC_SEQ, D_MODEL), 1.0),
        'mem_prompt': nrm((BATCH, N_MEM, D_MODEL), 1.0),
        'state_ssm': nrm((N_EVEN, DEC_BATCH, SSM_HEADS, SSM_HEAD_DIM, SSM_STATE), 0.5),
        'state_ssm_conv': nrm((N_EVEN, DEC_BATCH, SSM_CONV - 1, SSM_CONV_DIM), 1.0),
        'state_sconv': nrm((N_ODD, DEC_BATCH, SC_CONV - 1, SC_DIM), 1.0),
        'cache_mem_k': nrm((DEPTH, DEC_BATCH, N_MEM, MEM_HEADS, MEM_HEAD_DIM), 1.0),
        'cache_mem_v': nrm((DEPTH, DEC_BATCH, N_MEM, MEM_HEADS, MEM_HEAD_DIM), 1.0),
        'g_mix': gain((DEPTH, D_MODEL)),
        'g_mem': gain((DEPTH, D_MODEL)),
        'g_ffn': gain((DEPTH, D_MODEL)),
        'g_final': gain((D_MODEL,)),
        'w_in_ab': nrm((N_EVEN, D_MODEL, PROJ_AB), D_MODEL ** -0.5),
        'w_spatial': nrm((N_EVEN, A_GROUPS, CHUNK, CHUNK), CHUNK ** -0.5),
        'b_spatial': 1.0 + nrm((N_EVEN, A_GROUPS, CHUNK), 0.02),
        'conv_w_ssm': nrm((N_EVEN, SSM_CONV, SSM_CONV_DIM), SSM_CONV ** -0.5),
        'conv_b_ssm': nrm((N_EVEN, SSM_CONV_DIM), 0.02),
        'dt_bias': dt0 + jnp.log(-jnp.expm1(-dt0)),
        'a_log': jnp.log(jax.random.uniform(next(ks), (N_EVEN, SSM_HEADS), jnp.float32, 1.0, 16.0)),
        'd_skip': 1.0 + nrm((N_EVEN, SSM_HEADS), 0.1),
        'g_ssm_norm': gain((N_EVEN, SSM_DIM)),
        'w_out_ab': nrm((N_EVEN, A_DIM + SSM_DIM, D_MODEL), (A_DIM + SSM_DIM) ** -0.5),
        'w_ffn_gate': nrm((N_EVEN, D_MODEL, D_FF), D_MODEL ** -0.5),
        'w_ffn_up': nrm((N_EVEN, D_MODEL, D_FF), D_MODEL ** -0.5),
        'w_ffn_down': nrm((N_EVEN, D_FF, D_MODEL), D_FF ** -0.5),
        'w_in_c': nrm((N_ODD, D_MODEL, 3 * SC_DIM), D_MODEL ** -0.5),
        'conv_w_c': nrm((N_ODD, SC_CONV, SC_DIM), SC_CONV ** -0.5),
        'w_out_c': nrm((N_ODD, SC_DIM, D_MODEL), SC_DIM ** -0.5),
        'w_router': nrm((N_ODD, D_MODEL, N_EXPERTS), D_MODEL ** -0.5),
        'w_exp_gate': nrm((N_ODD, N_EXPERTS, D_MODEL, D_EXP), D_MODEL ** -0.5),
        'w_exp_up': nrm((N_ODD, N_EXPERTS, D_MODEL, D_EXP), D_MODEL ** -0.5),
        'w_exp_down': nrm((N_ODD, N_EXPERTS, D_EXP, D_MODEL), D_EXP ** -0.5),
        'w_mem_q': nrm((DEPTH, D_MODEL, D_MODEL), D_MODEL ** -0.5),
        'w_mem_k': nrm((DEPTH, D_MODEL, D_MODEL), D_MODEL ** -0.5),
        'w_mem_v': nrm((DEPTH, D_MODEL, D_MODEL), D_MODEL ** -0.5),
        'w_mem_o': nrm((DEPTH, D_MODEL, D_MODEL), D_MODEL ** -0.5),
    }


def reference(x_prompt, x_sample, mem_prompt, state_ssm, state_ssm_conv, state_sconv, cache_mem_k, cache_mem_v,
              g_mix, g_mem, g_ffn, g_final, w_in_ab, w_spatial, b_spatial, conv_w_ssm, conv_b_ssm, dt_bias, a_log,
              d_skip, g_ssm_norm, w_out_ab, w_ffn_gate, w_ffn_up, w_ffn_down, w_in_c, conv_w_c, w_out_c, w_router,
              w_exp_gate, w_exp_up, w_exp_down, w_mem_q, w_mem_k, w_mem_v, w_mem_o):
    p = dict(g_mix=g_mix, g_mem=g_mem, g_ffn=g_ffn, g_final=g_final, w_in_ab=w_in_ab, w_spatial=w_spatial,
             b_spatial=b_spatial, conv_w_ssm=conv_w_ssm, conv_b_ssm=conv_b_ssm, dt_bias=dt_bias, a_log=a_log,
             d_skip=d_skip, g_ssm_norm=g_ssm_norm, w_out_ab=w_out_ab, w_ffn_gate=w_ffn_gate, w_ffn_up=w_ffn_up,
             w_ffn_down=w_ffn_down, w_in_c=w_in_c, conv_w_c=conv_w_c, w_out_c=w_out_c, w_router=w_router,
             w_exp_gate=w_exp_gate, w_exp_up=w_exp_up, w_exp_down=w_exp_down, w_mem_q=w_mem_q, w_mem_o=w_mem_o)
    bp = x_prompt.shape[0]
    dtype = x_prompt.dtype
    mem_k_p = jnp.einsum('bmd,lde->lbme', mem_prompt, w_mem_k).reshape(DEPTH, bp, N_MEM, MEM_HEADS, MEM_HEAD_DIM)
    mem_v_p = jnp.einsum('bmd,lde->lbme', mem_prompt, w_mem_v).reshape(DEPTH, bp, N_MEM, MEM_HEADS, MEM_HEAD_DIM)
    ssm0 = jnp.zeros((N_EVEN, bp, SSM_HEADS, SSM_HEAD_DIM, SSM_STATE), dtype)
    ssmconv0 = jnp.zeros((N_EVEN, bp, SSM_CONV - 1, SSM_CONV_DIM), dtype)
    sconv0 = jnp.zeros((N_ODD, bp, SC_CONV - 1, SC_DIM), dtype)
    y_prompt, ssm_p, ssmconv_p, sconv_p, _ = trunk(x_prompt, mem_k_p, mem_v_p, ssm0, ssmconv0, sconv0, CHUNK, p)
    y_sample, ssm_s, ssmconv_s, sconv_s, v_s = trunk(x_sample, cache_mem_k, cache_mem_v, state_ssm, state_ssm_conv,
                                                     state_sconv, x_sample.shape[1], p)
    return (y_prompt, y_sample, ssm_p, ssm_s, ssmconv_p, ssmconv_s, sconv_p, sconv_s, mem_k_p, mem_v_p, v_s)
```

```python
import functools

import jax
import jax.numpy as jnp
from jax import lax
from jax.experimental import pallas as pl
from jax.experimental.pallas import tpu as pltpu

D_MODEL = 1024
EPS = 1e-6
CHUNK = 128
A_GROUPS = 8
A_HEAD = D_MODEL // A_GROUPS
SSM_HEADS = 16
SSM_HEAD_DIM = 64
SSM_GROUPS = 2
SSM_STATE = 128
SSM_CONV = 4
SSM_DIM = D_MODEL
SSM_CONV_DIM = SSM_DIM + 2 * SSM_GROUPS * SSM_STATE
N_MEM = 256
MEM_HEADS = 4
MEM_HEAD_DIM = D_MODEL // MEM_HEADS
N_EXPERTS = 8

LANES = 128
SUBLANES = 8
VMEM_LIMIT = 56 * 1024 * 1024

_BF = jnp.bfloat16
_F32 = jnp.float32
_HI = lax.Precision.HIGHEST
_NT = (((1,), (1,)), ((), ()))


def _cparams(n_axes):
    return pltpu.CompilerParams(dimension_semantics=("arbitrary",) * n_axes, vmem_limit_bytes=VMEM_LIMIT)


def _resident(shape):
    nd = len(shape)
    return pl.BlockSpec(shape, lambda *_: (0,) * nd, pipeline_mode=pl.Buffered(1))


def _rms(x, g):
    ms = jnp.mean(x * x, axis=-1, keepdims=True)
    return x * lax.rsqrt(ms + EPS) * g


def _silu(x):
    return x * jax.nn.sigmoid(x)


def _dot(a, b):
    return jnp.dot(a, b, preferred_element_type=_F32)


def _softplus(x):
    return jnp.maximum(x, 0.0) + jnp.log1p(jnp.exp(-jnp.abs(x)))


def _pos_and_seq(rows, seq):
    r = lax.broadcasted_iota(jnp.int32, (rows, 1), 0)
    if seq & (seq - 1) == 0:
        return r & (seq - 1), r >> (seq.bit_length() - 1)
    return lax.rem(r, seq), lax.div(r, seq)


def _kv_body(m_ref, wk_ref, wv_ref, k_ref, v_ref):
    m = m_ref[...].astype(_BF)
    k_ref[0] = _dot(m, wk_ref[0])
    v_ref[0] = _dot(m, wv_ref[0])


def _mem_kv(mem2d, wk, wv):
    n, d = mem2d.shape
    depth = wk.shape[0]
    out = jax.ShapeDtypeStruct((depth, n, d), _F32)
    wspec = pl.BlockSpec((1, d, d), lambda l: (l, 0, 0))
    ospec = pl.BlockSpec((1, n, d), lambda l: (l, 0, 0))
    return pl.pallas_call(
        _kv_body, out_shape=(out, out), grid=(depth,),
        in_specs=[pl.BlockSpec((n, d), lambda l: (0, 0)), wspec, wspec],
        out_specs=(ospec, ospec), compiler_params=_cparams(1), name="mem_kv",
    )(mem2d, wk, wv)


_PROJ_CHUNK = 512


def _inproj_ab_body(x_ref, g_ref, w_ref, wdt_ref, u_ref, v_ref, z_ref, xbc_ref, dt_ref):
    hn = _rms(x_ref[...], g_ref[...]).astype(_BF)
    col = 0
    for ref, width, act in ((u_ref, D_MODEL, True), (v_ref, D_MODEL, True), (z_ref, D_MODEL, False),
                            (xbc_ref, SSM_CONV_DIM, False)):
        for c in range(width // _PROJ_CHUNK):
            r = _dot(hn, w_ref[:, col:col + _PROJ_CHUNK])
            if act:
                r = jax.nn.gelu(r)
            ref[:, c * _PROJ_CHUNK:(c + 1) * _PROJ_CHUNK] = r.astype(ref.dtype)
            col += _PROJ_CHUNK
    dt_ref[...] = _dot(hn, wdt_ref[...])


def _inproj_ab(x2d, g, w_main, w_dt, act_dtype, tm):
    m = x2d.shape[0]
    row = lambda width: pl.BlockSpec((tm, width), lambda i: (i, 0))
    outs = (jax.ShapeDtypeStruct((m, D_MODEL), act_dtype),) * 3 + (
        jax.ShapeDtypeStruct((m, SSM_CONV_DIM), _F32), jax.ShapeDtypeStruct((m, LANES), _F32))
    return pl.pallas_call(
        _inproj_ab_body, out_shape=outs, grid=(m // tm,),
        in_specs=[row(D_MODEL), _resident(g.shape), _resident(w_main.shape), _resident(w_dt.shape)],
        out_specs=(row(D_MODEL), row(D_MODEL), row(D_MODEL), row(SSM_CONV_DIM), row(LANES)),
        compiler_params=_cparams(1), name="inproj_ab",
    )(x2d, g, w_main, w_dt)


_PAIR = 2 * SSM_HEAD_DIM
_HPG = SSM_HEADS // SSM_GROUPS
_GRP_DIM = SSM_DIM // SSM_GROUPS


def _mix_ab_prompt_body(u_ref, v_ref, z_ref, xbc_ref, dt_ref, ws_ref, bs_ref, cw_ref, cb_ref, dtb_ref, alog_ref,
                        dskip_ref, gn_ref, y_ref, st_ref, cst_ref, xp_ref, sT_ref):
    q = CHUNK
    c = pl.program_id(1)

    @pl.when(c == 0)
    def _():
        xp_ref[0:SUBLANES, :] = jnp.zeros((SUBLANES, SSM_CONV_DIM), _F32)
        sT_ref[...] = jnp.zeros_like(sT_ref)

    tri = lax.broadcasted_iota(jnp.int32, (q, q), 0) >= lax.broadcasted_iota(jnp.int32, (q, q), 1)

    for g in range(A_GROUPS):
        sl = slice(g * A_HEAD, (g + 1) * A_HEAD)
        w = jnp.where(tri, ws_ref[g], 0.0).astype(_BF)
        s = _dot(w, v_ref[:, sl].astype(_BF)) + bs_ref[g]
        y_ref[:, sl] = (u_ref[:, sl].astype(_F32) * s).astype(y_ref.dtype)

    xbc = xbc_ref[...]
    xp_ref[SUBLANES:SUBLANES + q, :] = xbc
    base = SUBLANES - (SSM_CONV - 1)
    conv = xp_ref[base:base + q, :] * cw_ref[0:1, :]
    for k in range(1, SSM_CONV):
        conv = conv + xp_ref[base + k:base + k + q, :] * cw_ref[k:k + 1, :]
    xp_ref[0:SUBLANES, :] = xbc[q - SUBLANES:q, :]
    act = _silu(conv + cb_ref[...])
    xs = act[:, :SSM_DIM]
    bm = act[:, SSM_DIM:SSM_DIM + SSM_GROUPS * SSM_STATE]
    cm = act[:, SSM_DIM + SSM_GROUPS * SSM_STATE:]

    dt = _softplus(dt_ref[...] + dtb_ref[...])
    d_a = dt * (-jnp.exp(alog_ref[...]))
    la = jnp.dot(tri.astype(_F32), d_a, precision=_HI, preferred_element_type=_F32)
    la_t = la.T
    dt_t = dt.T
    e_la = jnp.exp(la)
    last = la[q - 1:q, :]
    wend = jnp.exp(last - la) * dt
    e_last = jnp.exp(last)
    lo = lax.broadcasted_iota(jnp.int32, (q, _PAIR), 1) < SSM_HEAD_DIM
    lo_row = lo[0:1, :]

    for g in range(SSM_GROUPS):
        bm_t = bm[:, g * SSM_STATE:(g + 1) * SSM_STATE].T.astype(_BF)
        cm_g = cm[:, g * SSM_STATE:(g + 1) * SSM_STATE].astype(_BF)
        cb = _dot(cm_g, bm_t)
        ys = []
        for pp in range(_HPG // 2):
            p = g * (_HPG // 2) + pp
            cols = slice(p * _PAIR, (p + 1) * _PAIR)
            wts = []
            for h in (2 * p, 2 * p + 1):
                seg = la[:, h:h + 1] - la_t[h:h + 1, :]
                wts.append(cb * jnp.exp(jnp.where(tri, seg, -jnp.inf)) * dt_t[h:h + 1, :])
            w2 = jnp.concatenate(wts, axis=1).astype(_BF)
            x_pair = xs[:, cols]
            x2 = jnp.concatenate([jnp.where(lo, x_pair, 0.0), jnp.where(lo, 0.0, x_pair)], axis=0).astype(_BF)
            s_t = sT_ref[:, cols]
            e_pair = jnp.where(lo, e_la[:, 2 * p:2 * p + 1], e_la[:, 2 * p + 1:2 * p + 2])
            y_pair = _dot(w2, x2) + _dot(cm_g, s_t.astype(_BF)) * e_pair
            wd_pair = jnp.where(lo, wend[:, 2 * p:2 * p + 1], wend[:, 2 * p + 1:2 * p + 2])
            el_pair = jnp.where(lo_row, e_last[:, 2 * p:2 * p + 1], e_last[:, 2 * p + 1:2 * p + 2])
            sT_ref[:, cols] = s_t * el_pair + _dot(bm_t, (x_pair * wd_pair).astype(_BF))
            ys.append(y_pair + x_pair * dskip_ref[:, cols])
        gcols = slice(g * _GRP_DIM, (g + 1) * _GRP_DIM)
        gated = jnp.concatenate(ys, axis=1) * _silu(z_ref[:, gcols].astype(_F32))
        y_ref[:, D_MODEL + g * _GRP_DIM:D_MODEL + (g + 1) * _GRP_DIM] = _rms(gated, gn_ref[:, gcols]).astype(y_ref.dtype)

    @pl.when(c == pl.num_programs(1) - 1)
    def _():
        for p in range(SSM_HEADS // 2):
            st_ref[0, p * _PAIR:(p + 1) * _PAIR, :] = sT_ref[:, p * _PAIR:(p + 1) * _PAIR].T
        cst_ref[0] = xp_ref[SUBLANES + q - (SSM_CONV - 1):SUBLANES + q, :]


def _mix_ab_prompt(u, v, z, xbc, dt, bsz, seq, ws, bs, cw, cb, dtb, alog, dskip, gn):
    nc = seq // CHUNK
    row = lambda width: pl.BlockSpec((CHUNK, width), lambda b, c: (b * nc + c, 0))
    small = [ws, bs, cw, cb, dtb, alog, dskip, gn]
    outs = (jax.ShapeDtypeStruct((bsz * seq, 2 * D_MODEL), _BF),
            jax.ShapeDtypeStruct((bsz, SSM_DIM, SSM_STATE), _F32),
            jax.ShapeDtypeStruct((bsz, SSM_CONV - 1, SSM_CONV_DIM), _F32))
    return pl.pallas_call(
        _mix_ab_prompt_body, out_shape=outs, grid=(bsz, nc),
        in_specs=[row(D_MODEL), row(D_MODEL), row(D_MODEL), row(SSM_CONV_DIM), row(LANES)]
        + [_resident(a.shape) for a in small],
        out_specs=(row(2 * D_MODEL),
                   pl.BlockSpec((1, SSM_DIM, SSM_STATE), lambda b, c: (b, 0, 0)),
                   pl.BlockSpec((1, SSM_CONV - 1, SSM_CONV_DIM), lambda b, c: (b, 0, 0))),
        scratch_shapes=[pltpu.VMEM((SUBLANES + CHUNK, SSM_CONV_DIM), _F32), pltpu.VMEM((SSM_STATE, SSM_DIM), _F32)],
        compiler_params=_cparams(2), name="mix_ab_prompt",
    )(u, v, z, xbc, dt, *small)


_SEQ_BLK = 8


def _mix_ab_sample_body(seq, u_ref, v_ref, z_ref, xbc_ref, dt_ref, e_ref, st_ref, coef_ref, sb_ref, cw_ref, cb_ref,
                        dtb_ref, alog_ref, dskip_ref, gn_ref, y_ref, sto_ref, sh_ref):
    rows = _SEQ_BLK * seq
    t, seq_of_row = _pos_and_seq(rows, seq)
    pad_rows = sh_ref.shape[0]
    sh_ref[...] = jnp.zeros((pad_rows, SSM_CONV_DIM), _F32)

    def shifts(val, deltas):
        width = val.shape[1]
        sh_ref[SUBLANES:SUBLANES + rows, 0:width] = val
        return [sh_ref[SUBLANES - d:SUBLANES - d + rows, 0:width] for d in deltas]

    back = tuple(range(1, seq))

    v = v_ref[...]
    s = coef_ref[0] * v + sb_ref[...]
    for d, vs in zip(back, shifts(v, back)):
        s = s + coef_ref[d] * jnp.where(t >= d, vs, 0.0)
    y_ref[:, 0:D_MODEL] = u_ref[...] * s

    xbc = xbc_ref[...]
    conv = xbc * cw_ref[SSM_CONV - 1:SSM_CONV, :]
    for j, xsft in zip(range(1, SSM_CONV), shifts(xbc, tuple(range(1, SSM_CONV)))):
        conv = conv + (jnp.where(t >= j, xsft, 0.0) + e_ref[j - 1]) * cw_ref[SSM_CONV - 1 - j:SSM_CONV - j, :]
    act = _silu(conv + cb_ref[...])
    xs = act[:, :SSM_DIM]
    bm = act[:, SSM_DIM:SSM_DIM + SSM_GROUPS * SSM_STATE]
    cm = act[:, SSM_DIM + SSM_GROUPS * SSM_STATE:]

    dt = _softplus(dt_ref[...] + dtb_ref[...])
    d_a = dt * (-jnp.exp(alog_ref[...]))
    la = d_a
    for d, sft in zip(back, shifts(d_a, back)):
        la = la + jnp.where(t >= d, sft, 0.0)
    la_back = shifts(la, back)
    dt_back = shifts(dt, back)
    last = jnp.where(t == seq - 1, la, 0.0)
    fwd = shifts(la, tuple(-d for d in back))
    for d, sft in zip(back, fwd):
        last = last + jnp.where(t == seq - 1 - d, sft, 0.0)
    wend = jnp.exp(last - la) * dt
    e_last = jnp.exp(last)
    e_la = jnp.exp(la)

    head_lane = lax.broadcasted_iota(jnp.int32, (rows, LANES), 1)
    bm_back = shifts(bm, back)

    def cb_heads(b_other):
        per_g = [jnp.sum(cm[:, g * SSM_STATE:(g + 1) * SSM_STATE] * b_other[:, g * SSM_STATE:(g + 1) * SSM_STATE],
                         axis=-1, keepdims=True) for g in range(SSM_GROUPS)]
        return jnp.where(head_lane < _HPG, per_g[0], per_g[1])

    gs = [cb_heads(bm) * dt]
    for i, d in enumerate(back):
        gd = cb_heads(bm_back[i]) * jnp.exp(la - la_back[i]) * dt_back[i]
        gs.append(jnp.where(t >= d, gd, 0.0))

    head0 = lax.broadcasted_iota(jnp.int32, (LANES, SSM_DIM), 0) * SSM_HEAD_DIM
    chan = lax.broadcasted_iota(jnp.int32, (LANES, SSM_DIM), 1)
    expand = ((chan >= head0) & (chan < head0 + SSM_HEAD_DIM)).astype(_F32)
    stacked = jnp.concatenate([e_la, wend, e_last] + gs, axis=0)
    full = jnp.dot(stacked, expand, precision=_HI, preferred_element_type=_F32)
    e_la_f, wend_f, e_last_f = full[0:rows], full[rows:2 * rows], full[2 * rows:3 * rows]
    y2 = full[3 * rows:4 * rows] * xs
    for i, (d, xsft) in enumerate(zip(back, shifts(xs, back))):
        y2 = y2 + full[(4 + i) * rows:(5 + i) * rows] * jnp.where(t >= d, xsft, 0.0)

    zpad = jnp.zeros((LANES - rows, SSM_DIM), _F32)
    xw_t = jnp.concatenate([xs * wend_f, zpad], axis=0).T.astype(_BF)
    el_t = jnp.concatenate([e_last_f, zpad], axis=0).T
    row_lane = lax.broadcasted_iota(jnp.int32, (_GRP_DIM, LANES), 1)
    bpad = jnp.zeros((LANES - rows, SSM_STATE), _F32)

    y1 = [jnp.zeros((rows, _GRP_DIM), _F32) for _ in range(SSM_GROUPS)]
    for i in range(_SEQ_BLK):
        mine = seq_of_row == i
        for g in range(SSM_GROUPS):
            gr = slice(g * _GRP_DIM, (g + 1) * _GRP_DIM)
            st = st_ref[i, gr, :]
            b_g = jnp.where(mine, bm[:, g * SSM_STATE:(g + 1) * SSM_STATE], 0.0)
            b_g = jnp.concatenate([b_g, bpad], axis=0).astype(_BF)
            decay = jnp.sum(jnp.where(row_lane == i * seq, el_t[gr, :], 0.0), axis=1, keepdims=True)
            sto_ref[i, gr, :] = st * decay + _dot(xw_t[gr, :], b_g)
            c_g = cm[:, g * SSM_STATE:(g + 1) * SSM_STATE].astype(_BF)
            ch = lax.dot_general(c_g, st.astype(_BF), _NT, preferred_element_type=_F32)
            y1[g] = y1[g] + jnp.where(mine, ch, 0.0)

    for g in range(SSM_GROUPS):
        gr = slice(g * _GRP_DIM, (g + 1) * _GRP_DIM)
        ys = y1[g] * e_la_f[:, gr] + y2[:, gr] + xs[:, gr] * dskip_ref[:, gr]
        gated = ys * _silu(z_ref[:, gr])
        y_ref[:, D_MODEL + g * _GRP_DIM:D_MODEL + (g + 1) * _GRP_DIM] = _rms(gated, gn_ref[:, gr])


def _mix_ab_sample(u, v, z, xbc, dt, e_buf, state, seq, coef, sb, cw, cb, dtb, alog, dskip, gn):
    bsz = state.shape[0]
    rows = _SEQ_BLK * seq
    row = lambda width: pl.BlockSpec((rows, width), lambda i: (i, 0))
    small = [coef, sb, cw, cb, dtb, alog, dskip, gn]
    st_spec = pl.BlockSpec((_SEQ_BLK, SSM_DIM, SSM_STATE), lambda i: (i, 0, 0))
    outs = (jax.ShapeDtypeStruct((bsz * seq, 2 * D_MODEL), _F32), jax.ShapeDtypeStruct(state.shape, _F32))
    return pl.pallas_call(
        functools.partial(_mix_ab_sample_body, seq), out_shape=outs, grid=(bsz // _SEQ_BLK,),
        in_specs=[row(D_MODEL), row(D_MODEL), row(D_MODEL), row(SSM_CONV_DIM), row(LANES),
                  pl.BlockSpec((SSM_CONV - 1, rows, SSM_CONV_DIM), lambda i: (0, i, 0)), st_spec]
        + [_resident(a.shape) for a in small],
        out_specs=(row(2 * D_MODEL), st_spec),
        scratch_shapes=[pltpu.VMEM((rows + 2 * SUBLANES, SSM_CONV_DIM), _F32)],
        compiler_params=_cparams(1), name="mix_ab_sample",
    )(u, v, z, xbc, dt, e_buf, state, *small)


def _out_q_body(a_ref, x_ref, wo_ref, gm_ref, wq_ref, xo_ref, q_ref):
    xn = x_ref[...] + _dot(a_ref[...].astype(_BF), wo_ref[...])
    xo_ref[...] = xn
    q_ref[...] = _dot(_rms(xn, gm_ref[...]).astype(_BF), wq_ref[...]).astype(q_ref.dtype)


def _out_q(a, x2d, wo, gm, wq, q_dtype, tm):
    m = x2d.shape[0]
    row = lambda width: pl.BlockSpec((tm, width), lambda i: (i, 0))
    outs = (jax.ShapeDtypeStruct((m, D_MODEL), _F32), jax.ShapeDtypeStruct((m, D_MODEL), q_dtype))
    return pl.pallas_call(
        _out_q_body, out_shape=outs, grid=(m // tm,),
        in_specs=[row(a.shape[1]), row(D_MODEL), _resident(wo.shape), _resident(gm.shape), _resident(wq.shape)],
        out_specs=(row(D_MODEL), row(D_MODEL)), compiler_params=_cparams(1), name="out_q",
    )(a, x2d, wo, gm, wq)


_ATT_SCALE = MEM_HEAD_DIM ** -0.5


def _attend(q, k, v):
    outs = []
    for h in range(MEM_HEADS):
        sl = slice(h * MEM_HEAD_DIM, (h + 1) * MEM_HEAD_DIM)
        s = lax.dot_general(q[:, sl], k[:, sl], _NT, preferred_element_type=_F32) * _ATT_SCALE
        p = jnp.exp(s - jnp.max(s, axis=-1, keepdims=True))
        p = p / jnp.sum(p, axis=-1, keepdims=True)
        outs.append(_dot(p.astype(_BF), v[:, sl]))
    return jnp.concatenate(outs, axis=1)


def _attn_prompt_body(q_ref, k_ref, v_ref, x_ref, wo_ref, o_ref):
    o = _attend(q_ref[...], k_ref[0].astype(_BF), v_ref[0].astype(_BF))
    o_ref[...] = x_ref[...] + _dot(o.astype(_BF), wo_ref[...])


def _attn_prompt(q, k, v, x2d, wo, layer, bsz, seq, tq):
    nt = seq // tq
    row = pl.BlockSpec((tq, D_MODEL), lambda b, j: (b * nt + j, 0))
    kv = pl.BlockSpec((1, N_MEM, D_MODEL), lambda b, j: (layer * bsz + b, 0, 0))
    return pl.pallas_call(
        _attn_prompt_body, out_shape=jax.ShapeDtypeStruct(x2d.shape, _F32), grid=(bsz, nt),
        in_specs=[row, kv, kv, row, _resident(wo.shape)], out_specs=row,
        compiler_params=_cparams(2), name="attn_prompt",
    )(q, k, v, x2d, wo)


def _attn_sample_body(seq, q_ref, k_ref, v_ref, x_ref, wo_ref, o_ref, o_scr):
    per_tile = SUBLANES // seq
    _, seq_of_row = _pos_and_seq(SUBLANES, seq)
    for tile in range(_SEQ_BLK // per_tile):
        rows = slice(tile * SUBLANES, (tile + 1) * SUBLANES)
        qt = q_ref[rows, :].astype(_BF)
        o = None
        for j in range(per_tile):
            i = tile * per_tile + j
            oi = _attend(qt, k_ref[i].astype(_BF), v_ref[i].astype(_BF))
            o = oi if o is None else jnp.where(seq_of_row == j, oi, o)
        o_scr[rows, :] = o
    o_ref[...] = x_ref[...] + _dot(o_scr[...].astype(_BF), wo_ref[...])


def _attn_sample(q, k, v, x2d, wo, layer, bsz, seq):
    rows = _SEQ_BLK * seq
    nb = bsz // _SEQ_BLK
    row = pl.BlockSpec((rows, D_MODEL), lambda i: (i, 0))
    kv = pl.BlockSpec((_SEQ_BLK, N_MEM, D_MODEL), lambda i: (layer * nb + i, 0, 0))
    return pl.pallas_call(
        functools.partial(_attn_sample_body, seq), out_shape=jax.ShapeDtypeStruct(x2d.shape, _F32), grid=(nb,),
        in_specs=[row, kv, kv, row, _resident(wo.shape)], out_specs=row,
        scratch_shapes=[pltpu.VMEM((rows, D_MODEL), _F32)],
        compiler_params=_cparams(1), name="attn_sample",
    )(q, k, v, x2d, wo)


_FF_GRID_CHUNKS = 2
_FF_SUB = 512


def _ffn_body(routed, final_norm, x_ref, g_ref, wr_ref, wg_ref, wu_ref, wd_ref, gf_ref, o_ref, hf_ref, comb_ref,
              acc_ref):
    e = pl.program_id(1)
    c = pl.program_id(2)
    rows = x_ref.shape[0]
    lane = lax.broadcasted_iota(jnp.int32, (rows, LANES), 1)

    @pl.when((e == 0) & (c == 0))
    def _():
        hf = _rms(x_ref[...], g_ref[...])
        hf_ref[...] = hf.astype(_BF)
        acc_ref[...] = jnp.zeros_like(acc_ref)
        if routed:
            logits = jnp.dot(hf, wr_ref[...], precision=_HI, preferred_element_type=_F32)
            lg = jnp.where(lane < N_EXPERTS, logits, -jnp.inf)
            v1 = jnp.max(lg, axis=-1, keepdims=True)
            i1 = jnp.min(jnp.where(lg == v1, lane, LANES), axis=-1, keepdims=True)
            lg2 = jnp.where(lane == i1, -jnp.inf, lg)
            v2 = jnp.max(lg2, axis=-1, keepdims=True)
            i2 = jnp.min(jnp.where(lg2 == v2, lane, LANES), axis=-1, keepdims=True)
            e2 = jnp.exp(v2 - v1)
            comb_ref[...] = jnp.where(lane == i1, 1.0 / (1.0 + e2), 0.0) + jnp.where(lane == i2, e2 / (1.0 + e2), 0.0)

    hf = hf_ref[...]
    width = wg_ref.shape[2]
    out = None
    for lo in range(0, width, _FF_SUB):
        hi = min(lo + _FF_SUB, width)
        hid = _silu(_dot(hf, wg_ref[0, :, lo:hi])) * _dot(hf, wu_ref[0, :, lo:hi])
        part = _dot(hid.astype(_BF), wd_ref[0, lo:hi, :])
        out = part if out is None else out + part
    if routed:
        out = out * jnp.sum(jnp.where(lane == e, comb_ref[...], 0.0), axis=-1, keepdims=True)
    acc_ref[...] += out

    @pl.when((e == pl.num_programs(1) - 1) & (c == pl.num_programs(2) - 1))
    def _():
        y = x_ref[...] + acc_ref[...]
        if final_norm:
            y = _rms(y, gf_ref[...])
        o_ref[...] = y


def _ffn(x2d, g, wr, wg, wu, wd, gf, routed, final_norm, tm):
    m = x2d.shape[0]
    n_e, _, d_ff = wg.shape
    fc = d_ff // _FF_GRID_CHUNKS
    row = pl.BlockSpec((tm, D_MODEL), lambda i, e, c: (i, 0))
    up = pl.BlockSpec((1, D_MODEL, fc), lambda i, e, c: (e, 0, c))
    down = pl.BlockSpec((1, fc, D_MODEL), lambda i, e, c: (e, c, 0))
    return pl.pallas_call(
        functools.partial(_ffn_body, routed, final_norm), out_shape=jax.ShapeDtypeStruct(x2d.shape, _F32),
        grid=(m // tm, n_e, _FF_GRID_CHUNKS),
        in_specs=[row, _resident(g.shape), _resident(wr.shape), up, up, down, _resident(gf.shape)], out_specs=row,
        scratch_shapes=[pltpu.VMEM((tm, D_MODEL), _BF), pltpu.VMEM((tm, LANES), _F32), pltpu.VMEM((tm, D_MODEL), _F32)],
        compiler_params=_cparams(3), name="ffn_routed" if routed else "ffn_dense",
    )(x2d, g, wr, wg, wu, wd, gf)


_SC_CONV = 3


def _mix_c_body(per_seq, seq, x_ref, g_ref, win_ref, cw_ref, e_ref, wout_ref, gm_ref, wq_ref, xo_ref, q_ref, p_ref,
                pp_ref):
    tm = x_ref.shape[0]
    x = x_ref[...]
    hn = _rms(x, g_ref[...]).astype(_BF)
    bg = _dot(hn, win_ref[:, 0:D_MODEL])
    p = _dot(hn, win_ref[:, D_MODEL:2 * D_MODEL]) * _dot(hn, win_ref[:, 2 * D_MODEL:3 * D_MODEL])
    if per_seq:
        @pl.when(pl.program_id(1) == 0)
        def _():
            pp_ref[0:SUBLANES, :] = jnp.zeros((SUBLANES, D_MODEL), _F32)
    else:
        pp_ref[0:SUBLANES, :] = jnp.zeros((SUBLANES, D_MODEL), _F32)
    pp_ref[SUBLANES:SUBLANES + tm, :] = p
    p1 = pp_ref[SUBLANES - 1:SUBLANES - 1 + tm, :]
    p2 = pp_ref[SUBLANES - 2:SUBLANES - 2 + tm, :]
    if per_seq:
        pp_ref[0:SUBLANES, :] = p[tm - SUBLANES:tm, :]
    else:
        t, _ = _pos_and_seq(tm, seq)
        p1 = jnp.where(t >= 1, p1, 0.0) + e_ref[0]
        p2 = jnp.where(t >= 2, p2, 0.0) + e_ref[1]
    conv = p2 * cw_ref[0:1, :] + p1 * cw_ref[1:2, :] + p * cw_ref[2:3, :]
    xn = x + _dot((bg * conv).astype(_BF), wout_ref[...])
    xo_ref[...] = xn
    q_ref[...] = _dot(_rms(xn, gm_ref[...]).astype(_BF), wq_ref[...]).astype(q_ref.dtype)
    if per_seq:
        @pl.when(pl.program_id(1) == pl.num_programs(1) - 1)
        def _():
            p_ref[0] = pp_ref[SUBLANES + tm - (_SC_CONV - 1):SUBLANES + tm, :]
    else:
        p_ref[...] = p


def _mix_c(x2d, g, win, cw, e_buf, wout, gm, wq, q_dtype, per_seq, bsz, seq, tm):
    m = x2d.shape[0]
    if per_seq:
        nt = seq // tm
        grid = (bsz, nt)
        imap = lambda b, j: (b * nt + j, 0)
        e_spec = pl.BlockSpec((_SC_CONV - 1, SUBLANES, D_MODEL), lambda b, j: (0, 0, 0))
        p_shape = jax.ShapeDtypeStruct((bsz, _SC_CONV - 1, D_MODEL), _F32)
        p_spec = pl.BlockSpec((1, _SC_CONV - 1, D_MODEL), lambda b, j: (b, 0, 0))
    else:
        grid = (m // tm, 1)
        imap = lambda i, j: (i, 0)
        e_spec = pl.BlockSpec((_SC_CONV - 1, tm, D_MODEL), lambda i, j: (0, i, 0))
        p_shape = jax.ShapeDtypeStruct((m, D_MODEL), _F32)
        p_spec = pl.BlockSpec((tm, D_MODEL), imap)
    row = pl.BlockSpec((tm, D_MODEL), imap)
    outs = (jax.ShapeDtypeStruct((m, D_MODEL), _F32), jax.ShapeDtypeStruct((m, D_MODEL), q_dtype), p_shape)
    return pl.pallas_call(
        functools.partial(_mix_c_body, per_seq, seq), out_shape=outs, grid=grid,
        in_specs=[row, _resident(g.shape), _resident(win.shape), _resident(cw.shape), e_spec, _resident(wout.shape),
                  _resident(gm.shape), _resident(wq.shape)],
        out_specs=(row, row, p_spec),
        scratch_shapes=[pltpu.VMEM((SUBLANES + tm, D_MODEL), _F32)],
        compiler_params=_cparams(2), name="mix_c",
    )(x2d, g, win, cw, e_buf, wout, gm, wq)


def _expand_buf(buf, seq, n_back):
    bsz, km1, ch = buf.shape
    outs = []
    for j in range(1, n_back + 1):
        rows = [buf[:, km1 - j + t] if t < j else jnp.zeros((bsz, ch), buf.dtype) for t in range(seq)]
        outs.append(jnp.stack(rows, axis=1).reshape(bsz * seq, ch))
    return jnp.stack(outs)


def _row(vec, width=None):
    vec = vec.reshape(1, -1).astype(_F32)
    if width is not None and vec.shape[1] < width:
        vec = jnp.pad(vec, ((0, 0), (0, width - vec.shape[1])))
    return vec


def _trunk(x, k_all, v_all, prompt, states, w):
    bsz, seq, _ = x.shape
    m = bsz * seq
    x2d = x.reshape(m, D_MODEL)
    tm = min(512, m)
    dskip = _row(jnp.repeat(w["d_skip"][0], SSM_HEAD_DIM))
    common = (w["conv_w_ssm"][0], _row(w["conv_b_ssm"][0]), _row(w["dt_bias"][0], LANES), _row(w["a_log"][0], LANES),
              dskip, _row(w["g_ssm_norm"][0]))

    act_dtype = _BF if prompt else _F32
    u, v, z, xbc, dt = _inproj_ab(x2d, _row(w["g_mix"][0]), w["w_in_main"], w["w_in_dt"], act_dtype, tm)
    if prompt:
        y_ab, ssm_state, conv_state = _mix_ab_prompt(
            u, v, z, xbc, dt, bsz, seq, w["w_spatial"][0], w["b_spatial"][0][:, :, None], *common)
        v_out = None
    else:
        ws = w["w_spatial"][0][:, :seq, :seq]
        coef = []
        for d in range(seq):
            per_t = [ws[:, t, t - d] if t >= d else jnp.zeros((A_GROUPS,), _F32) for t in range(seq)]
            coef.append(jnp.tile(jnp.repeat(jnp.stack(per_t), A_HEAD, axis=1), (_SEQ_BLK, 1)))
        sb = jnp.tile(jnp.repeat(w["b_spatial"][0][:, :seq].T, A_HEAD, axis=1), (_SEQ_BLK, 1))
        e_buf = _expand_buf(states["ssm_conv"], seq, SSM_CONV - 1)
        y_ab, ssm_state = _mix_ab_sample(u, v, z, xbc, dt, e_buf, states["ssm"].reshape(bsz, SSM_DIM, SSM_STATE), seq,
                                         jnp.stack(coef), sb, *common)
        conv_state = xbc.reshape(bsz, seq, SSM_CONV_DIM)[:, seq - (SSM_CONV - 1):]
        v_out = v.reshape(bsz, seq, D_MODEL)
    x2d, q = _out_q(y_ab, x2d, w["w_out_ab"], _row(w["g_mem"][0]), w["w_mem_q"][0], act_dtype, tm)

    def attend(q, x2d, layer):
        if prompt:
            return _attn_prompt(q, k_all, v_all, x2d, w["w_mem_o"][layer], layer, bsz, seq, tm)
        return _attn_sample(q, k_all, v_all, x2d, w["w_mem_o"][layer], layer, bsz, seq)

    x2d = attend(q, x2d, 0)
    tf = min(512, m)
    dummy_r = jnp.zeros((D_MODEL, LANES), _F32)
    x2d = _ffn(x2d, _row(w["g_ffn"][0]), dummy_r, w["w_ffn_gate"], w["w_ffn_up"], w["w_ffn_down"], _row(w["g_final"]),
               False, False, tf)

    if prompt:
        e_buf = jnp.zeros((_SC_CONV - 1, SUBLANES, D_MODEL), _F32)
        x2d, q, sconv_state = _mix_c(x2d, _row(w["g_mix"][1]), w["w_in_c"], w["conv_w_c"][0], e_buf, w["w_out_c"],
                                     _row(w["g_mem"][1]), w["w_mem_q"][1], act_dtype, True, bsz, seq, tm)
    else:
        e_buf = _expand_buf(states["sconv"], seq, _SC_CONV - 1)
        x2d, q, p = _mix_c(x2d, _row(w["g_mix"][1]), w["w_in_c"], w["conv_w_c"][0], e_buf, w["w_out_c"],
                           _row(w["g_mem"][1]), w["w_mem_q"][1], act_dtype, False, bsz, seq, tm)
        sconv_state = p.reshape(bsz, seq, D_MODEL)[:, seq - (_SC_CONV - 1):]
    x2d = attend(q, x2d, 1)
    x2d = _ffn(x2d, _row(w["g_ffn"][1]), w["w_router"], w["w_exp_gate"], w["w_exp_up"], w["w_exp_down"],
               _row(w["g_final"]), True, True, tf)
    y = x2d.reshape(bsz, seq, D_MODEL)
    ssm_state = ssm_state.reshape(1, bsz, SSM_HEADS, SSM_HEAD_DIM, SSM_STATE)
    return y, ssm_state, conv_state[None], sconv_state[None], v_out


def kernel(x_prompt, x_sample, mem_prompt, state_ssm, state_ssm_conv, state_sconv, cache_mem_k, cache_mem_v, g_mix, g_mem, g_ffn, g_final, w_in_ab, w_spatial, b_spatial, conv_w_ssm, conv_b_ssm, dt_bias, a_log, d_skip, g_ssm_norm, w_out_ab, w_ffn_gate, w_ffn_up, w_ffn_down, w_in_c, conv_w_c, w_out_c, w_router, w_exp_gate, w_exp_up, w_exp_down, w_mem_q, w_mem_k, w_mem_v, w_mem_o):
    depth = w_mem_q.shape[0]
    bp = x_prompt.shape[0]
    bs = x_sample.shape[0]
    n_main = 3 * D_MODEL + SSM_CONV_DIM
    bf = lambda a: a.astype(_BF)
    w = dict(
        g_mix=g_mix, g_mem=g_mem, g_ffn=g_ffn, g_final=g_final, w_spatial=w_spatial, b_spatial=b_spatial,
        conv_w_ssm=conv_w_ssm, conv_b_ssm=conv_b_ssm, dt_bias=dt_bias, a_log=a_log, d_skip=d_skip, g_ssm_norm=g_ssm_norm,
        conv_w_c=conv_w_c,
        w_in_main=bf(w_in_ab[0][:, :n_main]),
        w_in_dt=bf(jnp.pad(w_in_ab[0][:, n_main:], ((0, 0), (0, LANES - SSM_HEADS)))),
        w_out_ab=bf(w_out_ab[0]), w_ffn_gate=bf(w_ffn_gate), w_ffn_up=bf(w_ffn_up), w_ffn_down=bf(w_ffn_down),
        w_in_c=bf(w_in_c[0]), w_out_c=bf(w_out_c[0]),
        w_router=jnp.pad(w_router[0], ((0, 0), (0, LANES - N_EXPERTS))),
        w_exp_gate=bf(w_exp_gate[0]), w_exp_up=bf(w_exp_up[0]), w_exp_down=bf(w_exp_down[0]),
        w_mem_q=bf(w_mem_q), w_mem_o=bf(w_mem_o),
    )
    mem_k_p, mem_v_p = _mem_kv(mem_prompt.reshape(bp * N_MEM, D_MODEL), bf(w_mem_k), bf(w_mem_v))
    y_p, ssm_p, ssmconv_p, sconv_p, _ = _trunk(
        x_prompt, mem_k_p.reshape(depth * bp, N_MEM, D_MODEL), mem_v_p.reshape(depth * bp, N_MEM, D_MODEL), True, None, w)
    states = dict(ssm=state_ssm[0], ssm_conv=state_ssm_conv[0], sconv=state_sconv[0])
    y_s, ssm_s, ssmconv_s, sconv_s, v_s = _trunk(
        x_sample, cache_mem_k.reshape(depth * bs, N_MEM, D_MODEL), cache_mem_v.reshape(depth * bs, N_MEM, D_MODEL),
        False, states, w)
    kv_shape = (depth, bp, N_MEM, MEM_HEADS, MEM_HEAD_DIM)
    return (y_p, y_s, ssm_p, ssm_s, ssmconv_p, ssmconv_s, sconv_p, sconv_s,
            mem_k_p.reshape(kv_shape), mem_v_p.reshape(kv_shape), v_s[None])
```

```python
import functools

import jax
import jax.numpy as jnp
from jax import lax
from jax.experimental import pallas as pl
from jax.experimental.pallas import tpu as pltpu

D_MODEL = 1024
EPS = 1e-6
CHUNK = 128
A_GROUPS = 8
A_HEAD = D_MODEL // A_GROUPS
SSM_HEADS = 16
SSM_HEAD_DIM = 64
SSM_GROUPS = 2
SSM_STATE = 128
SSM_CONV = 4
SSM_DIM = D_MODEL
SSM_CONV_DIM = SSM_DIM + 2 * SSM_GROUPS * SSM_STATE
N_MEM = 256
MEM_HEADS = 4
MEM_HEAD_DIM = D_MODEL // MEM_HEADS
N_EXPERTS = 8

LANES = 128
SUBLANES = 8
VMEM_LIMIT = 56 * 1024 * 1024

_BF = jnp.bfloat16
_F32 = jnp.float32
_HI = lax.Precision.HIGHEST
_NT = (((1,), (1,)), ((), ()))


def _cparams(n_axes):
    return pltpu.CompilerParams(dimension_semantics=("arbitrary",) * n_axes, vmem_limit_bytes=VMEM_LIMIT)


def _resident(shape):
    nd = len(shape)
    return pl.BlockSpec(shape, lambda *_: (0,) * nd, pipeline_mode=pl.Buffered(1))


def _rms(x, g):
    ms = jnp.mean(x * x, axis=-1, keepdims=True)
    return x * lax.rsqrt(ms + EPS) * g


def _silu(x):
    return x * jax.nn.sigmoid(x)


def _dot(a, b):
    return jnp.dot(a, b, preferred_element_type=_F32)


def _softplus(x):
    return jnp.maximum(x, 0.0) + jnp.log1p(jnp.exp(-jnp.abs(x)))


def _pos_and_seq(rows, seq):
    r = lax.broadcasted_iota(jnp.int32, (rows, 1), 0)
    if seq & (seq - 1) == 0:
        return r & (seq - 1), r >> (seq.bit_length() - 1)
    return lax.rem(r, seq), lax.div(r, seq)


def _kv_body(m_ref, wk_ref, wv_ref, k_ref, v_ref):
    m = m_ref[...].astype(_BF)
    k_ref[0] = _dot(m, wk_ref[0])
    v_ref[0] = _dot(m, wv_ref[0])


def _mem_kv(mem2d, wk, wv):
    n, d = mem2d.shape
    depth = wk.shape[0]
    out = jax.ShapeDtypeStruct((depth, n, d), _F32)
    wspec = pl.BlockSpec((1, d, d), lambda l: (l, 0, 0))
    ospec = pl.BlockSpec((1, n, d), lambda l: (l, 0, 0))
    return pl.pallas_call(
        _kv_body, out_shape=(out, out), grid=(depth,),
        in_specs=[pl.BlockSpec((n, d), lambda l: (0, 0)), wspec, wspec],
        out_specs=(ospec, ospec), compiler_params=_cparams(1), name="mem_kv",
    )(mem2d, wk, wv)


_PROJ_CHUNK = 512


def _inproj_ab_body(x_ref, g_ref, w_ref, wdt_ref, u_ref, v_ref, z_ref, xbc_ref, dt_ref):
    hn = _rms(x_ref[...], g_ref[...]).astype(_BF)
    col = 0
    for ref, width, act in ((u_ref, D_MODEL, True), (v_ref, D_MODEL, True), (z_ref, D_MODEL, False),
                            (xbc_ref, SSM_CONV_DIM, False)):
        for c in range(width // _PROJ_CHUNK):
            r = _dot(hn, w_ref[:, col:col + _PROJ_CHUNK])
            if act:
                r = jax.nn.gelu(r)
            ref[:, c * _PROJ_CHUNK:(c + 1) * _PROJ_CHUNK] = r.astype(ref.dtype)
            col += _PROJ_CHUNK
    dt_ref[...] = _dot(hn, wdt_ref[...])


def _inproj_ab(x2d, g, w_main, w_dt, act_dtype, tm):
    m = x2d.shape[0]
    row = lambda width: pl.BlockSpec((tm, width), lambda i: (i, 0))
    outs = (jax.ShapeDtypeStruct((m, D_MODEL), act_dtype),) * 3 + (
        jax.ShapeDtypeStruct((m, SSM_CONV_DIM), _F32), jax.ShapeDtypeStruct((m, LANES), _F32))
    return pl.pallas_call(
        _inproj_ab_body, out_shape=outs, grid=(m // tm,),
        in_specs=[row(D_MODEL), _resident(g.shape), _resident(w_main.shape), _resident(w_dt.shape)],
        out_specs=(row(D_MODEL), row(D_MODEL), row(D_MODEL), row(SSM_CONV_DIM), row(LANES)),
        compiler_params=_cparams(1), name="inproj_ab",
    )(x2d, g, w_main, w_dt)


_PAIR = 2 * SSM_HEAD_DIM
_HPG = SSM_HEADS // SSM_GROUPS
_GRP_DIM = SSM_DIM // SSM_GROUPS


def _mix_ab_prompt_body(u_ref, v_ref, z_ref, xbc_ref, dt_ref, ws_ref, bs_ref, cw_ref, cb_ref, dtb_ref, alog_ref,
                        dskip_ref, gn_ref, y_ref, st_ref, cst_ref, xp_ref, sT_ref):
    q = CHUNK
    c = pl.program_id(1)

    @pl.when(c == 0)
    def _():
        xp_ref[0:SUBLANES, :] = jnp.zeros((SUBLANES, SSM_CONV_DIM), _F32)
        sT_ref[...] = jnp.zeros_like(sT_ref)

    tri = lax.broadcasted_iota(jnp.int32, (q, q), 0) >= lax.broadcasted_iota(jnp.int32, (q, q), 1)

    for g in range(A_GROUPS):
        sl = slice(g * A_HEAD, (g + 1) * A_HEAD)
        w = jnp.where(tri, ws_ref[g], 0.0).astype(_BF)
        s = _dot(w, v_ref[:, sl].astype(_BF)) + bs_ref[g]
        y_ref[:, sl] = (u_ref[:, sl].astype(_F32) * s).astype(y_ref.dtype)

    xbc = xbc_ref[...]
    xp_ref[SUBLANES:SUBLANES + q, :] = xbc
    base = SUBLANES - (SSM_CONV - 1)
    conv = xp_ref[base:base + q, :] * cw_ref[0:1, :]
    for k in range(1, SSM_CONV):
        conv = conv + xp_ref[base + k:base + k + q, :] * cw_ref[k:k + 1, :]
    xp_ref[0:SUBLANES, :] = xbc[q - SUBLANES:q, :]
    act = _silu(conv + cb_ref[...])
    xs = act[:, :SSM_DIM]
    bm = act[:, SSM_DIM:SSM_DIM + SSM_GROUPS * SSM_STATE]
    cm = act[:, SSM_DIM + SSM_GROUPS * SSM_STATE:]

    dt = _softplus(dt_ref[...] + dtb_ref[...])
    d_a = dt * (-jnp.exp(alog_ref[...]))
    la = jnp.dot(tri.astype(_F32), d_a, precision=_HI, preferred_element_type=_F32)
    la_t = la.T
    dt_t = dt.T
    e_la = jnp.exp(la)
    last = la[q - 1:q, :]
    wend = jnp.exp(last - la) * dt
    e_last = jnp.exp(last)
    lo = lax.broadcasted_iota(jnp.int32, (q, _PAIR), 1) < SSM_HEAD_DIM
    lo_row = lo[0:1, :]

    for g in range(SSM_GROUPS):
        bm_t = bm[:, g * SSM_STATE:(g + 1) * SSM_STATE].T.astype(_BF)
        cm_g = cm[:, g * SSM_STATE:(g + 1) * SSM_STATE].astype(_BF)
        cb = _dot(cm_g, bm_t)
        ys = []
        for pp in range(_HPG // 2):
            p = g * (_HPG // 2) + pp
            cols = slice(p * _PAIR, (p + 1) * _PAIR)
            wts = []
            for h in (2 * p, 2 * p + 1):
                seg = la[:, h:h + 1] - la_t[h:h + 1, :]
                wts.append(cb * jnp.exp(jnp.where(tri, seg, -jnp.inf)) * dt_t[h:h + 1, :])
            w2 = jnp.concatenate(wts, axis=1).astype(_BF)
            x_pair = xs[:, cols]
            x2 = jnp.concatenate([jnp.where(lo, x_pair, 0.0), jnp.where(lo, 0.0, x_pair)], axis=0).astype(_BF)
            s_t = sT_ref[:, cols]
            e_pair = jnp.where(lo, e_la[:, 2 * p:2 * p + 1], e_la[:, 2 * p + 1:2 * p + 2])
            y_pair = _dot(w2, x2) + _dot(cm_g, s_t.astype(_BF)) * e_pair
            wd_pair = jnp.where(lo, wend[:, 2 * p:2 * p + 1], wend[:, 2 * p + 1:2 * p + 2])
            el_pair = jnp.where(lo_row, e_last[:, 2 * p:2 * p + 1], e_last[:, 2 * p + 1:2 * p + 2])
            sT_ref[:, cols] = s_t * el_pair + _dot(bm_t, (x_pair * wd_pair).astype(_BF))
            ys.append(y_pair + x_pair * dskip_ref[:, cols])
        gcols = slice(g * _GRP_DIM, (g + 1) * _GRP_DIM)
        gated = jnp.concatenate(ys, axis=1) * _silu(z_ref[:, gcols].astype(_F32))
        y_ref[:, D_MODEL + g * _GRP_DIM:D_MODEL + (g + 1) * _GRP_DIM] = _rms(gated, gn_ref[:, gcols]).astype(y_ref.dtype)

    @pl.when(c == pl.num_programs(1) - 1)
    def _():
        for p in range(SSM_HEADS // 2):
            st_ref[0, p * _PAIR:(p + 1) * _PAIR, :] = sT_ref[:, p * _PAIR:(p + 1) * _PAIR].T
        cst_ref[0] = xp_ref[SUBLANES + q - (SSM_CONV - 1):SUBLANES + q, :]


def _mix_ab_prompt(u, v, z, xbc, dt, bsz, seq, ws, bs, cw, cb, dtb, alog, dskip, gn):
    nc = seq // CHUNK
    row = lambda width: pl.BlockSpec((CHUNK, width), lambda b, c: (b * nc + c, 0))
    small = [ws, bs, cw, cb, dtb, alog, dskip, gn]
    outs = (jax.ShapeDtypeStruct((bsz * seq, 2 * D_MODEL), _BF),
            jax.ShapeDtypeStruct((bsz, SSM_DIM, SSM_STATE), _F32),
            jax.ShapeDtypeStruct((bsz, SSM_CONV - 1, SSM_CONV_DIM), _F32))
    return pl.pallas_call(
        _mix_ab_prompt_body, out_shape=outs, grid=(bsz, nc),
        in_specs=[row(D_MODEL), row(D_MODEL), row(D_MODEL), row(SSM_CONV_DIM), row(LANES)]
        + [_resident(a.shape) for a in small],
        out_specs=(row(2 * D_MODEL),
                   pl.BlockSpec((1, SSM_DIM, SSM_STATE), lambda b, c: (b, 0, 0)),
                   pl.BlockSpec((1, SSM_CONV - 1, SSM_CONV_DIM), lambda b, c: (b, 0, 0))),
        scratch_shapes=[pltpu.VMEM((SUBLANES + CHUNK, SSM_CONV_DIM), _F32), pltpu.VMEM((SSM_STATE, SSM_DIM), _F32)],
        compiler_params=_cparams(2), name="mix_ab_prompt",
    )(u, v, z, xbc, dt, *small)


_SEQ_BLK = 8


def _mix_ab_sample_body(seq, u_ref, v_ref, z_ref, xbc_ref, dt_ref, e_ref, st_ref, coef_ref, sb_ref, cw_ref, cb_ref,
                        dtb_ref, alog_ref, dskip_ref, gn_ref, y_ref, sto_ref, sh_ref):
    rows = _SEQ_BLK * seq
    t, seq_of_row = _pos_and_seq(rows, seq)
    pad_rows = sh_ref.shape[0]
    sh_ref[...] = jnp.zeros((pad_rows, SSM_CONV_DIM), _F32)

    def shifts(val, deltas):
        width = val.shape[1]
        sh_ref[SUBLANES:SUBLANES + rows, 0:width] = val
        return [sh_ref[SUBLANES - d:SUBLANES - d + rows, 0:width] for d in deltas]

    back = tuple(range(1, seq))

    v = v_ref[...]
    s = coef_ref[0] * v + sb_ref[...]
    for d, vs in zip(back, shifts(v, back)):
        s = s + coef_ref[d] * jnp.where(t >= d, vs, 0.0)
    y_ref[:, 0:D_MODEL] = u_ref[...] * s

    xbc = xbc_ref[...]
    conv = xbc * cw_ref[SSM_CONV - 1:SSM_CONV, :]
    for j, xsft in zip(range(1, SSM_CONV), shifts(xbc, tuple(range(1, SSM_CONV)))):
        conv = conv + (jnp.where(t >= j, xsft, 0.0) + e_ref[j - 1]) * cw_ref[SSM_CONV - 1 - j:SSM_CONV - j, :]
    act = _silu(conv + cb_ref[...])
    xs = act[:, :SSM_DIM]
    bm = act[:, SSM_DIM:SSM_DIM + SSM_GROUPS * SSM_STATE]
    cm = act[:, SSM_DIM + SSM_GROUPS * SSM_STATE:]

    dt = _softplus(dt_ref[...] + dtb_ref[...])
    d_a = dt * (-jnp.exp(alog_ref[...]))
    la = d_a
    for d, sft in zip(back, shifts(d_a, back)):
        la = la + jnp.where(t >= d, sft, 0.0)
    la_back = shifts(la, back)
    dt_back = shifts(dt, back)
    last = jnp.where(t == seq - 1, la, 0.0)
    fwd = shifts(la, tuple(-d for d in back))
    for d, sft in zip(back, fwd):
        last = last + jnp.where(t == seq - 1 - d, sft, 0.0)
    wend = jnp.exp(last - la) * dt
    e_last = jnp.exp(last)
    e_la = jnp.exp(la)

    head_lane = lax.broadcasted_iota(jnp.int32, (rows, LANES), 1)
    bm_back = shifts(bm, back)

    def cb_heads(b_other):
        per_g = [jnp.sum(cm[:, g * SSM_STATE:(g + 1) * SSM_STATE] * b_other[:, g * SSM_STATE:(g + 1) * SSM_STATE],
                         axis=-1, keepdims=True) for g in range(SSM_GROUPS)]
        return jnp.where(head_lane < _HPG, per_g[0], per_g[1])

    gs = [cb_heads(bm) * dt]
    for i, d in enumerate(back):
        gd = cb_heads(bm_back[i]) * jnp.exp(la - la_back[i]) * dt_back[i]
        gs.append(jnp.where(t >= d, gd, 0.0))

    head0 = lax.broadcasted_iota(jnp.int32, (LANES, SSM_DIM), 0) * SSM_HEAD_DIM
    chan = lax.broadcasted_iota(jnp.int32, (LANES, SSM_DIM), 1)
    expand = ((chan >= head0) & (chan < head0 + SSM_HEAD_DIM)).astype(_F32)
    stacked = jnp.concatenate([e_la, wend, e_last] + gs, axis=0)
    full = jnp.dot(stacked, expand, precision=_HI, preferred_element_type=_F32)
    e_la_f, wend_f, e_last_f = full[0:rows], full[rows:2 * rows], full[2 * rows:3 * rows]
    y2 = full[3 * rows:4 * rows] * xs
    for i, (d, xsft) in enumerate(zip(back, shifts(xs, back))):
        y2 = y2 + full[(4 + i) * rows:(5 + i) * rows] * jnp.where(t >= d, xsft, 0.0)

    zpad = jnp.zeros((LANES - rows, SSM_DIM), _F32)
    xw_t = jnp.concatenate([xs * wend_f, zpad], axis=0).T.astype(_BF)
    el_t = jnp.concatenate([e_last_f, zpad], axis=0).T
    row_lane = lax.broadcasted_iota(jnp.int32, (_GRP_DIM, LANES), 1)
    bpad = jnp.zeros((LANES - rows, SSM_STATE), _F32)

    y1 = [jnp.zeros((rows, _GRP_DIM), _F32) for _ in range(SSM_GROUPS)]
    for i in range(_SEQ_BLK):
        mine = seq_of_row == i
        for g in range(SSM_GROUPS):
            gr = slice(g * _GRP_DIM, (g + 1) * _GRP_DIM)
            st = st_ref[i, gr, :]
            b_g = jnp.where(mine, bm[:, g * SSM_STATE:(g + 1) * SSM_STATE], 0.0)
            b_g = jnp.concatenate([b_g, bpad], axis=0).astype(_BF)
            decay = jnp.sum(jnp.where(row_lane == i * seq, el_t[gr, :], 0.0), axis=1, keepdims=True)
            sto_ref[i, gr, :] = st * decay + _dot(xw_t[gr, :], b_g)
            c_g = cm[:, g * SSM_STATE:(g + 1) * SSM_STATE].astype(_BF)
            ch = lax.dot_general(c_g, st.astype(_BF), _NT, preferred_element_type=_F32)
            y1[g] = y1[g] + jnp.where(mine, ch, 0.0)

    for g in range(SSM_GROUPS):
        gr = slice(g * _GRP_DIM, (g + 1) * _GRP_DIM)
        ys = y1[g] * e_la_f[:, gr] + y2[:, gr] + xs[:, gr] * dskip_ref[:, gr]
        gated = ys * _silu(z_ref[:, gr])
        y_ref[:, D_MODEL + g * _GRP_DIM:D_MODEL + (g + 1) * _GRP_DIM] = _rms(gated, gn_ref[:, gr])


def _mix_ab_sample(u, v, z, xbc, dt, e_buf, state, seq, coef, sb, cw, cb, dtb, alog, dskip, gn):
    bsz = state.shape[0]
    rows = _SEQ_BLK * seq
    row = lambda width: pl.BlockSpec((rows, width), lambda i: (i, 0))
    small = [coef, sb, cw, cb, dtb, alog, dskip, gn]
    st_spec = pl.BlockSpec((_SEQ_BLK, SSM_DIM, SSM_STATE), lambda i: (i, 0, 0))
    outs = (jax.ShapeDtypeStruct((bsz * seq, 2 * D_MODEL), _F32), jax.ShapeDtypeStruct(state.shape, _F32))
    return pl.pallas_call(
        functools.partial(_mix_ab_sample_body, seq), out_shape=outs, grid=(bsz // _SEQ_BLK,),
        in_specs=[row(D_MODEL), row(D_MODEL), row(D_MODEL), row(SSM_CONV_DIM), row(LANES),
                  pl.BlockSpec((SSM_CONV - 1, rows, SSM_CONV_DIM), lambda i: (0, i, 0)), st_spec]
        + [_resident(a.shape) for a in small],
        out_specs=(row(2 * D_MODEL), st_spec),
        scratch_shapes=[pltpu.VMEM((rows + 2 * SUBLANES, SSM_CONV_DIM), _F32)],
        compiler_params=_cparams(1), name="mix_ab_sample",
    )(u, v, z, xbc, dt, e_buf, state, *small)


def _out_q_body(a_ref, x_ref, wo_ref, gm_ref, wq_ref, xo_ref, q_ref):
    xn = x_ref[...] + _dot(a_ref[...].astype(_BF), wo_ref[...])
    xo_ref[...] = xn
    q_ref[...] = _dot(_rms(xn, gm_ref[...]).astype(_BF), wq_ref[...]).astype(q_ref.dtype)


def _out_q(a, x2d, wo, gm, wq, q_dtype, tm):
    m = x2d.shape[0]
    row = lambda width: pl.BlockSpec((tm, width), lambda i: (i, 0))
    outs = (jax.ShapeDtypeStruct((m, D_MODEL), _F32), jax.ShapeDtypeStruct((m, D_MODEL), q_dtype))
    return pl.pallas_call(
        _out_q_body, out_shape=outs, grid=(m // tm,),
        in_specs=[row(a.shape[1]), row(D_MODEL), _resident(wo.shape), _resident(gm.shape), _resident(wq.shape)],
        out_specs=(row(D_MODEL), row(D_MODEL)), compiler_params=_cparams(1), name="out_q",
    )(a, x2d, wo, gm, wq)


_ATT_SCALE = MEM_HEAD_DIM ** -0.5


def _attend(q, k, v):
    outs = []
    for h in range(MEM_HEADS):
        sl = slice(h * MEM_HEAD_DIM, (h + 1) * MEM_HEAD_DIM)
        s = lax.dot_general(q[:, sl], k[:, sl], _NT, preferred_element_type=_F32) * _ATT_SCALE
        p = jnp.exp(s - jnp.max(s, axis=-1, keepdims=True))
        p = p / jnp.sum(p, axis=-1, keepdims=True)
        outs.append(_dot(p.astype(_BF), v[:, sl]))
    return jnp.concatenate(outs, axis=1)


def _attn_prompt_body(q_ref, k_ref, v_ref, x_ref, wo_ref, o_ref):
    o = _attend(q_ref[...], k_ref[0].astype(_BF), v_ref[0].astype(_BF))
    o_ref[...] = x_ref[...] + _dot(o.astype(_BF), wo_ref[...])


def _attn_prompt(q, k, v, x2d, wo, layer, bsz, seq, tq):
    nt = seq // tq
    row = pl.BlockSpec((tq, D_MODEL), lambda b, j: (b * nt + j, 0))
    kv = pl.BlockSpec((1, N_MEM, D_MODEL), lambda b, j: (layer * bsz + b, 0, 0))
    return pl.pallas_call(
        _attn_prompt_body, out_shape=jax.ShapeDtypeStruct(x2d.shape, _F32), grid=(bsz, nt),
        in_specs=[row, kv, kv, row, _resident(wo.shape)], out_specs=row,
        compiler_params=_cparams(2), name="attn_prompt",
    )(q, k, v, x2d, wo)


def _attn_sample_body(seq, q_ref, k_ref, v_ref, x_ref, wo_ref, o_ref, o_scr):
    per_tile = SUBLANES // seq
    _, seq_of_row = _pos_and_seq(SUBLANES, seq)
    n_lt = MEM_HEAD_DIM // LANES
    rows_mh = N_MEM * MEM_HEADS
    col = lax.broadcasted_iota(jnp.int32, (MEM_HEADS * SUBLANES, rows_mh), 1)
    row = lax.broadcasted_iota(jnp.int32, (MEM_HEADS * SUBLANES, rows_mh), 0)
    same_head = (col & (MEM_HEADS - 1)) == (row >> (SUBLANES.bit_length() - 1))
    for tile in range(_SEQ_BLK // per_tile):
        q8 = q_ref[tile * SUBLANES:(tile + 1) * SUBLANES, :].astype(_BF)
        o = None
        for j in range(per_tile):
            i = tile * per_tile + j
            s = None
            for lt in range(n_lt):
                qh = jnp.concatenate([q8[:, h * MEM_HEAD_DIM + lt * LANES:h * MEM_HEAD_DIM + (lt + 1) * LANES]
                                      for h in range(MEM_HEADS)], axis=0)
                k_lt = k_ref[i, :, :, lt * LANES:(lt + 1) * LANES].reshape(rows_mh, LANES).astype(_BF)
                part = lax.dot_general(qh, k_lt, _NT, preferred_element_type=_F32)
                s = part if s is None else s + part
            s = jnp.where(same_head, s * _ATT_SCALE, -jnp.inf)
            p = jnp.exp(s - jnp.max(s, axis=-1, keepdims=True))
            p = (p / jnp.sum(p, axis=-1, keepdims=True)).astype(_BF)
            pieces = [None] * (MEM_HEADS * n_lt)
            for lt in range(n_lt):
                v_lt = v_ref[i, :, :, lt * LANES:(lt + 1) * LANES].reshape(rows_mh, LANES).astype(_BF)
                o_lt = _dot(p, v_lt)
                for h in range(MEM_HEADS):
                    pieces[h * n_lt + lt] = o_lt[h * SUBLANES:(h + 1) * SUBLANES, :]
            oi = jnp.concatenate(pieces, axis=1)
            o = oi if o is None else jnp.where(seq_of_row == j, oi, o)
        o_scr[tile * SUBLANES:(tile + 1) * SUBLANES, :] = o
    o_ref[...] = x_ref[...] + _dot(o_scr[...].astype(_BF), wo_ref[...])


def _attn_sample(q, k, v, x2d, wo, layer, bsz, seq):
    rows = _SEQ_BLK * seq
    nb = bsz // _SEQ_BLK
    row = pl.BlockSpec((rows, D_MODEL), lambda i: (i, 0))
    kv = pl.BlockSpec((_SEQ_BLK, N_MEM, MEM_HEADS, MEM_HEAD_DIM), lambda i: (layer * nb + i, 0, 0, 0))
    return pl.pallas_call(
        functools.partial(_attn_sample_body, seq), out_shape=jax.ShapeDtypeStruct(x2d.shape, _F32), grid=(nb,),
        in_specs=[row, kv, kv, row, _resident(wo.shape)], out_specs=row,
        scratch_shapes=[pltpu.VMEM((rows, D_MODEL), _F32)],
        compiler_params=_cparams(1), name="attn_sample",
    )(q, k, v, x2d, wo)


_FF_GRID_CHUNKS = 2
_FF_SUB = 512


def _ffn_body(routed, final_norm, x_ref, g_ref, wr_ref, wg_ref, wu_ref, wd_ref, gf_ref, o_ref, hf_ref, comb_ref,
              acc_ref):
    e = pl.program_id(1)
    c = pl.program_id(2)
    rows = x_ref.shape[0]
    lane = lax.broadcasted_iota(jnp.int32, (rows, LANES), 1)

    @pl.when((e == 0) & (c == 0))
    def _():
        hf = _rms(x_ref[...], g_ref[...])
        hf_ref[...] = hf.astype(_BF)
        acc_ref[...] = jnp.zeros_like(acc_ref)
        if routed:
            logits = jnp.dot(hf, wr_ref[...], precision=_HI, preferred_element_type=_F32)
            lg = jnp.where(lane < N_EXPERTS, logits, -jnp.inf)
            v1 = jnp.max(lg, axis=-1, keepdims=True)
            i1 = jnp.min(jnp.where(lg == v1, lane, LANES), axis=-1, keepdims=True)
            lg2 = jnp.where(lane == i1, -jnp.inf, lg)
            v2 = jnp.max(lg2, axis=-1, keepdims=True)
            i2 = jnp.min(jnp.where(lg2 == v2, lane, LANES), axis=-1, keepdims=True)
            e2 = jnp.exp(v2 - v1)
            comb_ref[...] = jnp.where(lane == i1, 1.0 / (1.0 + e2), 0.0) + jnp.where(lane == i2, e2 / (1.0 + e2), 0.0)

    hf = hf_ref[...]
    width = wg_ref.shape[2]
    out = None
    for lo in range(0, width, _FF_SUB):
        hi = min(lo + _FF_SUB, width)
        hid = _silu(_dot(hf, wg_ref[0, :, lo:hi])) * _dot(hf, wu_ref[0, :, lo:hi])
        part = _dot(hid.astype(_BF), wd_ref[0, lo:hi, :])
        out = part if out is None else out + part
    if routed:
        out = out * jnp.sum(jnp.where(lane == e, comb_ref[...], 0.0), axis=-1, keepdims=True)
    acc_ref[...] += out

    @pl.when((e == pl.num_programs(1) - 1) & (c == pl.num_programs(2) - 1))
    def _():
        y = x_ref[...] + acc_ref[...]
        if final_norm:
            y = _rms(y, gf_ref[...])
        o_ref[...] = y


def _ffn(x2d, g, wr, wg, wu, wd, gf, routed, final_norm, tm):
    m = x2d.shape[0]
    n_e, _, d_ff = wg.shape
    fc = d_ff // _FF_GRID_CHUNKS
    row = pl.BlockSpec((tm, D_MODEL), lambda i, e, c: (i, 0))
    up = pl.BlockSpec((1, D_MODEL, fc), lambda i, e, c: (e, 0, c))
    down = pl.BlockSpec((1, fc, D_MODEL), lambda i, e, c: (e, c, 0))
    return pl.pallas_call(
        functools.partial(_ffn_body, routed, final_norm), out_shape=jax.ShapeDtypeStruct(x2d.shape, _F32),
        grid=(m // tm, n_e, _FF_GRID_CHUNKS),
        in_specs=[row, _resident(g.shape), _resident(wr.shape), up, up, down, _resident(gf.shape)], out_specs=row,
        scratch_shapes=[pltpu.VMEM((tm, D_MODEL), _BF), pltpu.VMEM((tm, LANES), _F32), pltpu.VMEM((tm, D_MODEL), _F32)],
        compiler_params=_cparams(3), name="ffn_routed" if routed else "ffn_dense",
    )(x2d, g, wr, wg, wu, wd, gf)


_SC_CONV = 3


def _mix_c_body(per_seq, seq, x_ref, g_ref, win_ref, cw_ref, e_ref, wout_ref, gm_ref, wq_ref, xo_ref, q_ref, p_ref,
                pp_ref):
    tm = x_ref.shape[0]
    x = x_ref[...]
    hn = _rms(x, g_ref[...]).astype(_BF)
    bg = _dot(hn, win_ref[:, 0:D_MODEL])
    p = _dot(hn, win_ref[:, D_MODEL:2 * D_MODEL]) * _dot(hn, win_ref[:, 2 * D_MODEL:3 * D_MODEL])
    if per_seq:
        @pl.when(pl.program_id(1) == 0)
        def _():
            pp_ref[0:SUBLANES, :] = jnp.zeros((SUBLANES, D_MODEL), _F32)
    else:
        pp_ref[0:SUBLANES, :] = jnp.zeros((SUBLANES, D_MODEL), _F32)
    pp_ref[SUBLANES:SUBLANES + tm, :] = p
    p1 = pp_ref[SUBLANES - 1:SUBLANES - 1 + tm, :]
    p2 = pp_ref[SUBLANES - 2:SUBLANES - 2 + tm, :]
    if per_seq:
        pp_ref[0:SUBLANES, :] = p[tm - SUBLANES:tm, :]
    else:
        t, _ = _pos_and_seq(tm, seq)
        p1 = jnp.where(t >= 1, p1, 0.0) + e_ref[0]
        p2 = jnp.where(t >= 2, p2, 0.0) + e_ref[1]
    conv = p2 * cw_ref[0:1, :] + p1 * cw_ref[1:2, :] + p * cw_ref[2:3, :]
    xn = x + _dot((bg * conv).astype(_BF), wout_ref[...])
    xo_ref[...] = xn
    q_ref[...] = _dot(_rms(xn, gm_ref[...]).astype(_BF), wq_ref[...]).astype(q_ref.dtype)
    if per_seq:
        @pl.when(pl.program_id(1) == pl.num_programs(1) - 1)
        def _():
            p_ref[0] = pp_ref[SUBLANES + tm - (_SC_CONV - 1):SUBLANES + tm, :]
    else:
        p_ref[...] = p


def _mix_c(x2d, g, win, cw, e_buf, wout, gm, wq, q_dtype, per_seq, bsz, seq, tm):
    m = x2d.shape[0]
    if per_seq:
        nt = seq // tm
        grid = (bsz, nt)
        imap = lambda b, j: (b * nt + j, 0)
        e_spec = pl.BlockSpec((_SC_CONV - 1, SUBLANES, D_MODEL), lambda b, j: (0, 0, 0))
        p_shape = jax.ShapeDtypeStruct((bsz, _SC_CONV - 1, D_MODEL), _F32)
        p_spec = pl.BlockSpec((1, _SC_CONV - 1, D_MODEL), lambda b, j: (b, 0, 0))
    else:
        grid = (m // tm, 1)
        imap = lambda i, j: (i, 0)
        e_spec = pl.BlockSpec((_SC_CONV - 1, tm, D_MODEL), lambda i, j: (0, i, 0))
        p_shape = jax.ShapeDtypeStruct((m, D_MODEL), _F32)
        p_spec = pl.BlockSpec((tm, D_MODEL), imap)
    row = pl.BlockSpec((tm, D_MODEL), imap)
    outs = (jax.ShapeDtypeStruct((m, D_MODEL), _F32), jax.ShapeDtypeStruct((m, D_MODEL), q_dtype), p_shape)
    return pl.pallas_call(
        functools.partial(_mix_c_body, per_seq, seq), out_shape=outs, grid=grid,
        in_specs=[row, _resident(g.shape), _resident(win.shape), _resident(cw.shape), e_spec, _resident(wout.shape),
                  _resident(gm.shape), _resident(wq.shape)],
        out_specs=(row, row, p_spec),
        scratch_shapes=[pltpu.VMEM((SUBLANES + tm, D_MODEL), _F32)],
        compiler_params=_cparams(2), name="mix_c",
    )(x2d, g, win, cw, e_buf, wout, gm, wq)


def _expand_buf(buf, seq, n_back):
    bsz, km1, ch = buf.shape
    outs = []
    for j in range(1, n_back + 1):
        rows = [buf[:, km1 - j + t] if t < j else jnp.zeros((bsz, ch), buf.dtype) for t in range(seq)]
        outs.append(jnp.stack(rows, axis=1).reshape(bsz * seq, ch))
    return jnp.stack(outs)


def _row(vec, width=None):
    vec = vec.reshape(1, -1).astype(_F32)
    if width is not None and vec.shape[1] < width:
        vec = jnp.pad(vec, ((0, 0), (0, width - vec.shape[1])))
    return vec


def _trunk(x, k_all, v_all, prompt, states, w):
    bsz, seq, _ = x.shape
    m = bsz * seq
    x2d = x.reshape(m, D_MODEL)
    tm = min(512, m)
    dskip = _row(jnp.repeat(w["d_skip"][0], SSM_HEAD_DIM))
    common = (w["conv_w_ssm"][0], _row(w["conv_b_ssm"][0]), _row(w["dt_bias"][0], LANES), _row(w["a_log"][0], LANES),
              dskip, _row(w["g_ssm_norm"][0]))

    act_dtype = _BF if prompt else _F32
    u, v, z, xbc, dt = _inproj_ab(x2d, _row(w["g_mix"][0]), w["w_in_main"], w["w_in_dt"], act_dtype, tm)
    if prompt:
        y_ab, ssm_state, conv_state = _mix_ab_prompt(
            u, v, z, xbc, dt, bsz, seq, w["w_spatial"][0], w["b_spatial"][0][:, :, None], *common)
        v_out = None
    else:
        ws = w["w_spatial"][0][:, :seq, :seq]
        coef = []
        for d in range(seq):
            per_t = [ws[:, t, t - d] if t >= d else jnp.zeros((A_GROUPS,), _F32) for t in range(seq)]
            coef.append(jnp.tile(jnp.repeat(jnp.stack(per_t), A_HEAD, axis=1), (_SEQ_BLK, 1)))
        sb = jnp.tile(jnp.repeat(w["b_spatial"][0][:, :seq].T, A_HEAD, axis=1), (_SEQ_BLK, 1))
        e_buf = _expand_buf(states["ssm_conv"], seq, SSM_CONV - 1)
        y_ab, ssm_state = _mix_ab_sample(u, v, z, xbc, dt, e_buf, states["ssm"].reshape(bsz, SSM_DIM, SSM_STATE), seq,
                                         jnp.stack(coef), sb, *common)
        conv_state = xbc.reshape(bsz, seq, SSM_CONV_DIM)[:, seq - (SSM_CONV - 1):]
        v_out = v.reshape(bsz, seq, D_MODEL)
    x2d, q = _out_q(y_ab, x2d, w["w_out_ab"], _row(w["g_mem"][0]), w["w_mem_q"][0], act_dtype, tm)

    def attend(q, x2d, layer):
        if prompt:
            return _attn_prompt(q, k_all, v_all, x2d, w["w_mem_o"][layer], layer, bsz, seq, tm)
        return _attn_sample(q, k_all, v_all, x2d, w["w_mem_o"][layer], layer, bsz, seq)

    x2d = attend(q, x2d, 0)
    tf = min(512, m)
    dummy_r = jnp.zeros((D_MODEL, LANES), _F32)
    x2d = _ffn(x2d, _row(w["g_ffn"][0]), dummy_r, w["w_ffn_gate"], w["w_ffn_up"], w["w_ffn_down"], _row(w["g_final"]),
               False, False, tf)

    if prompt:
        e_buf = jnp.zeros((_SC_CONV - 1, SUBLANES, D_MODEL), _F32)
        x2d, q, sconv_state = _mix_c(x2d, _row(w["g_mix"][1]), w["w_in_c"], w["conv_w_c"][0], e_buf, w["w_out_c"],
                                     _row(w["g_mem"][1]), w["w_mem_q"][1], act_dtype, True, bsz, seq, tm)
    else:
        e_buf = _expand_buf(states["sconv"], seq, _SC_CONV - 1)
        x2d, q, p = _mix_c(x2d, _row(w["g_mix"][1]), w["w_in_c"], w["conv_w_c"][0], e_buf, w["w_out_c"],
                           _row(w["g_mem"][1]), w["w_mem_q"][1], act_dtype, False, bsz, seq, tm)
        sconv_state = p.reshape(bsz, seq, D_MODEL)[:, seq - (_SC_CONV - 1):]
    x2d = attend(q, x2d, 1)
    x2d = _ffn(x2d, _row(w["g_ffn"][1]), w["w_router"], w["w_exp_gate"], w["w_exp_up"], w["w_exp_down"],
               _row(w["g_final"]), True, True, tf)
    y = x2d.reshape(bsz, seq, D_MODEL)
    ssm_state = ssm_state.reshape(1, bsz, SSM_HEADS, SSM_HEAD_DIM, SSM_STATE)
    return y, ssm_state, conv_state[None], sconv_state[None], v_out


def kernel(x_prompt, x_sample, mem_prompt, state_ssm, state_ssm_conv, state_sconv, cache_mem_k, cache_mem_v, g_mix, g_mem, g_ffn, g_final, w_in_ab, w_spatial, b_spatial, conv_w_ssm, conv_b_ssm, dt_bias, a_log, d_skip, g_ssm_norm, w_out_ab, w_ffn_gate, w_ffn_up, w_ffn_down, w_in_c, conv_w_c, w_out_c, w_router, w_exp_gate, w_exp_up, w_exp_down, w_mem_q, w_mem_k, w_mem_v, w_mem_o):
    depth = w_mem_q.shape[0]
    bp = x_prompt.shape[0]
    bs = x_sample.shape[0]
    n_main = 3 * D_MODEL + SSM_CONV_DIM
    bf = lambda a: a.astype(_BF)
    w = dict(
        g_mix=g_mix, g_mem=g_mem, g_ffn=g_ffn, g_final=g_final, w_spatial=w_spatial, b_spatial=b_spatial,
        conv_w_ssm=conv_w_ssm, conv_b_ssm=conv_b_ssm, dt_bias=dt_bias, a_log=a_log, d_skip=d_skip, g_ssm_norm=g_ssm_norm,
        conv_w_c=conv_w_c,
        w_in_main=bf(w_in_ab[0][:, :n_main]),
        w_in_dt=bf(jnp.pad(w_in_ab[0][:, n_main:], ((0, 0), (0, LANES - SSM_HEADS)))),
        w_out_ab=bf(w_out_ab[0]), w_ffn_gate=bf(w_ffn_gate), w_ffn_up=bf(w_ffn_up), w_ffn_down=bf(w_ffn_down),
        w_in_c=bf(w_in_c[0]), w_out_c=bf(w_out_c[0]),
        w_router=jnp.pad(w_router[0], ((0, 0), (0, LANES - N_EXPERTS))),
        w_exp_gate=bf(w_exp_gate[0]), w_exp_up=bf(w_exp_up[0]), w_exp_down=bf(w_exp_down[0]),
        w_mem_q=bf(w_mem_q), w_mem_o=bf(w_mem_o),
    )
    mem_k_p, mem_v_p = _mem_kv(mem_prompt.reshape(bp * N_MEM, D_MODEL), bf(w_mem_k), bf(w_mem_v))
    y_p, ssm_p, ssmconv_p, sconv_p, _ = _trunk(
        x_prompt, mem_k_p.reshape(depth * bp, N_MEM, D_MODEL), mem_v_p.reshape(depth * bp, N_MEM, D_MODEL), True, None, w)
    states = dict(ssm=state_ssm[0], ssm_conv=state_ssm_conv[0], sconv=state_sconv[0])
    y_s, ssm_s, ssmconv_s, sconv_s, v_s = _trunk(
        x_sample, cache_mem_k.reshape(depth * bs, N_MEM, MEM_HEADS, MEM_HEAD_DIM),
        cache_mem_v.reshape(depth * bs, N_MEM, MEM_HEADS, MEM_HEAD_DIM), False, states, w)
    kv_shape = (depth, bp, N_MEM, MEM_HEADS, MEM_HEAD_DIM)
    return (y_p, y_s, ssm_p, ssm_s, ssmconv_p, ssmconv_s, sconv_p, sconv_s,
            mem_k_p.reshape(kv_shape), mem_v_p.reshape(kv_shape), v_s[None])
```

```python
import functools

import jax
import jax.numpy as jnp
from jax import lax
from jax.experimental import pallas as pl
from jax.experimental.pallas import tpu as pltpu

D_MODEL = 1024
EPS = 1e-6
CHUNK = 128
A_GROUPS = 8
A_HEAD = D_MODEL // A_GROUPS
SSM_HEADS = 16
SSM_HEAD_DIM = 64
SSM_GROUPS = 2
SSM_STATE = 128
SSM_CONV = 4
SSM_DIM = D_MODEL
SSM_CONV_DIM = SSM_DIM + 2 * SSM_GROUPS * SSM_STATE
N_MEM = 256
MEM_HEADS = 4
MEM_HEAD_DIM = D_MODEL // MEM_HEADS
N_EXPERTS = 8

LANES = 128
SUBLANES = 8
VMEM_LIMIT = 56 * 1024 * 1024

_BF = jnp.bfloat16
_F32 = jnp.float32
_HI = lax.Precision.HIGHEST
_NT = (((1,), (1,)), ((), ()))


def _cparams(n_axes):
    return pltpu.CompilerParams(dimension_semantics=("arbitrary",) * n_axes, vmem_limit_bytes=VMEM_LIMIT)


def _resident(shape):
    nd = len(shape)
    return pl.BlockSpec(shape, lambda *_: (0,) * nd, pipeline_mode=pl.Buffered(1))


def _rms(x, g):
    ms = jnp.mean(x * x, axis=-1, keepdims=True)
    return x * lax.rsqrt(ms + EPS) * g


def _silu(x):
    return x * jax.nn.sigmoid(x)


def _dot(a, b):
    return jnp.dot(a, b, preferred_element_type=_F32)


def _softplus(x):
    return jnp.maximum(x, 0.0) + jnp.log1p(jnp.exp(-jnp.abs(x)))


def _pos_and_seq(rows, seq):
    r = lax.broadcasted_iota(jnp.int32, (rows, 1), 0)
    if seq & (seq - 1) == 0:
        return r & (seq - 1), r >> (seq.bit_length() - 1)
    return lax.rem(r, seq), lax.div(r, seq)


def _kv_body(m_ref, wk_ref, wv_ref, k_ref, v_ref):
    m = m_ref[...].astype(_BF)
    k_ref[0] = _dot(m, wk_ref[0])
    v_ref[0] = _dot(m, wv_ref[0])


def _mem_kv(mem2d, wk, wv):
    n, d = mem2d.shape
    depth = wk.shape[0]
    out = jax.ShapeDtypeStruct((depth, n, d), _F32)
    wspec = pl.BlockSpec((1, d, d), lambda l: (l, 0, 0))
    ospec = pl.BlockSpec((1, n, d), lambda l: (l, 0, 0))
    return pl.pallas_call(
        _kv_body, out_shape=(out, out), grid=(depth,),
        in_specs=[pl.BlockSpec((n, d), lambda l: (0, 0)), wspec, wspec],
        out_specs=(ospec, ospec), compiler_params=_cparams(1), name="mem_kv",
    )(mem2d, wk, wv)


_PROJ_CHUNK = 512


def _inproj_ab_body(x_ref, g_ref, w_ref, wdt_ref, u_ref, v_ref, z_ref, xbc_ref, dt_ref):
    hn = _rms(x_ref[...], g_ref[...]).astype(_BF)
    col = 0
    for ref, width, act in ((u_ref, D_MODEL, True), (v_ref, D_MODEL, True), (z_ref, D_MODEL, False),
                            (xbc_ref, SSM_CONV_DIM, False)):
        for c in range(width // _PROJ_CHUNK):
            r = _dot(hn, w_ref[:, col:col + _PROJ_CHUNK])
            if act:
                r = jax.nn.gelu(r)
            ref[:, c * _PROJ_CHUNK:(c + 1) * _PROJ_CHUNK] = r.astype(ref.dtype)
            col += _PROJ_CHUNK
    dt_ref[...] = _dot(hn, wdt_ref[...])


def _inproj_ab(x2d, g, w_main, w_dt, act_dtype, tm):
    m = x2d.shape[0]
    row = lambda width: pl.BlockSpec((tm, width), lambda i: (i, 0))
    outs = (jax.ShapeDtypeStruct((m, D_MODEL), act_dtype),) * 3 + (
        jax.ShapeDtypeStruct((m, SSM_CONV_DIM), _F32), jax.ShapeDtypeStruct((m, LANES), _F32))
    return pl.pallas_call(
        _inproj_ab_body, out_shape=outs, grid=(m // tm,),
        in_specs=[row(D_MODEL), _resident(g.shape), _resident(w_main.shape), _resident(w_dt.shape)],
        out_specs=(row(D_MODEL), row(D_MODEL), row(D_MODEL), row(SSM_CONV_DIM), row(LANES)),
        compiler_params=_cparams(1), name="inproj_ab",
    )(x2d, g, w_main, w_dt)


_PAIR = 2 * SSM_HEAD_DIM
_HPG = SSM_HEADS // SSM_GROUPS
_GRP_DIM = SSM_DIM // SSM_GROUPS


def _mix_ab_prompt_body(u_ref, v_ref, z_ref, xbc_ref, dt_ref, ws_ref, bs_ref, cw_ref, cb_ref, dtb_ref, alog_ref,
                        dskip_ref, gn_ref, y_ref, st_ref, cst_ref, xp_ref, sT_ref):
    q = CHUNK
    c = pl.program_id(1)

    @pl.when(c == 0)
    def _():
        xp_ref[0:SUBLANES, :] = jnp.zeros((SUBLANES, SSM_CONV_DIM), _F32)
        sT_ref[...] = jnp.zeros_like(sT_ref)

    tri = lax.broadcasted_iota(jnp.int32, (q, q), 0) >= lax.broadcasted_iota(jnp.int32, (q, q), 1)

    for g in range(A_GROUPS):
        sl = slice(g * A_HEAD, (g + 1) * A_HEAD)
        w = jnp.where(tri, ws_ref[g], 0.0).astype(_BF)
        s = _dot(w, v_ref[:, sl].astype(_BF)) + bs_ref[g]
        y_ref[:, sl] = (u_ref[:, sl].astype(_F32) * s).astype(y_ref.dtype)

    xbc = xbc_ref[...]
    xp_ref[SUBLANES:SUBLANES + q, :] = xbc
    base = SUBLANES - (SSM_CONV - 1)
    conv = xp_ref[base:base + q, :] * cw_ref[0:1, :]
    for k in range(1, SSM_CONV):
        conv = conv + xp_ref[base + k:base + k + q, :] * cw_ref[k:k + 1, :]
    xp_ref[0:SUBLANES, :] = xbc[q - SUBLANES:q, :]
    act = _silu(conv + cb_ref[...])
    xs = act[:, :SSM_DIM]
    bm = act[:, SSM_DIM:SSM_DIM + SSM_GROUPS * SSM_STATE]
    cm = act[:, SSM_DIM + SSM_GROUPS * SSM_STATE:]

    dt = _softplus(dt_ref[...] + dtb_ref[...])
    d_a = dt * (-jnp.exp(alog_ref[...]))
    la = jnp.dot(tri.astype(_F32), d_a, precision=_HI, preferred_element_type=_F32)
    la_t = la.T
    dt_t = dt.T
    e_la = jnp.exp(la)
    last = la[q - 1:q, :]
    wend = jnp.exp(last - la) * dt
    e_last = jnp.exp(last)
    lo = lax.broadcasted_iota(jnp.int32, (q, _PAIR), 1) < SSM_HEAD_DIM
    lo_row = lo[0:1, :]

    for g in range(SSM_GROUPS):
        bm_t = bm[:, g * SSM_STATE:(g + 1) * SSM_STATE].T.astype(_BF)
        cm_g = cm[:, g * SSM_STATE:(g + 1) * SSM_STATE].astype(_BF)
        cb = _dot(cm_g, bm_t)
        ys = []
        for pp in range(_HPG // 2):
            p = g * (_HPG // 2) + pp
            cols = slice(p * _PAIR, (p + 1) * _PAIR)
            wts = []
            for h in (2 * p, 2 * p + 1):
                seg = la[:, h:h + 1] - la_t[h:h + 1, :]
                wts.append(cb * jnp.exp(jnp.where(tri, seg, -jnp.inf)) * dt_t[h:h + 1, :])
            w2 = jnp.concatenate(wts, axis=1).astype(_BF)
            x_pair = xs[:, cols]
            x2 = jnp.concatenate([jnp.where(lo, x_pair, 0.0), jnp.where(lo, 0.0, x_pair)], axis=0).astype(_BF)
            s_t = sT_ref[:, cols]
            e_pair = jnp.where(lo, e_la[:, 2 * p:2 * p + 1], e_la[:, 2 * p + 1:2 * p + 2])
            y_pair = _dot(w2, x2) + _dot(cm_g, s_t.astype(_BF)) * e_pair
            wd_pair = jnp.where(lo, wend[:, 2 * p:2 * p + 1], wend[:, 2 * p + 1:2 * p + 2])
            el_pair = jnp.where(lo_row, e_last[:, 2 * p:2 * p + 1], e_last[:, 2 * p + 1:2 * p + 2])
            sT_ref[:, cols] = s_t * el_pair + _dot(bm_t, (x_pair * wd_pair).astype(_BF))
            ys.append(y_pair + x_pair * dskip_ref[:, cols])
        gcols = slice(g * _GRP_DIM, (g + 1) * _GRP_DIM)
        gated = jnp.concatenate(ys, axis=1) * _silu(z_ref[:, gcols].astype(_F32))
        y_ref[:, D_MODEL + g * _GRP_DIM:D_MODEL + (g + 1) * _GRP_DIM] = _rms(gated, gn_ref[:, gcols]).astype(y_ref.dtype)

    @pl.when(c == pl.num_programs(1) - 1)
    def _():
        for p in range(SSM_HEADS // 2):
            st_ref[0, p * _PAIR:(p + 1) * _PAIR, :] = sT_ref[:, p * _PAIR:(p + 1) * _PAIR].T
        cst_ref[0] = xp_ref[SUBLANES + q - (SSM_CONV - 1):SUBLANES + q, :]


def _mix_ab_prompt(u, v, z, xbc, dt, bsz, seq, ws, bs, cw, cb, dtb, alog, dskip, gn):
    nc = seq // CHUNK
    row = lambda width: pl.BlockSpec((CHUNK, width), lambda b, c: (b * nc + c, 0))
    small = [ws, bs, cw, cb, dtb, alog, dskip, gn]
    outs = (jax.ShapeDtypeStruct((bsz * seq, 2 * D_MODEL), _BF),
            jax.ShapeDtypeStruct((bsz, SSM_DIM, SSM_STATE), _F32),
            jax.ShapeDtypeStruct((bsz, SSM_CONV - 1, SSM_CONV_DIM), _F32))
    return pl.pallas_call(
        _mix_ab_prompt_body, out_shape=outs, grid=(bsz, nc),
        in_specs=[row(D_MODEL), row(D_MODEL), row(D_MODEL), row(SSM_CONV_DIM), row(LANES)]
        + [_resident(a.shape) for a in small],
        out_specs=(row(2 * D_MODEL),
                   pl.BlockSpec((1, SSM_DIM, SSM_STATE), lambda b, c: (b, 0, 0)),
                   pl.BlockSpec((1, SSM_CONV - 1, SSM_CONV_DIM), lambda b, c: (b, 0, 0))),
        scratch_shapes=[pltpu.VMEM((SUBLANES + CHUNK, SSM_CONV_DIM), _F32), pltpu.VMEM((SSM_STATE, SSM_DIM), _F32)],
        compiler_params=_cparams(2), name="mix_ab_prompt",
    )(u, v, z, xbc, dt, *small)


_SEQ_BLK = 8


def _mix_ab_sample_body(seq, u_ref, v_ref, z_ref, xbc_ref, dt_ref, e_ref, st_ref, coef_ref, sb_ref, cw_ref, cb_ref,
                        dtb_ref, alog_ref, dskip_ref, gn_ref, y_ref, sto_ref, sh_ref):
    rows = _SEQ_BLK * seq
    t, seq_of_row = _pos_and_seq(rows, seq)
    pad_rows = sh_ref.shape[0]
    sh_ref[...] = jnp.zeros((pad_rows, SSM_CONV_DIM), _F32)

    def shifts(val, deltas):
        width = val.shape[1]
        sh_ref[SUBLANES:SUBLANES + rows, 0:width] = val
        return [sh_ref[SUBLANES - d:SUBLANES - d + rows, 0:width] for d in deltas]

    back = tuple(range(1, seq))

    v = v_ref[...]
    s = coef_ref[0] * v + sb_ref[...]
    for d, vs in zip(back, shifts(v, back)):
        s = s + coef_ref[d] * jnp.where(t >= d, vs, 0.0)
    y_ref[:, 0:D_MODEL] = u_ref[...] * s

    xbc = xbc_ref[...]
    conv = xbc * cw_ref[SSM_CONV - 1:SSM_CONV, :]
    for j, xsft in zip(range(1, SSM_CONV), shifts(xbc, tuple(range(1, SSM_CONV)))):
        conv = conv + (jnp.where(t >= j, xsft, 0.0) + e_ref[j - 1]) * cw_ref[SSM_CONV - 1 - j:SSM_CONV - j, :]
    act = _silu(conv + cb_ref[...])
    xs = act[:, :SSM_DIM]
    bm = act[:, SSM_DIM:SSM_DIM + SSM_GROUPS * SSM_STATE]
    cm = act[:, SSM_DIM + SSM_GROUPS * SSM_STATE:]

    dt = _softplus(dt_ref[...] + dtb_ref[...])
    d_a = dt * (-jnp.exp(alog_ref[...]))
    la = d_a
    for d, sft in zip(back, shifts(d_a, back)):
        la = la + jnp.where(t >= d, sft, 0.0)
    la_back = shifts(la, back)
    dt_back = shifts(dt, back)
    last = jnp.where(t == seq - 1, la, 0.0)
    fwd = shifts(la, tuple(-d for d in back))
    for d, sft in zip(back, fwd):
        last = last + jnp.where(t == seq - 1 - d, sft, 0.0)
    wend = jnp.exp(last - la) * dt
    e_last = jnp.exp(last)
    e_la = jnp.exp(la)

    head_lane = lax.broadcasted_iota(jnp.int32, (rows, LANES), 1)
    bm_back = shifts(bm, back)

    def cb_heads(b_other):
        per_g = [jnp.sum(cm[:, g * SSM_STATE:(g + 1) * SSM_STATE] * b_other[:, g * SSM_STATE:(g + 1) * SSM_STATE],
                         axis=-1, keepdims=True) for g in range(SSM_GROUPS)]
        return jnp.where(head_lane < _HPG, per_g[0], per_g[1])

    gs = [cb_heads(bm) * dt]
    for i, d in enumerate(back):
        gd = cb_heads(bm_back[i]) * jnp.exp(la - la_back[i]) * dt_back[i]
        gs.append(jnp.where(t >= d, gd, 0.0))

    head0 = lax.broadcasted_iota(jnp.int32, (LANES, SSM_DIM), 0) * SSM_HEAD_DIM
    chan = lax.broadcasted_iota(jnp.int32, (LANES, SSM_DIM), 1)
    expand = ((chan >= head0) & (chan < head0 + SSM_HEAD_DIM)).astype(_F32)
    stacked = jnp.concatenate([e_la, wend, e_last] + gs, axis=0)
    full = jnp.dot(stacked, expand, precision=_HI, preferred_element_type=_F32)
    e_la_f, wend_f, e_last_f = full[0:rows], full[rows:2 * rows], full[2 * rows:3 * rows]
    y2 = full[3 * rows:4 * rows] * xs
    for i, (d, xsft) in enumerate(zip(back, shifts(xs, back))):
        y2 = y2 + full[(4 + i) * rows:(5 + i) * rows] * jnp.where(t >= d, xsft, 0.0)

    zpad = jnp.zeros((LANES - rows, SSM_DIM), _F32)
    xw_t = jnp.concatenate([xs * wend_f, zpad], axis=0).T.astype(_BF)
    el_t = jnp.concatenate([e_last_f, zpad], axis=0).T
    row_lane = lax.broadcasted_iota(jnp.int32, (_GRP_DIM, LANES), 1)
    bpad = jnp.zeros((LANES - rows, SSM_STATE), _F32)

    y1 = [jnp.zeros((rows, _GRP_DIM), _F32) for _ in range(SSM_GROUPS)]
    for i in range(_SEQ_BLK):
        mine = seq_of_row == i
        for g in range(SSM_GROUPS):
            gr = slice(g * _GRP_DIM, (g + 1) * _GRP_DIM)
            st = st_ref[i, gr, :]
            b_g = jnp.where(mine, bm[:, g * SSM_STATE:(g + 1) * SSM_STATE], 0.0)
            b_g = jnp.concatenate([b_g, bpad], axis=0).astype(_BF)
            decay = jnp.sum(jnp.where(row_lane == i * seq, el_t[gr, :], 0.0), axis=1, keepdims=True)
            sto_ref[i, gr, :] = st * decay + _dot(xw_t[gr, :], b_g)
            c_g = cm[:, g * SSM_STATE:(g + 1) * SSM_STATE].astype(_BF)
            ch = lax.dot_general(c_g, st.astype(_BF), _NT, preferred_element_type=_F32)
            y1[g] = y1[g] + jnp.where(mine, ch, 0.0)

    for g in range(SSM_GROUPS):
        gr = slice(g * _GRP_DIM, (g + 1) * _GRP_DIM)
        ys = y1[g] * e_la_f[:, gr] + y2[:, gr] + xs[:, gr] * dskip_ref[:, gr]
        gated = ys * _silu(z_ref[:, gr])
        y_ref[:, D_MODEL + g * _GRP_DIM:D_MODEL + (g + 1) * _GRP_DIM] = _rms(gated, gn_ref[:, gr])


def _mix_ab_sample(u, v, z, xbc, dt, e_buf, state, seq, coef, sb, cw, cb, dtb, alog, dskip, gn):
    bsz = state.shape[0]
    rows = _SEQ_BLK * seq
    row = lambda width: pl.BlockSpec((rows, width), lambda i: (i, 0))
    small = [coef, sb, cw, cb, dtb, alog, dskip, gn]
    st_spec = pl.BlockSpec((_SEQ_BLK, SSM_DIM, SSM_STATE), lambda i: (i, 0, 0))
    outs = (jax.ShapeDtypeStruct((bsz * seq, 2 * D_MODEL), _F32), jax.ShapeDtypeStruct(state.shape, _F32))
    return pl.pallas_call(
        functools.partial(_mix_ab_sample_body, seq), out_shape=outs, grid=(bsz // _SEQ_BLK,),
        in_specs=[row(D_MODEL), row(D_MODEL), row(D_MODEL), row(SSM_CONV_DIM), row(LANES),
                  pl.BlockSpec((SSM_CONV - 1, rows, SSM_CONV_DIM), lambda i: (0, i, 0)), st_spec]
        + [_resident(a.shape) for a in small],
        out_specs=(row(2 * D_MODEL), st_spec),
        scratch_shapes=[pltpu.VMEM((rows + 2 * SUBLANES, SSM_CONV_DIM), _F32)],
        compiler_params=_cparams(1), name="mix_ab_sample",
    )(u, v, z, xbc, dt, e_buf, state, *small)


def _out_q_body(a_ref, x_ref, wo_ref, gm_ref, wq_ref, xo_ref, q_ref):
    xn = x_ref[...] + _dot(a_ref[...].astype(_BF), wo_ref[...])
    xo_ref[...] = xn
    q_ref[...] = _dot(_rms(xn, gm_ref[...]).astype(_BF), wq_ref[...]).astype(q_ref.dtype)


def _out_q(a, x2d, wo, gm, wq, q_dtype, tm):
    m = x2d.shape[0]
    row = lambda width: pl.BlockSpec((tm, width), lambda i: (i, 0))
    outs = (jax.ShapeDtypeStruct((m, D_MODEL), _F32), jax.ShapeDtypeStruct((m, D_MODEL), q_dtype))
    return pl.pallas_call(
        _out_q_body, out_shape=outs, grid=(m // tm,),
        in_specs=[row(a.shape[1]), row(D_MODEL), _resident(wo.shape), _resident(gm.shape), _resident(wq.shape)],
        out_specs=(row(D_MODEL), row(D_MODEL)), compiler_params=_cparams(1), name="out_q",
    )(a, x2d, wo, gm, wq)


_ATT_SCALE = MEM_HEAD_DIM ** -0.5


def _attend(q, k, v):
    outs = []
    for h in range(MEM_HEADS):
        sl = slice(h * MEM_HEAD_DIM, (h + 1) * MEM_HEAD_DIM)
        s = lax.dot_general(q[:, sl], k[:, sl], _NT, preferred_element_type=_F32) * _ATT_SCALE
        p = jnp.exp(s - jnp.max(s, axis=-1, keepdims=True))
        p = p / jnp.sum(p, axis=-1, keepdims=True)
        outs.append(_dot(p.astype(_BF), v[:, sl]))
    return jnp.concatenate(outs, axis=1)


def _attn_prompt_body(q_ref, k_ref, v_ref, x_ref, wo_ref, o_ref):
    o = _attend(q_ref[...], k_ref[0].astype(_BF), v_ref[0].astype(_BF))
    o_ref[...] = x_ref[...] + _dot(o.astype(_BF), wo_ref[...])


def _attn_prompt(q, k, v, x2d, wo, layer, bsz, seq, tq):
    nt = seq // tq
    row = pl.BlockSpec((tq, D_MODEL), lambda b, j: (b * nt + j, 0))
    kv = pl.BlockSpec((1, N_MEM, D_MODEL), lambda b, j: (layer * bsz + b, 0, 0))
    return pl.pallas_call(
        _attn_prompt_body, out_shape=jax.ShapeDtypeStruct(x2d.shape, _F32), grid=(bsz, nt),
        in_specs=[row, kv, kv, row, _resident(wo.shape)], out_specs=row,
        compiler_params=_cparams(2), name="attn_prompt",
    )(q, k, v, x2d, wo)


def _attn_sample_body(seq, q_ref, k_ref, v_ref, x_ref, wo_ref, o_ref, o_scr):
    per_tile = SUBLANES // seq
    _, seq_of_row = _pos_and_seq(SUBLANES, seq)
    n_lt = MEM_HEAD_DIM // LANES
    rows_mh = N_MEM * MEM_HEADS
    col = lax.broadcasted_iota(jnp.int32, (MEM_HEADS * SUBLANES, rows_mh), 1)
    row = lax.broadcasted_iota(jnp.int32, (MEM_HEADS * SUBLANES, rows_mh), 0)
    same_head = (col & (MEM_HEADS - 1)) == (row >> (SUBLANES.bit_length() - 1))
    for tile in range(_SEQ_BLK // per_tile):
        q8 = q_ref[tile * SUBLANES:(tile + 1) * SUBLANES, :].astype(_BF)
        o = None
        for j in range(per_tile):
            i = tile * per_tile + j
            s = None
            for lt in range(n_lt):
                qh = jnp.concatenate([q8[:, h * MEM_HEAD_DIM + lt * LANES:h * MEM_HEAD_DIM + (lt + 1) * LANES]
                                      for h in range(MEM_HEADS)], axis=0)
                k_lt = k_ref[i, :, :, lt * LANES:(lt + 1) * LANES].reshape(rows_mh, LANES).astype(_BF)
                part = lax.dot_general(qh, k_lt, _NT, preferred_element_type=_F32)
                s = part if s is None else s + part
            s = jnp.where(same_head, s * _ATT_SCALE, -jnp.inf)
            p = jnp.exp(s - jnp.max(s, axis=-1, keepdims=True))
            p = (p / jnp.sum(p, axis=-1, keepdims=True)).astype(_BF)
            pieces = [None] * (MEM_HEADS * n_lt)
            for lt in range(n_lt):
                v_lt = v_ref[i, :, :, lt * LANES:(lt + 1) * LANES].reshape(rows_mh, LANES).astype(_BF)
                o_lt = _dot(p, v_lt)
                for h in range(MEM_HEADS):
                    pieces[h * n_lt + lt] = o_lt[h * SUBLANES:(h + 1) * SUBLANES, :]
            oi = jnp.concatenate(pieces, axis=1)
            o = oi if o is None else jnp.where(seq_of_row == j, oi, o)
        o_scr[tile * SUBLANES:(tile + 1) * SUBLANES, :] = o
    o_ref[...] = x_ref[...] + _dot(o_scr[...].astype(_BF), wo_ref[...])


def _attn_sample(q, k, v, x2d, wo, layer, bsz, seq):
    rows = _SEQ_BLK * seq
    nb = bsz // _SEQ_BLK
    row = pl.BlockSpec((rows, D_MODEL), lambda i: (i, 0))
    kv = pl.BlockSpec((_SEQ_BLK, N_MEM, MEM_HEADS, MEM_HEAD_DIM), lambda i: (layer * nb + i, 0, 0, 0))
    return pl.pallas_call(
        functools.partial(_attn_sample_body, seq), out_shape=jax.ShapeDtypeStruct(x2d.shape, _F32), grid=(nb,),
        in_specs=[row, kv, kv, row, _resident(wo.shape)], out_specs=row,
        scratch_shapes=[pltpu.VMEM((rows, D_MODEL), _F32)],
        compiler_params=_cparams(1), name="attn_sample",
    )(q, k, v, x2d, wo)


_FF_GRID_CHUNKS = 2
_FF_SUB = 512
_MOE_TILE = 1024


def _swiglu_chunk(h, wg_ref, wu_ref, wd_ref):
    width = wg_ref.shape[2]
    out = None
    for lo in range(0, width, _FF_SUB):
        hi = min(lo + _FF_SUB, width)
        hid = _silu(_dot(h, wg_ref[0, :, lo:hi])) * _dot(h, wu_ref[0, :, lo:hi])
        part = _dot(hid.astype(_BF), wd_ref[0, lo:hi, :])
        out = part if out is None else out + part
    return out


def _ffn_body(x_ref, g_ref, wg_ref, wu_ref, wd_ref, o_ref, hf_ref):
    c = pl.program_id(1)

    @pl.when(c == 0)
    def _():
        x = x_ref[...]
        hf_ref[...] = _rms(x, g_ref[...]).astype(_BF)
        o_ref[...] = x

    o_ref[...] += _swiglu_chunk(hf_ref[...], wg_ref, wu_ref, wd_ref)


def _ffn(x2d, g, wg, wu, wd, tm):
    m = x2d.shape[0]
    d_ff = wg.shape[2]
    fc = d_ff // _FF_GRID_CHUNKS
    row = pl.BlockSpec((tm, D_MODEL), lambda i, c: (i, 0))
    up = pl.BlockSpec((1, D_MODEL, fc), lambda i, c: (0, 0, c))
    down = pl.BlockSpec((1, fc, D_MODEL), lambda i, c: (0, c, 0))
    return pl.pallas_call(
        _ffn_body, out_shape=jax.ShapeDtypeStruct(x2d.shape, _F32), grid=(m // tm, _FF_GRID_CHUNKS),
        in_specs=[row, _resident(g.shape), up, up, down], out_specs=row,
        scratch_shapes=[pltpu.VMEM((tm, D_MODEL), _BF)],
        compiler_params=_cparams(2), name="ffn_dense",
    )(x2d, g, wg, wu, wd)


_MOE_PICK_PROB = 2.0 / N_EXPERTS
_MOE_CAP_SIGMAS = 2.3
_BF16_ROWS = 16


def _moe_capacity(tm):
    mean = tm * _MOE_PICK_PROB
    sigma = (tm * _MOE_PICK_PROB * (1.0 - _MOE_PICK_PROB)) ** 0.5
    return int(-(-(mean + _MOE_CAP_SIGMAS * sigma) // _BF16_ROWS) * _BF16_ROWS)


def _moe_body(cap, cap_pad, final_norm, x_ref, g_ref, wr_ref, wg_ref, wu_ref, wd_ref, gf_ref, o_ref, hf_ref, comb_ref,
              rank_ref, rank_t_ref, cnt_ref, xs_ref, y_ref):
    e = pl.program_id(1)
    c = pl.program_id(2)
    tm = x_ref.shape[0]
    lane = lax.broadcasted_iota(jnp.int32, (tm, LANES), 1)

    @pl.when((e == 0) & (c == 0))
    def _():
        x = x_ref[...]
        hf = _rms(x, g_ref[...])
        hf_ref[...] = hf.astype(_BF)
        o_ref[...] = x
        logits = jnp.dot(hf, wr_ref[...], precision=_HI, preferred_element_type=_F32)
        lg = jnp.where(lane < N_EXPERTS, logits, -jnp.inf)
        v1 = jnp.max(lg, axis=-1, keepdims=True)
        i1 = jnp.min(jnp.where(lg == v1, lane, LANES), axis=-1, keepdims=True)
        lg2 = jnp.where(lane == i1, -jnp.inf, lg)
        v2 = jnp.max(lg2, axis=-1, keepdims=True)
        i2 = jnp.min(jnp.where(lg2 == v2, lane, LANES), axis=-1, keepdims=True)
        e2 = jnp.exp(v2 - v1)
        comb_ref[...] = jnp.where(lane == i1, 1.0 / (1.0 + e2), 0.0) + jnp.where(lane == i2, e2 / (1.0 + e2), 0.0)
        picked = (lane == i1) | (lane == i2)
        picked_f = picked.astype(_F32)
        earlier = lax.broadcasted_iota(jnp.int32, (tm, tm), 1) < lax.broadcasted_iota(jnp.int32, (tm, tm), 0)
        rank = jnp.where(picked, _dot(earlier.astype(_BF), picked_f.astype(_BF)), -1.0)
        rank_ref[...] = rank
        for b in range(tm // LANES):
            rank_t_ref[:, b * LANES:(b + 1) * LANES] = rank[b * LANES:(b + 1) * LANES, :].T
        cnt_ref[...] = jnp.sum(picked_f, axis=0, keepdims=True)

    lane_row = lax.broadcasted_iota(jnp.int32, (1, LANES), 1)
    count = jnp.sum(jnp.where(lane_row == e, cnt_ref[...], 0.0)).astype(jnp.int32)
    n_rounds = lax.div(count + (cap - 1), cap)

    @pl.when(c == 0)
    def _():
        rank_row = rank_t_ref[pl.ds(e, 1), :]
        slot = lax.broadcasted_iota(jnp.int32, (cap, 1), 0).astype(_F32)

        def gather(j, carry):
            pick = (rank_row == slot + (j * cap).astype(_F32)).astype(_BF)
            xs_ref[j] = _dot(pick, hf_ref[...]).astype(_BF)
            return carry

        lax.fori_loop(0, n_rounds, gather, 0)

    def expert(j, carry):
        out = _swiglu_chunk(xs_ref[j], wg_ref, wu_ref, wd_ref)

        @pl.when(c == 0)
        def _():
            y_ref[j] = out

        @pl.when(c != 0)
        def _():
            y_ref[j] += out

        return carry

    lax.fori_loop(0, n_rounds, expert, 0)

    @pl.when(c == pl.num_programs(2) - 1)
    def _():
        rank_col = jnp.sum(jnp.where(lane == e, rank_ref[...], 0.0), axis=1, keepdims=True)
        comb_col = jnp.sum(jnp.where(lane == e, comb_ref[...], 0.0), axis=1, keepdims=True)
        slot = lax.broadcasted_iota(jnp.int32, (1, cap_pad), 1)
        slot_f = slot.astype(_F32)
        pad = jnp.zeros((cap_pad - cap, D_MODEL), _BF)

        def scatter(j, carry):
            put = ((rank_col - (j * cap).astype(_F32) == slot_f) & (slot < cap)).astype(_BF)
            y = jnp.concatenate([y_ref[j].astype(_BF), pad], axis=0)
            o_ref[...] += comb_col * _dot(put, y)
            return carry

        lax.fori_loop(0, n_rounds, scatter, 0)

    if final_norm:
        @pl.when((e == pl.num_programs(1) - 1) & (c == pl.num_programs(2) - 1))
        def _():
            o_ref[...] = _rms(o_ref[...], gf_ref[...])


def _moe(x2d, g, wr, wg, wu, wd, gf, final_norm, tm):
    m = x2d.shape[0]
    n_e, _, d_ff = wg.shape
    assert tm % LANES == 0 and m % tm == 0, (m, tm)
    fc = d_ff // _FF_GRID_CHUNKS
    cap = _moe_capacity(tm)
    cap_pad = -(-cap // LANES) * LANES
    max_rounds = -(-tm // cap)
    row = pl.BlockSpec((tm, D_MODEL), lambda i, e, c: (i, 0))
    row_once = pl.BlockSpec((tm, D_MODEL), lambda i, e, c: (i, 0), pipeline_mode=pl.Buffered(1))
    up = pl.BlockSpec((1, D_MODEL, fc), lambda i, e, c: (e, 0, c))
    down = pl.BlockSpec((1, fc, D_MODEL), lambda i, e, c: (e, c, 0))
    return pl.pallas_call(
        functools.partial(_moe_body, cap, cap_pad, final_norm), out_shape=jax.ShapeDtypeStruct(x2d.shape, _F32),
        grid=(m // tm, n_e, _FF_GRID_CHUNKS),
        in_specs=[row_once, _resident(g.shape), _resident(wr.shape), up, up, down, _resident(gf.shape)], out_specs=row,
        scratch_shapes=[pltpu.VMEM((tm, D_MODEL), _BF), pltpu.VMEM((tm, LANES), _F32), pltpu.VMEM((tm, LANES), _F32),
                        pltpu.VMEM((LANES, tm), _F32), pltpu.VMEM((1, LANES), _F32),
                        pltpu.VMEM((max_rounds, cap, D_MODEL), _BF), pltpu.VMEM((max_rounds, cap, D_MODEL), _F32)],
        compiler_params=_cparams(3), name="moe",
    )(x2d, g, wr, wg, wu, wd, gf)


_SC_CONV = 3


def _mix_c_body(per_seq, seq, x_ref, g_ref, win_ref, cw_ref, e_ref, wout_ref, gm_ref, wq_ref, xo_ref, q_ref, p_ref,
                pp_ref):
    tm = x_ref.shape[0]
    x = x_ref[...]
    hn = _rms(x, g_ref[...]).astype(_BF)
    bg = _dot(hn, win_ref[:, 0:D_MODEL])
    p = _dot(hn, win_ref[:, D_MODEL:2 * D_MODEL]) * _dot(hn, win_ref[:, 2 * D_MODEL:3 * D_MODEL])
    if per_seq:
        @pl.when(pl.program_id(1) == 0)
        def _():
            pp_ref[0:SUBLANES, :] = jnp.zeros((SUBLANES, D_MODEL), _F32)
    else:
        pp_ref[0:SUBLANES, :] = jnp.zeros((SUBLANES, D_MODEL), _F32)
    pp_ref[SUBLANES:SUBLANES + tm, :] = p
    p1 = pp_ref[SUBLANES - 1:SUBLANES - 1 + tm, :]
    p2 = pp_ref[SUBLANES - 2:SUBLANES - 2 + tm, :]
    if per_seq:
        pp_ref[0:SUBLANES, :] = p[tm - SUBLANES:tm, :]
    else:
        t, _ = _pos_and_seq(tm, seq)
        p1 = jnp.where(t >= 1, p1, 0.0) + e_ref[0]
        p2 = jnp.where(t >= 2, p2, 0.0) + e_ref[1]
    conv = p2 * cw_ref[0:1, :] + p1 * cw_ref[1:2, :] + p * cw_ref[2:3, :]
    xn = x + _dot((bg * conv).astype(_BF), wout_ref[...])
    xo_ref[...] = xn
    q_ref[...] = _dot(_rms(xn, gm_ref[...]).astype(_BF), wq_ref[...]).astype(q_ref.dtype)
    if per_seq:
        @pl.when(pl.program_id(1) == pl.num_programs(1) - 1)
        def _():
            p_ref[0] = pp_ref[SUBLANES + tm - (_SC_CONV - 1):SUBLANES + tm, :]
    else:
        p_ref[...] = p


def _mix_c(x2d, g, win, cw, e_buf, wout, gm, wq, q_dtype, per_seq, bsz, seq, tm):
    m = x2d.shape[0]
    if per_seq:
        nt = seq // tm
        grid = (bsz, nt)
        imap = lambda b, j: (b * nt + j, 0)
        e_spec = pl.BlockSpec((_SC_CONV - 1, SUBLANES, D_MODEL), lambda b, j: (0, 0, 0))
        p_shape = jax.ShapeDtypeStruct((bsz, _SC_CONV - 1, D_MODEL), _F32)
        p_spec = pl.BlockSpec((1, _SC_CONV - 1, D_MODEL), lambda b, j: (b, 0, 0))
    else:
        grid = (m // tm, 1)
        imap = lambda i, j: (i, 0)
        e_spec = pl.BlockSpec((_SC_CONV - 1, tm, D_MODEL), lambda i, j: (0, i, 0))
        p_shape = jax.ShapeDtypeStruct((m, D_MODEL), _F32)
        p_spec = pl.BlockSpec((tm, D_MODEL), imap)
    row = pl.BlockSpec((tm, D_MODEL), imap)
    outs = (jax.ShapeDtypeStruct((m, D_MODEL), _F32), jax.ShapeDtypeStruct((m, D_MODEL), q_dtype), p_shape)
    return pl.pallas_call(
        functools.partial(_mix_c_body, per_seq, seq), out_shape=outs, grid=grid,
        in_specs=[row, _resident(g.shape), _resident(win.shape), _resident(cw.shape), e_spec, _resident(wout.shape),
                  _resident(gm.shape), _resident(wq.shape)],
        out_specs=(row, row, p_spec),
        scratch_shapes=[pltpu.VMEM((SUBLANES + tm, D_MODEL), _F32)],
        compiler_params=_cparams(2), name="mix_c",
    )(x2d, g, win, cw, e_buf, wout, gm, wq)


def _expand_buf(buf, seq, n_back):
    bsz, km1, ch = buf.shape
    outs = []
    for j in range(1, n_back + 1):
        rows = [buf[:, km1 - j + t] if t < j else jnp.zeros((bsz, ch), buf.dtype) for t in range(seq)]
        outs.append(jnp.stack(rows, axis=1).reshape(bsz * seq, ch))
    return jnp.stack(outs)


def _row(vec, width=None):
    vec = vec.reshape(1, -1).astype(_F32)
    if width is not None and vec.shape[1] < width:
        vec = jnp.pad(vec, ((0, 0), (0, width - vec.shape[1])))
    return vec


def _trunk(x, k_all, v_all, prompt, states, w):
    bsz, seq, _ = x.shape
    m = bsz * seq
    x2d = x.reshape(m, D_MODEL)
    tm = min(512, m)
    dskip = _row(jnp.repeat(w["d_skip"][0], SSM_HEAD_DIM))
    common = (w["conv_w_ssm"][0], _row(w["conv_b_ssm"][0]), _row(w["dt_bias"][0], LANES), _row(w["a_log"][0], LANES),
              dskip, _row(w["g_ssm_norm"][0]))

    act_dtype = _BF if prompt else _F32
    u, v, z, xbc, dt = _inproj_ab(x2d, _row(w["g_mix"][0]), w["w_in_main"], w["w_in_dt"], act_dtype, tm)
    if prompt:
        y_ab, ssm_state, conv_state = _mix_ab_prompt(
            u, v, z, xbc, dt, bsz, seq, w["w_spatial"][0], w["b_spatial"][0][:, :, None], *common)
        v_out = None
    else:
        ws = w["w_spatial"][0][:, :seq, :seq]
        coef = []
        for d in range(seq):
            per_t = [ws[:, t, t - d] if t >= d else jnp.zeros((A_GROUPS,), _F32) for t in range(seq)]
            coef.append(jnp.tile(jnp.repeat(jnp.stack(per_t), A_HEAD, axis=1), (_SEQ_BLK, 1)))
        sb = jnp.tile(jnp.repeat(w["b_spatial"][0][:, :seq].T, A_HEAD, axis=1), (_SEQ_BLK, 1))
        e_buf = _expand_buf(states["ssm_conv"], seq, SSM_CONV - 1)
        y_ab, ssm_state = _mix_ab_sample(u, v, z, xbc, dt, e_buf, states["ssm"].reshape(bsz, SSM_DIM, SSM_STATE), seq,
                                         jnp.stack(coef), sb, *common)
        conv_state = xbc.reshape(bsz, seq, SSM_CONV_DIM)[:, seq - (SSM_CONV - 1):]
        v_out = v.reshape(bsz, seq, D_MODEL)
    x2d, q = _out_q(y_ab, x2d, w["w_out_ab"], _row(w["g_mem"][0]), w["w_mem_q"][0], act_dtype, tm)

    def attend(q, x2d, layer):
        if prompt:
            return _attn_prompt(q, k_all, v_all, x2d, w["w_mem_o"][layer], layer, bsz, seq, tm)
        return _attn_sample(q, k_all, v_all, x2d, w["w_mem_o"][layer], layer, bsz, seq)

    x2d = attend(q, x2d, 0)
    x2d = _ffn(x2d, _row(w["g_ffn"][0]), w["w_ffn_gate"], w["w_ffn_up"], w["w_ffn_down"], tm)

    if prompt:
        e_buf = jnp.zeros((_SC_CONV - 1, SUBLANES, D_MODEL), _F32)
        x2d, q, sconv_state = _mix_c(x2d, _row(w["g_mix"][1]), w["w_in_c"], w["conv_w_c"][0], e_buf, w["w_out_c"],
                                     _row(w["g_mem"][1]), w["w_mem_q"][1], act_dtype, True, bsz, seq, tm)
    else:
        e_buf = _expand_buf(states["sconv"], seq, _SC_CONV - 1)
        x2d, q, p = _mix_c(x2d, _row(w["g_mix"][1]), w["w_in_c"], w["conv_w_c"][0], e_buf, w["w_out_c"],
                           _row(w["g_mem"][1]), w["w_mem_q"][1], act_dtype, False, bsz, seq, tm)
        sconv_state = p.reshape(bsz, seq, D_MODEL)[:, seq - (_SC_CONV - 1):]
    x2d = attend(q, x2d, 1)
    x2d = _moe(x2d, _row(w["g_ffn"][1]), w["w_router"], w["w_exp_gate"], w["w_exp_up"], w["w_exp_down"],
               _row(w["g_final"]), True, min(_MOE_TILE, m))
    y = x2d.reshape(bsz, seq, D_MODEL)
    ssm_state = ssm_state.reshape(1, bsz, SSM_HEADS, SSM_HEAD_DIM, SSM_STATE)
    return y, ssm_state, conv_state[None], sconv_state[None], v_out


def kernel(x_prompt, x_sample, mem_prompt, state_ssm, state_ssm_conv, state_sconv, cache_mem_k, cache_mem_v, g_mix, g_mem, g_ffn, g_final, w_in_ab, w_spatial, b_spatial, conv_w_ssm, conv_b_ssm, dt_bias, a_log, d_skip, g_ssm_norm, w_out_ab, w_ffn_gate, w_ffn_up, w_ffn_down, w_in_c, conv_w_c, w_out_c, w_router, w_exp_gate, w_exp_up, w_exp_down, w_mem_q, w_mem_k, w_mem_v, w_mem_o):
    depth = w_mem_q.shape[0]
    bp = x_prompt.shape[0]
    bs = x_sample.shape[0]
    n_main = 3 * D_MODEL + SSM_CONV_DIM
    bf = lambda a: a.astype(_BF)
    w = dict(
        g_mix=g_mix, g_mem=g_mem, g_ffn=g_ffn, g_final=g_final, w_spatial=w_spatial, b_spatial=b_spatial,
        conv_w_ssm=conv_w_ssm, conv_b_ssm=conv_b_ssm, dt_bias=dt_bias, a_log=a_log, d_skip=d_skip, g_ssm_norm=g_ssm_norm,
        conv_w_c=conv_w_c,
        w_in_main=bf(w_in_ab[0][:, :n_main]),
        w_in_dt=bf(jnp.pad(w_in_ab[0][:, n_main:], ((0, 0), (0, LANES - SSM_HEADS)))),
        w_out_ab=bf(w_out_ab[0]), w_ffn_gate=bf(w_ffn_gate), w_ffn_up=bf(w_ffn_up), w_ffn_down=bf(w_ffn_down),
        w_in_c=bf(w_in_c[0]), w_out_c=bf(w_out_c[0]),
        w_router=jnp.pad(w_router[0], ((0, 0), (0, LANES - N_EXPERTS))),
        w_exp_gate=bf(w_exp_gate[0]), w_exp_up=bf(w_exp_up[0]), w_exp_down=bf(w_exp_down[0]),
        w_mem_q=bf(w_mem_q), w_mem_o=bf(w_mem_o),
    )
    mem_k_p, mem_v_p = _mem_kv(mem_prompt.reshape(bp * N_MEM, D_MODEL), bf(w_mem_k), bf(w_mem_v))
    y_p, ssm_p, ssmconv_p, sconv_p, _ = _trunk(
        x_prompt, mem_k_p.reshape(depth * bp, N_MEM, D_MODEL), mem_v_p.reshape(depth * bp, N_MEM, D_MODEL), True, None, w)
    states = dict(ssm=state_ssm[0], ssm_conv=state_ssm_conv[0], sconv=state_sconv[0])
    y_s, ssm_s, ssmconv_s, sconv_s, v_s = _trunk(
        x_sample, cache_mem_k.reshape(depth * bs, N_MEM, MEM_HEADS, MEM_HEAD_DIM),
        cache_mem_v.reshape(depth * bs, N_MEM, MEM_HEADS, MEM_HEAD_DIM), False, states, w)
    kv_shape = (depth, bp, N_MEM, MEM_HEADS, MEM_HEAD_DIM)
    return (y_p, y_s, ssm_p, ssm_s, ssmconv_p, ssmconv_s, sconv_p, sconv_s,
            mem_k_p.reshape(kv_shape), mem_v_p.reshape(kv_shape), v_s[None])
```

```python
import functools

import jax
import jax.numpy as jnp
from jax import lax
from jax.experimental import pallas as pl
from jax.experimental.pallas import tpu as pltpu

D_MODEL = 1024
EPS = 1e-6
CHUNK = 128
A_GROUPS = 8
A_HEAD = D_MODEL // A_GROUPS
SSM_HEADS = 16
SSM_HEAD_DIM = 64
SSM_GROUPS = 2
SSM_STATE = 128
SSM_CONV = 4
SSM_DIM = D_MODEL
SSM_CONV_DIM = SSM_DIM + 2 * SSM_GROUPS * SSM_STATE
N_MEM = 256
MEM_HEADS = 4
MEM_HEAD_DIM = D_MODEL // MEM_HEADS
N_EXPERTS = 8

LANES = 128
SUBLANES = 8
VMEM_LIMIT = 56 * 1024 * 1024

_BF = jnp.bfloat16
_F32 = jnp.float32
_HI = lax.Precision.HIGHEST
_NT = (((1,), (1,)), ((), ()))


def _cparams(n_axes):
    return pltpu.CompilerParams(dimension_semantics=("arbitrary",) * n_axes, vmem_limit_bytes=VMEM_LIMIT)


def _resident(shape):
    nd = len(shape)
    return pl.BlockSpec(shape, lambda *_: (0,) * nd, pipeline_mode=pl.Buffered(1))


def _rms(x, g):
    ms = jnp.mean(x * x, axis=-1, keepdims=True)
    return x * lax.rsqrt(ms + EPS) * g


def _silu(x):
    return x * jax.nn.sigmoid(x)


def _dot(a, b):
    return jnp.dot(a, b, preferred_element_type=_F32)


def _softplus(x):
    return jnp.maximum(x, 0.0) + jnp.log1p(jnp.exp(-jnp.abs(x)))


def _pos_and_seq(rows, seq):
    r = lax.broadcasted_iota(jnp.int32, (rows, 1), 0)
    if seq & (seq - 1) == 0:
        return r & (seq - 1), r >> (seq.bit_length() - 1)
    return lax.rem(r, seq), lax.div(r, seq)


def _kv_body(m_ref, wk_ref, wv_ref, k_ref, v_ref, kb_ref, vb_ref):
    m = m_ref[0].astype(_BF)
    for w_ref, o_ref, ob_ref in ((wk_ref, k_ref, kb_ref), (wv_ref, v_ref, vb_ref)):
        res = _dot(m, w_ref[0])
        ob_ref[0] = res.astype(_BF)
        for h in range(MEM_HEADS):
            o_ref[0, 0, :, h, :] = res[:, h * MEM_HEAD_DIM:(h + 1) * MEM_HEAD_DIM]


def _mem_kv(mem, wk, wv):
    bsz, n, d = mem.shape
    depth = wk.shape[0]
    out = jax.ShapeDtypeStruct((depth, bsz, n, MEM_HEADS, MEM_HEAD_DIM), _F32)
    out_b = jax.ShapeDtypeStruct((depth * bsz, n, d), _BF)
    wspec = pl.BlockSpec((1, d, d), lambda l, b: (l, 0, 0))
    ospec = pl.BlockSpec((1, 1, n, MEM_HEADS, MEM_HEAD_DIM), lambda l, b: (l, b, 0, 0, 0))
    bspec = pl.BlockSpec((1, n, d), lambda l, b: (l * bsz + b, 0, 0))
    return pl.pallas_call(
        _kv_body, out_shape=(out, out, out_b, out_b), grid=(depth, bsz),
        in_specs=[pl.BlockSpec((1, n, d), lambda l, b: (b, 0, 0)), wspec, wspec],
        out_specs=(ospec, ospec, bspec, bspec), compiler_params=_cparams(2), name="mem_kv",
    )(mem, wk, wv)


_PROJ_CHUNK = 512


def _inproj_ab_body(x_ref, g_ref, w_ref, wdt_ref, u_ref, v_ref, z_ref, xbc_ref, dt_ref):
    hn = _rms(x_ref[...], g_ref[...]).astype(_BF)
    col = 0
    for ref, width, act in ((u_ref, D_MODEL, True), (v_ref, D_MODEL, True), (z_ref, D_MODEL, False),
                            (xbc_ref, SSM_CONV_DIM, False)):
        for c in range(width // _PROJ_CHUNK):
            r = _dot(hn, w_ref[:, col:col + _PROJ_CHUNK])
            if act:
                r = jax.nn.gelu(r)
            ref[:, c * _PROJ_CHUNK:(c + 1) * _PROJ_CHUNK] = r.astype(ref.dtype)
            col += _PROJ_CHUNK
    dt_ref[...] = _dot(hn, wdt_ref[...])


def _inproj_ab(x2d, g, w_main, w_dt, act_dtype, tm):
    m = x2d.shape[0]
    row = lambda width: pl.BlockSpec((tm, width), lambda i: (i, 0))
    outs = (jax.ShapeDtypeStruct((m, D_MODEL), act_dtype),) * 3 + (
        jax.ShapeDtypeStruct((m, SSM_CONV_DIM), _F32), jax.ShapeDtypeStruct((m, LANES), _F32))
    return pl.pallas_call(
        _inproj_ab_body, out_shape=outs, grid=(m // tm,),
        in_specs=[row(D_MODEL), _resident(g.shape), _resident(w_main.shape), _resident(w_dt.shape)],
        out_specs=(row(D_MODEL), row(D_MODEL), row(D_MODEL), row(SSM_CONV_DIM), row(LANES)),
        compiler_params=_cparams(1), name="inproj_ab",
    )(x2d, g, w_main, w_dt)


_PAIR = 2 * SSM_HEAD_DIM
_HPG = SSM_HEADS // SSM_GROUPS
_GRP_DIM = SSM_DIM // SSM_GROUPS


def _mix_ab_prompt_body(u_ref, v_ref, z_ref, xbc_ref, dt_ref, ws_ref, bs_ref, cw_ref, cb_ref, dtb_ref, alog_ref,
                        dskip_ref, gn_ref, y_ref, st_ref, cst_ref, xp_ref, sT_ref):
    q = CHUNK
    c = pl.program_id(1)

    @pl.when(c == 0)
    def _():
        xp_ref[0:SUBLANES, :] = jnp.zeros((SUBLANES, SSM_CONV_DIM), _F32)
        sT_ref[...] = jnp.zeros_like(sT_ref)

    tri = lax.broadcasted_iota(jnp.int32, (q, q), 0) >= lax.broadcasted_iota(jnp.int32, (q, q), 1)

    for g in range(A_GROUPS):
        sl = slice(g * A_HEAD, (g + 1) * A_HEAD)
        w = jnp.where(tri, ws_ref[g], 0.0).astype(_BF)
        s = _dot(w, v_ref[:, sl].astype(_BF)) + bs_ref[g]
        y_ref[:, sl] = (u_ref[:, sl].astype(_F32) * s).astype(y_ref.dtype)

    xbc = xbc_ref[...]
    xp_ref[SUBLANES:SUBLANES + q, :] = xbc
    base = SUBLANES - (SSM_CONV - 1)
    conv = xp_ref[base:base + q, :] * cw_ref[0:1, :]
    for k in range(1, SSM_CONV):
        conv = conv + xp_ref[base + k:base + k + q, :] * cw_ref[k:k + 1, :]
    xp_ref[0:SUBLANES, :] = xbc[q - SUBLANES:q, :]
    act = _silu(conv + cb_ref[...])
    xs = act[:, :SSM_DIM]
    bm = act[:, SSM_DIM:SSM_DIM + SSM_GROUPS * SSM_STATE]
    cm = act[:, SSM_DIM + SSM_GROUPS * SSM_STATE:]

    dt = _softplus(dt_ref[...] + dtb_ref[...])
    d_a = dt * (-jnp.exp(alog_ref[...]))
    la = jnp.dot(tri.astype(_F32), d_a, precision=_HI, preferred_element_type=_F32)
    la_t = la.T
    dt_t = dt.T
    e_la = jnp.exp(la)
    last = la[q - 1:q, :]
    wend = jnp.exp(last - la) * dt
    e_last = jnp.exp(last)
    lo = lax.broadcasted_iota(jnp.int32, (q, _PAIR), 1) < SSM_HEAD_DIM
    lo_row = lo[0:1, :]

    for g in range(SSM_GROUPS):
        bm_t = bm[:, g * SSM_STATE:(g + 1) * SSM_STATE].T.astype(_BF)
        cm_g = cm[:, g * SSM_STATE:(g + 1) * SSM_STATE].astype(_BF)
        cb = _dot(cm_g, bm_t)
        ys = []
        for pp in range(_HPG // 2):
            p = g * (_HPG // 2) + pp
            cols = slice(p * _PAIR, (p + 1) * _PAIR)
            wts = []
            for h in (2 * p, 2 * p + 1):
                seg = la[:, h:h + 1] - la_t[h:h + 1, :]
                wts.append(cb * jnp.exp(jnp.where(tri, seg, -jnp.inf)) * dt_t[h:h + 1, :])
            w2 = jnp.concatenate(wts, axis=1).astype(_BF)
            x_pair = xs[:, cols]
            x2 = jnp.concatenate([jnp.where(lo, x_pair, 0.0), jnp.where(lo, 0.0, x_pair)], axis=0).astype(_BF)
            s_t = sT_ref[:, cols]
            e_pair = jnp.where(lo, e_la[:, 2 * p:2 * p + 1], e_la[:, 2 * p + 1:2 * p + 2])
            y_pair = _dot(w2, x2) + _dot(cm_g, s_t.astype(_BF)) * e_pair
            wd_pair = jnp.where(lo, wend[:, 2 * p:2 * p + 1], wend[:, 2 * p + 1:2 * p + 2])
            el_pair = jnp.where(lo_row, e_last[:, 2 * p:2 * p + 1], e_last[:, 2 * p + 1:2 * p + 2])
            sT_ref[:, cols] = s_t * el_pair + _dot(bm_t, (x_pair * wd_pair).astype(_BF))
            ys.append(y_pair + x_pair * dskip_ref[:, cols])
        gcols = slice(g * _GRP_DIM, (g + 1) * _GRP_DIM)
        gated = jnp.concatenate(ys, axis=1) * _silu(z_ref[:, gcols].astype(_F32))
        y_ref[:, D_MODEL + g * _GRP_DIM:D_MODEL + (g + 1) * _GRP_DIM] = _rms(gated, gn_ref[:, gcols]).astype(y_ref.dtype)

    @pl.when(c == pl.num_programs(1) - 1)
    def _():
        for p in range(SSM_HEADS // 2):
            st_ref[0, p * _PAIR:(p + 1) * _PAIR, :] = sT_ref[:, p * _PAIR:(p + 1) * _PAIR].T
        cst_ref[0] = xp_ref[SUBLANES + q - (SSM_CONV - 1):SUBLANES + q, :]


def _mix_ab_prompt(u, v, z, xbc, dt, bsz, seq, ws, bs, cw, cb, dtb, alog, dskip, gn):
    nc = seq // CHUNK
    row = lambda width: pl.BlockSpec((CHUNK, width), lambda b, c: (b * nc + c, 0))
    small = [ws, bs, cw, cb, dtb, alog, dskip, gn]
    outs = (jax.ShapeDtypeStruct((bsz * seq, 2 * D_MODEL), _BF),
            jax.ShapeDtypeStruct((bsz, SSM_DIM, SSM_STATE), _F32),
            jax.ShapeDtypeStruct((bsz, SSM_CONV - 1, SSM_CONV_DIM), _F32))
    return pl.pallas_call(
        _mix_ab_prompt_body, out_shape=outs, grid=(bsz, nc),
        in_specs=[row(D_MODEL), row(D_MODEL), row(D_MODEL), row(SSM_CONV_DIM), row(LANES)]
        + [_resident(a.shape) for a in small],
        out_specs=(row(2 * D_MODEL),
                   pl.BlockSpec((1, SSM_DIM, SSM_STATE), lambda b, c: (b, 0, 0)),
                   pl.BlockSpec((1, SSM_CONV - 1, SSM_CONV_DIM), lambda b, c: (b, 0, 0))),
        scratch_shapes=[pltpu.VMEM((SUBLANES + CHUNK, SSM_CONV_DIM), _F32), pltpu.VMEM((SSM_STATE, SSM_DIM), _F32)],
        compiler_params=_cparams(2), name="mix_ab_prompt",
    )(u, v, z, xbc, dt, *small)


_SEQ_BLK = 8


def _mix_ab_sample_body(seq, u_ref, v_ref, z_ref, xbc_ref, dt_ref, e_ref, st_ref, coef_ref, sb_ref, cw_ref, cb_ref,
                        dtb_ref, alog_ref, dskip_ref, gn_ref, y_ref, sto_ref, sh_ref):
    rows = _SEQ_BLK * seq
    t, seq_of_row = _pos_and_seq(rows, seq)
    pad_rows = sh_ref.shape[0]
    sh_ref[...] = jnp.zeros((pad_rows, SSM_CONV_DIM), _F32)

    def shifts(val, deltas):
        width = val.shape[1]
        sh_ref[SUBLANES:SUBLANES + rows, 0:width] = val
        return [sh_ref[SUBLANES - d:SUBLANES - d + rows, 0:width] for d in deltas]

    back = tuple(range(1, seq))

    v = v_ref[...]
    s = coef_ref[0] * v + sb_ref[...]
    for d, vs in zip(back, shifts(v, back)):
        s = s + coef_ref[d] * jnp.where(t >= d, vs, 0.0)
    y_ref[:, 0:D_MODEL] = u_ref[...] * s

    xbc = xbc_ref[...]
    conv = xbc * cw_ref[SSM_CONV - 1:SSM_CONV, :]
    for j, xsft in zip(range(1, SSM_CONV), shifts(xbc, tuple(range(1, SSM_CONV)))):
        conv = conv + (jnp.where(t >= j, xsft, 0.0) + e_ref[j - 1]) * cw_ref[SSM_CONV - 1 - j:SSM_CONV - j, :]
    act = _silu(conv + cb_ref[...])
    xs = act[:, :SSM_DIM]
    bm = act[:, SSM_DIM:SSM_DIM + SSM_GROUPS * SSM_STATE]
    cm = act[:, SSM_DIM + SSM_GROUPS * SSM_STATE:]

    dt = _softplus(dt_ref[...] + dtb_ref[...])
    d_a = dt * (-jnp.exp(alog_ref[...]))
    la = d_a
    for d, sft in zip(back, shifts(d_a, back)):
        la = la + jnp.where(t >= d, sft, 0.0)
    la_back = shifts(la, back)
    dt_back = shifts(dt, back)
    last = jnp.where(t == seq - 1, la, 0.0)
    fwd = shifts(la, tuple(-d for d in back))
    for d, sft in zip(back, fwd):
        last = last + jnp.where(t == seq - 1 - d, sft, 0.0)
    wend = jnp.exp(last - la) * dt
    e_last = jnp.exp(last)
    e_la = jnp.exp(la)

    head_lane = lax.broadcasted_iota(jnp.int32, (rows, LANES), 1)
    bm_back = shifts(bm, back)

    def cb_heads(b_other):
        per_g = [jnp.sum(cm[:, g * SSM_STATE:(g + 1) * SSM_STATE] * b_other[:, g * SSM_STATE:(g + 1) * SSM_STATE],
                         axis=-1, keepdims=True) for g in range(SSM_GROUPS)]
        return jnp.where(head_lane < _HPG, per_g[0], per_g[1])

    gs = [cb_heads(bm) * dt]
    for i, d in enumerate(back):
        gd = cb_heads(bm_back[i]) * jnp.exp(la - la_back[i]) * dt_back[i]
        gs.append(jnp.where(t >= d, gd, 0.0))

    head0 = lax.broadcasted_iota(jnp.int32, (LANES, SSM_DIM), 0) * SSM_HEAD_DIM
    chan = lax.broadcasted_iota(jnp.int32, (LANES, SSM_DIM), 1)
    expand = ((chan >= head0) & (chan < head0 + SSM_HEAD_DIM)).astype(_F32)
    stacked = jnp.concatenate([e_la, wend, e_last] + gs, axis=0)
    full = jnp.dot(stacked, expand, precision=_HI, preferred_element_type=_F32)
    e_la_f, wend_f, e_last_f = full[0:rows], full[rows:2 * rows], full[2 * rows:3 * rows]
    y2 = full[3 * rows:4 * rows] * xs
    for i, (d, xsft) in enumerate(zip(back, shifts(xs, back))):
        y2 = y2 + full[(4 + i) * rows:(5 + i) * rows] * jnp.where(t >= d, xsft, 0.0)

    zpad = jnp.zeros((LANES - rows, SSM_DIM), _F32)
    xw_t = jnp.concatenate([xs * wend_f, zpad], axis=0).T.astype(_BF)
    el_t = jnp.concatenate([e_last_f, zpad], axis=0).T
    row_lane = lax.broadcasted_iota(jnp.int32, (_GRP_DIM, LANES), 1)
    bpad = jnp.zeros((LANES - rows, SSM_STATE), _F32)

    y1 = [jnp.zeros((rows, _GRP_DIM), _F32) for _ in range(SSM_GROUPS)]
    for i in range(_SEQ_BLK):
        mine = seq_of_row == i
        for g in range(SSM_GROUPS):
            gr = slice(g * _GRP_DIM, (g + 1) * _GRP_DIM)
            st = st_ref[i, gr, :]
            b_g = jnp.where(mine, bm[:, g * SSM_STATE:(g + 1) * SSM_STATE], 0.0)
            b_g = jnp.concatenate([b_g, bpad], axis=0).astype(_BF)
            decay = jnp.sum(jnp.where(row_lane == i * seq, el_t[gr, :], 0.0), axis=1, keepdims=True)
            sto_ref[i, gr, :] = st * decay + _dot(xw_t[gr, :], b_g)
            c_g = cm[:, g * SSM_STATE:(g + 1) * SSM_STATE].astype(_BF)
            ch = lax.dot_general(c_g, st.astype(_BF), _NT, preferred_element_type=_F32)
            y1[g] = y1[g] + jnp.where(mine, ch, 0.0)

    for g in range(SSM_GROUPS):
        gr = slice(g * _GRP_DIM, (g + 1) * _GRP_DIM)
        ys = y1[g] * e_la_f[:, gr] + y2[:, gr] + xs[:, gr] * dskip_ref[:, gr]
        gated = ys * _silu(z_ref[:, gr])
        y_ref[:, D_MODEL + g * _GRP_DIM:D_MODEL + (g + 1) * _GRP_DIM] = _rms(gated, gn_ref[:, gr])


def _mix_ab_sample(u, v, z, xbc, dt, e_buf, state, seq, coef, sb, cw, cb, dtb, alog, dskip, gn):
    bsz = state.shape[0]
    rows = _SEQ_BLK * seq
    row = lambda width: pl.BlockSpec((rows, width), lambda i: (i, 0))
    small = [coef, sb, cw, cb, dtb, alog, dskip, gn]
    st_spec = pl.BlockSpec((_SEQ_BLK, SSM_DIM, SSM_STATE), lambda i: (i, 0, 0))
    outs = (jax.ShapeDtypeStruct((bsz * seq, 2 * D_MODEL), _F32), jax.ShapeDtypeStruct(state.shape, _F32))
    return pl.pallas_call(
        functools.partial(_mix_ab_sample_body, seq), out_shape=outs, grid=(bsz // _SEQ_BLK,),
        in_specs=[row(D_MODEL), row(D_MODEL), row(D_MODEL), row(SSM_CONV_DIM), row(LANES),
                  pl.BlockSpec((SSM_CONV - 1, rows, SSM_CONV_DIM), lambda i: (0, i, 0)), st_spec]
        + [_resident(a.shape) for a in small],
        out_specs=(row(2 * D_MODEL), st_spec),
        scratch_shapes=[pltpu.VMEM((rows + 2 * SUBLANES, SSM_CONV_DIM), _F32)],
        compiler_params=_cparams(1), name="mix_ab_sample",
    )(u, v, z, xbc, dt, e_buf, state, *small)


def _out_q_body(a_ref, x_ref, wo_ref, gm_ref, wq_ref, xo_ref, q_ref):
    xn = x_ref[...] + _dot(a_ref[...].astype(_BF), wo_ref[...])
    xo_ref[...] = xn
    q_ref[...] = _dot(_rms(xn, gm_ref[...]).astype(_BF), wq_ref[...]).astype(q_ref.dtype)


def _out_q(a, x2d, wo, gm, wq, q_dtype, tm):
    m = x2d.shape[0]
    row = lambda width: pl.BlockSpec((tm, width), lambda i: (i, 0))
    outs = (jax.ShapeDtypeStruct((m, D_MODEL), _F32), jax.ShapeDtypeStruct((m, D_MODEL), q_dtype))
    return pl.pallas_call(
        _out_q_body, out_shape=outs, grid=(m // tm,),
        in_specs=[row(a.shape[1]), row(D_MODEL), _resident(wo.shape), _resident(gm.shape), _resident(wq.shape)],
        out_specs=(row(D_MODEL), row(D_MODEL)), compiler_params=_cparams(1), name="out_q",
    )(a, x2d, wo, gm, wq)


_ATT_SCALE = MEM_HEAD_DIM ** -0.5


def _attend(q, k, v):
    outs = []
    for h in range(MEM_HEADS):
        sl = slice(h * MEM_HEAD_DIM, (h + 1) * MEM_HEAD_DIM)
        s = lax.dot_general(q[:, sl], k[:, sl], _NT, preferred_element_type=_F32) * _ATT_SCALE
        p = jnp.exp(s - jnp.max(s, axis=-1, keepdims=True))
        p = p / jnp.sum(p, axis=-1, keepdims=True)
        outs.append(_dot(p.astype(_BF), v[:, sl]))
    return jnp.concatenate(outs, axis=1)


def _attn_prompt_body(q_ref, k_ref, v_ref, x_ref, wo_ref, o_ref):
    o = _attend(q_ref[...], k_ref[0], v_ref[0])
    o_ref[...] = x_ref[...] + _dot(o.astype(_BF), wo_ref[...])


def _attn_prompt(q, k, v, x2d, wo, layer, bsz, seq, tq):
    nt = seq // tq
    row = pl.BlockSpec((tq, D_MODEL), lambda b, j: (b * nt + j, 0))
    kv = pl.BlockSpec((1, N_MEM, D_MODEL), lambda b, j: (layer * bsz + b, 0, 0))
    return pl.pallas_call(
        _attn_prompt_body, out_shape=jax.ShapeDtypeStruct(x2d.shape, _F32), grid=(bsz, nt),
        in_specs=[row, kv, kv, row, _resident(wo.shape)], out_specs=row,
        compiler_params=_cparams(2), name="attn_prompt",
    )(q, k, v, x2d, wo)


def _attn_sample_body(seq, q_ref, k_ref, v_ref, x_ref, wo_ref, o_ref, o_scr):
    per_tile = SUBLANES // seq
    _, seq_of_row = _pos_and_seq(SUBLANES, seq)
    n_lt = MEM_HEAD_DIM // LANES
    rows_mh = N_MEM * MEM_HEADS
    col = lax.broadcasted_iota(jnp.int32, (MEM_HEADS * SUBLANES, rows_mh), 1)
    row = lax.broadcasted_iota(jnp.int32, (MEM_HEADS * SUBLANES, rows_mh), 0)
    same_head = (col & (MEM_HEADS - 1)) == (row >> (SUBLANES.bit_length() - 1))
    for tile in range(_SEQ_BLK // per_tile):
        q8 = q_ref[tile * SUBLANES:(tile + 1) * SUBLANES, :].astype(_BF)
        o = None
        for j in range(per_tile):
            i = tile * per_tile + j
            s = None
            for lt in range(n_lt):
                qh = jnp.concatenate([q8[:, h * MEM_HEAD_DIM + lt * LANES:h * MEM_HEAD_DIM + (lt + 1) * LANES]
                                      for h in range(MEM_HEADS)], axis=0)
                k_lt = k_ref[i, :, :, lt * LANES:(lt + 1) * LANES].reshape(rows_mh, LANES).astype(_BF)
                part = lax.dot_general(qh, k_lt, _NT, preferred_element_type=_F32)
                s = part if s is None else s + part
            s = jnp.where(same_head, s * _ATT_SCALE, -jnp.inf)
            p = jnp.exp(s - jnp.max(s, axis=-1, keepdims=True))
            p = (p / jnp.sum(p, axis=-1, keepdims=True)).astype(_BF)
            pieces = [None] * (MEM_HEADS * n_lt)
            for lt in range(n_lt):
                v_lt = v_ref[i, :, :, lt * LANES:(lt + 1) * LANES].reshape(rows_mh, LANES).astype(_BF)
                o_lt = _dot(p, v_lt)
                for h in range(MEM_HEADS):
                    pieces[h * n_lt + lt] = o_lt[h * SUBLANES:(h + 1) * SUBLANES, :]
            oi = jnp.concatenate(pieces, axis=1)
            o = oi if o is None else jnp.where(seq_of_row == j, oi, o)
        o_scr[tile * SUBLANES:(tile + 1) * SUBLANES, :] = o
    o_ref[...] = x_ref[...] + _dot(o_scr[...].astype(_BF), wo_ref[...])


def _attn_sample(q, k, v, x2d, wo, layer, bsz, seq):
    rows = _SEQ_BLK * seq
    nb = bsz // _SEQ_BLK
    row = pl.BlockSpec((rows, D_MODEL), lambda i: (i, 0))
    kv = pl.BlockSpec((_SEQ_BLK, N_MEM, MEM_HEADS, MEM_HEAD_DIM), lambda i: (layer * nb + i, 0, 0, 0))
    return pl.pallas_call(
        functools.partial(_attn_sample_body, seq), out_shape=jax.ShapeDtypeStruct(x2d.shape, _F32), grid=(nb,),
        in_specs=[row, kv, kv, row, _resident(wo.shape)], out_specs=row,
        scratch_shapes=[pltpu.VMEM((rows, D_MODEL), _F32)],
        compiler_params=_cparams(1), name="attn_sample",
    )(q, k, v, x2d, wo)


_FF_GRID_CHUNKS = 2
_FF_SUB = 512
_MOE_TILE = 1024


def _swiglu_chunk(h, wg_ref, wu_ref, wd_ref):
    width = wg_ref.shape[2]
    out = None
    for lo in range(0, width, _FF_SUB):
        hi = min(lo + _FF_SUB, width)
        hid = _silu(_dot(h, wg_ref[0, :, lo:hi])) * _dot(h, wu_ref[0, :, lo:hi])
        part = _dot(hid.astype(_BF), wd_ref[0, lo:hi, :])
        out = part if out is None else out + part
    return out


def _ffn_body(x_ref, g_ref, wg_ref, wu_ref, wd_ref, o_ref, hf_ref):
    c = pl.program_id(1)

    @pl.when(c == 0)
    def _():
        x = x_ref[...]
        hf_ref[...] = _rms(x, g_ref[...]).astype(_BF)
        o_ref[...] = x

    o_ref[...] += _swiglu_chunk(hf_ref[...], wg_ref, wu_ref, wd_ref)


def _ffn(x2d, g, wg, wu, wd, tm):
    m = x2d.shape[0]
    d_ff = wg.shape[2]
    fc = d_ff // _FF_GRID_CHUNKS
    row = pl.BlockSpec((tm, D_MODEL), lambda i, c: (i, 0))
    up = pl.BlockSpec((1, D_MODEL, fc), lambda i, c: (0, 0, c))
    down = pl.BlockSpec((1, fc, D_MODEL), lambda i, c: (0, c, 0))
    return pl.pallas_call(
        _ffn_body, out_shape=jax.ShapeDtypeStruct(x2d.shape, _F32), grid=(m // tm, _FF_GRID_CHUNKS),
        in_specs=[row, _resident(g.shape), up, up, down], out_specs=row,
        scratch_shapes=[pltpu.VMEM((tm, D_MODEL), _BF)],
        compiler_params=_cparams(2), name="ffn_dense",
    )(x2d, g, wg, wu, wd)


_MOE_PICK_PROB = 2.0 / N_EXPERTS
_MOE_CAP_SIGMAS = 2.3
_BF16_ROWS = 16


def _moe_capacity(tm):
    mean = tm * _MOE_PICK_PROB
    sigma = (tm * _MOE_PICK_PROB * (1.0 - _MOE_PICK_PROB)) ** 0.5
    return int(-(-(mean + _MOE_CAP_SIGMAS * sigma) // _BF16_ROWS) * _BF16_ROWS)


def _moe_body(cap, cap_pad, final_norm, x_ref, g_ref, wr_ref, wg_ref, wu_ref, wd_ref, gf_ref, o_ref, hf_ref, comb_ref,
              rank_ref, rank_t_ref, cnt_ref):
    e = pl.program_id(1)
    tm = x_ref.shape[0]
    lane = lax.broadcasted_iota(jnp.int32, (tm, LANES), 1)

    @pl.when(e == 0)
    def _():
        x = x_ref[...]
        hf = _rms(x, g_ref[...])
        hf_ref[...] = hf.astype(_BF)
        o_ref[...] = x
        logits = jnp.dot(hf, wr_ref[...], precision=_HI, preferred_element_type=_F32)
        lg = jnp.where(lane < N_EXPERTS, logits, -jnp.inf)
        v1 = jnp.max(lg, axis=-1, keepdims=True)
        i1 = jnp.min(jnp.where(lg == v1, lane, LANES), axis=-1, keepdims=True)
        lg2 = jnp.where(lane == i1, -jnp.inf, lg)
        v2 = jnp.max(lg2, axis=-1, keepdims=True)
        i2 = jnp.min(jnp.where(lg2 == v2, lane, LANES), axis=-1, keepdims=True)
        e2 = jnp.exp(v2 - v1)
        comb_ref[...] = jnp.where(lane == i1, 1.0 / (1.0 + e2), 0.0) + jnp.where(lane == i2, e2 / (1.0 + e2), 0.0)
        picked = (lane == i1) | (lane == i2)
        picked_b = picked.astype(_BF)
        blk_r = lax.broadcasted_iota(jnp.int32, (LANES, LANES), 0)
        blk_c = lax.broadcasted_iota(jnp.int32, (LANES, LANES), 1)
        earlier = (blk_c < blk_r).astype(_BF)
        running = jnp.zeros((1, LANES), _F32)
        for b in range(tm // LANES):
            rows = slice(b * LANES, (b + 1) * LANES)
            pb = picked_b[rows, :]
            rank_b = jnp.where(picked[rows, :], _dot(earlier, pb) + running, -1.0)
            rank_ref[rows, :] = rank_b
            rank_t_ref[:, rows] = rank_b.T
            running = running + jnp.sum(pb.astype(_F32), axis=0, keepdims=True)
        cnt_ref[...] = running

    lane_row = lax.broadcasted_iota(jnp.int32, (1, LANES), 1)
    count = jnp.sum(jnp.where(lane_row == e, cnt_ref[...], 0.0)).astype(jnp.int32)

    @pl.when(count <= cap)
    def _():
        rank_row = rank_t_ref[pl.ds(e, 1), :]
        slot_col = lax.broadcasted_iota(jnp.int32, (cap, 1), 0).astype(_F32)
        pick = (rank_row == slot_col).astype(_BF)
        xs = _dot(pick, hf_ref[...]).astype(_BF)
        y = _swiglu_chunk(xs, wg_ref, wu_ref, wd_ref).astype(_BF)
        y = jnp.concatenate([y, jnp.zeros((cap_pad - cap, D_MODEL), _BF)], axis=0)
        rank_col = jnp.sum(jnp.where(lane == e, rank_ref[...], 0.0), axis=1, keepdims=True)
        comb_col = jnp.sum(jnp.where(lane == e, comb_ref[...], 0.0), axis=1, keepdims=True)
        slot_row = lax.broadcasted_iota(jnp.int32, (1, cap_pad), 1).astype(_F32)
        put = (rank_col == slot_row).astype(_BF)
        o_ref[...] += comb_col * _dot(put, y)

    @pl.when(count > cap)
    def _():
        comb_col = jnp.sum(jnp.where(lane == e, comb_ref[...], 0.0), axis=1, keepdims=True)
        o_ref[...] += comb_col * _swiglu_chunk(hf_ref[...], wg_ref, wu_ref, wd_ref)

    if final_norm:
        @pl.when(e == pl.num_programs(1) - 1)
        def _():
            o_ref[...] = _rms(o_ref[...], gf_ref[...])


def _moe(x2d, g, wr, wg, wu, wd, gf, final_norm, tm):
    m = x2d.shape[0]
    n_e, _, d_ff = wg.shape
    assert tm % LANES == 0 and m % tm == 0, (m, tm)
    cap = _moe_capacity(tm)
    cap_pad = -(-cap // LANES) * LANES
    row = pl.BlockSpec((tm, D_MODEL), lambda i, e: (i, 0))
    row_once = pl.BlockSpec((tm, D_MODEL), lambda i, e: (i, 0), pipeline_mode=pl.Buffered(1))
    up = pl.BlockSpec((1, D_MODEL, d_ff), lambda i, e: (e, 0, 0))
    down = pl.BlockSpec((1, d_ff, D_MODEL), lambda i, e: (e, 0, 0))
    return pl.pallas_call(
        functools.partial(_moe_body, cap, cap_pad, final_norm), out_shape=jax.ShapeDtypeStruct(x2d.shape, _F32),
        grid=(m // tm, n_e),
        in_specs=[row_once, _resident(g.shape), _resident(wr.shape), up, up, down, _resident(gf.shape)],
        out_specs=row_once,
        scratch_shapes=[pltpu.VMEM((tm, D_MODEL), _BF), pltpu.VMEM((tm, LANES), _F32), pltpu.VMEM((tm, LANES), _F32),
                        pltpu.VMEM((LANES, tm), _F32), pltpu.VMEM((1, LANES), _F32)],
        compiler_params=_cparams(2), name="moe",
    )(x2d, g, wr, wg, wu, wd, gf)


_SC_CONV = 3


def _mix_c_body(per_seq, seq, x_ref, g_ref, win_ref, cw_ref, e_ref, wout_ref, gm_ref, wq_ref, xo_ref, q_ref, p_ref,
                pp_ref):
    tm = x_ref.shape[0]
    x = x_ref[...]
    hn = _rms(x, g_ref[...]).astype(_BF)
    bg = _dot(hn, win_ref[:, 0:D_MODEL])
    p = _dot(hn, win_ref[:, D_MODEL:2 * D_MODEL]) * _dot(hn, win_ref[:, 2 * D_MODEL:3 * D_MODEL])
    if per_seq:
        @pl.when(pl.program_id(1) == 0)
        def _():
            pp_ref[0:SUBLANES, :] = jnp.zeros((SUBLANES, D_MODEL), _F32)
    else:
        pp_ref[0:SUBLANES, :] = jnp.zeros((SUBLANES, D_MODEL), _F32)
    pp_ref[SUBLANES:SUBLANES + tm, :] = p
    p1 = pp_ref[SUBLANES - 1:SUBLANES - 1 + tm, :]
    p2 = pp_ref[SUBLANES - 2:SUBLANES - 2 + tm, :]
    if per_seq:
        pp_ref[0:SUBLANES, :] = p[tm - SUBLANES:tm, :]
    else:
        t, _ = _pos_and_seq(tm, seq)
        p1 = jnp.where(t >= 1, p1, 0.0) + e_ref[0]
        p2 = jnp.where(t >= 2, p2, 0.0) + e_ref[1]
    conv = p2 * cw_ref[0:1, :] + p1 * cw_ref[1:2, :] + p * cw_ref[2:3, :]
    xn = x + _dot((bg * conv).astype(_BF), wout_ref[...])
    xo_ref[...] = xn
    q_ref[...] = _dot(_rms(xn, gm_ref[...]).astype(_BF), wq_ref[...]).astype(q_ref.dtype)
    if per_seq:
        @pl.when(pl.program_id(1) == pl.num_programs(1) - 1)
        def _():
            p_ref[0] = pp_ref[SUBLANES + tm - (_SC_CONV - 1):SUBLANES + tm, :]
    else:
        p_ref[...] = p


def _mix_c(x2d, g, win, cw, e_buf, wout, gm, wq, q_dtype, per_seq, bsz, seq, tm):
    m = x2d.shape[0]
    if per_seq:
        nt = seq // tm
        grid = (bsz, nt)
        imap = lambda b, j: (b * nt + j, 0)
        e_spec = pl.BlockSpec((_SC_CONV - 1, SUBLANES, D_MODEL), lambda b, j: (0, 0, 0))
        p_shape = jax.ShapeDtypeStruct((bsz, _SC_CONV - 1, D_MODEL), _F32)
        p_spec = pl.BlockSpec((1, _SC_CONV - 1, D_MODEL), lambda b, j: (b, 0, 0))
    else:
        grid = (m // tm, 1)
        imap = lambda i, j: (i, 0)
        e_spec = pl.BlockSpec((_SC_CONV - 1, tm, D_MODEL), lambda i, j: (0, i, 0))
        p_shape = jax.ShapeDtypeStruct((m, D_MODEL), _F32)
        p_spec = pl.BlockSpec((tm, D_MODEL), imap)
    row = pl.BlockSpec((tm, D_MODEL), imap)
    outs = (jax.ShapeDtypeStruct((m, D_MODEL), _F32), jax.ShapeDtypeStruct((m, D_MODEL), q_dtype), p_shape)
    return pl.pallas_call(
        functools.partial(_mix_c_body, per_seq, seq), out_shape=outs, grid=grid,
        in_specs=[row, _resident(g.shape), _resident(win.shape), _resident(cw.shape), e_spec, _resident(wout.shape),
                  _resident(gm.shape), _resident(wq.shape)],
        out_specs=(row, row, p_spec),
        scratch_shapes=[pltpu.VMEM((SUBLANES + tm, D_MODEL), _F32)],
        compiler_params=_cparams(2), name="mix_c",
    )(x2d, g, win, cw, e_buf, wout, gm, wq)


def _expand_buf(buf, seq, n_back):
    bsz, km1, ch = buf.shape
    outs = []
    for j in range(1, n_back + 1):
        rows = [buf[:, km1 - j + t] if t < j else jnp.zeros((bsz, ch), buf.dtype) for t in range(seq)]
        outs.append(jnp.stack(rows, axis=1).reshape(bsz * seq, ch))
    return jnp.stack(outs)


def _row(vec, width=None):
    vec = vec.reshape(1, -1).astype(_F32)
    if width is not None and vec.shape[1] < width:
        vec = jnp.pad(vec, ((0, 0), (0, width - vec.shape[1])))
    return vec


def _trunk(x, k_all, v_all, prompt, states, w):
    bsz, seq, _ = x.shape
    m = bsz * seq
    x2d = x.reshape(m, D_MODEL)
    tm = min(512, m)
    dskip = _row(jnp.repeat(w["d_skip"][0], SSM_HEAD_DIM))
    common = (w["conv_w_ssm"][0], _row(w["conv_b_ssm"][0]), _row(w["dt_bias"][0], LANES), _row(w["a_log"][0], LANES),
              dskip, _row(w["g_ssm_norm"][0]))

    act_dtype = _BF if prompt else _F32
    u, v, z, xbc, dt = _inproj_ab(x2d, _row(w["g_mix"][0]), w["w_in_main"], w["w_in_dt"], act_dtype, tm)
    if prompt:
        y_ab, ssm_state, conv_state = _mix_ab_prompt(
            u, v, z, xbc, dt, bsz, seq, w["w_spatial"][0], w["b_spatial"][0][:, :, None], *common)
        v_out = None
    else:
        ws = w["w_spatial"][0][:, :seq, :seq]
        coef = []
        for d in range(seq):
            per_t = [ws[:, t, t - d] if t >= d else jnp.zeros((A_GROUPS,), _F32) for t in range(seq)]
            coef.append(jnp.tile(jnp.repeat(jnp.stack(per_t), A_HEAD, axis=1), (_SEQ_BLK, 1)))
        sb = jnp.tile(jnp.repeat(w["b_spatial"][0][:, :seq].T, A_HEAD, axis=1), (_SEQ_BLK, 1))
        e_buf = _expand_buf(states["ssm_conv"], seq, SSM_CONV - 1)
        y_ab, ssm_state = _mix_ab_sample(u, v, z, xbc, dt, e_buf, states["ssm"].reshape(bsz, SSM_DIM, SSM_STATE), seq,
                                         jnp.stack(coef), sb, *common)
        conv_state = xbc.reshape(bsz, seq, SSM_CONV_DIM)[:, seq - (SSM_CONV - 1):]
        v_out = v.reshape(bsz, seq, D_MODEL)
    x2d, q = _out_q(y_ab, x2d, w["w_out_ab"], _row(w["g_mem"][0]), w["w_mem_q"][0], act_dtype, tm)

    def attend(q, x2d, layer):
        if prompt:
            return _attn_prompt(q, k_all, v_all, x2d, w["w_mem_o"][layer], layer, bsz, seq, tm)
        return _attn_sample(q, k_all, v_all, x2d, w["w_mem_o"][layer], layer, bsz, seq)

    x2d = attend(q, x2d, 0)
    x2d = _ffn(x2d, _row(w["g_ffn"][0]), w["w_ffn_gate"], w["w_ffn_up"], w["w_ffn_down"], min(_MOE_TILE, m))

    if prompt:
        e_buf = jnp.zeros((_SC_CONV - 1, SUBLANES, D_MODEL), _F32)
        x2d, q, sconv_state = _mix_c(x2d, _row(w["g_mix"][1]), w["w_in_c"], w["conv_w_c"][0], e_buf, w["w_out_c"],
                                     _row(w["g_mem"][1]), w["w_mem_q"][1], act_dtype, True, bsz, seq, tm)
    else:
        e_buf = _expand_buf(states["sconv"], seq, _SC_CONV - 1)
        x2d, q, p = _mix_c(x2d, _row(w["g_mix"][1]), w["w_in_c"], w["conv_w_c"][0], e_buf, w["w_out_c"],
                           _row(w["g_mem"][1]), w["w_mem_q"][1], act_dtype, False, bsz, seq, tm)
        sconv_state = p.reshape(bsz, seq, D_MODEL)[:, seq - (_SC_CONV - 1):]
    x2d = attend(q, x2d, 1)
    x2d = _moe(x2d, _row(w["g_ffn"][1]), w["w_router"], w["w_exp_gate"], w["w_exp_up"], w["w_exp_down"],
               _row(w["g_final"]), True, min(_MOE_TILE, m))
    y = x2d.reshape(bsz, seq, D_MODEL)
    ssm_state = ssm_state.reshape(1, bsz, SSM_HEADS, SSM_HEAD_DIM, SSM_STATE)
    return y, ssm_state, conv_state[None], sconv_state[None], v_out


def kernel(x_prompt, x_sample, mem_prompt, state_ssm, state_ssm_conv, state_sconv, cache_mem_k, cache_mem_v, g_mix, g_mem, g_ffn, g_final, w_in_ab, w_spatial, b_spatial, conv_w_ssm, conv_b_ssm, dt_bias, a_log, d_skip, g_ssm_norm, w_out_ab, w_ffn_gate, w_ffn_up, w_ffn_down, w_in_c, conv_w_c, w_out_c, w_router, w_exp_gate, w_exp_up, w_exp_down, w_mem_q, w_mem_k, w_mem_v, w_mem_o):
    depth = w_mem_q.shape[0]
    bp = x_prompt.shape[0]
    bs = x_sample.shape[0]
    n_main = 3 * D_MODEL + SSM_CONV_DIM
    bf = lambda a: a.astype(_BF)
    w = dict(
        g_mix=g_mix, g_mem=g_mem, g_ffn=g_ffn, g_final=g_final, w_spatial=w_spatial, b_spatial=b_spatial,
        conv_w_ssm=conv_w_ssm, conv_b_ssm=conv_b_ssm, dt_bias=dt_bias, a_log=a_log, d_skip=d_skip, g_ssm_norm=g_ssm_norm,
        conv_w_c=conv_w_c,
        w_in_main=bf(w_in_ab[0][:, :n_main]),
        w_in_dt=bf(jnp.pad(w_in_ab[0][:, n_main:], ((0, 0), (0, LANES - SSM_HEADS)))),
        w_out_ab=bf(w_out_ab[0]), w_ffn_gate=bf(w_ffn_gate), w_ffn_up=bf(w_ffn_up), w_ffn_down=bf(w_ffn_down),
        w_in_c=bf(w_in_c[0]), w_out_c=bf(w_out_c[0]),
        w_router=jnp.pad(w_router[0], ((0, 0), (0, LANES - N_EXPERTS))),
        w_exp_gate=bf(w_exp_gate[0]), w_exp_up=bf(w_exp_up[0]), w_exp_down=bf(w_exp_down[0]),
        w_mem_q=bf(w_mem_q), w_mem_o=bf(w_mem_o),
    )
    mem_k_p, mem_v_p, k_bf, v_bf = _mem_kv(mem_prompt, bf(w_mem_k), bf(w_mem_v))
    y_p, ssm_p, ssmconv_p, sconv_p, _ = _trunk(x_prompt, k_bf, v_bf, True, None, w)
    states = dict(ssm=state_ssm[0], ssm_conv=state_ssm_conv[0], sconv=state_sconv[0])
    y_s, ssm_s, ssmconv_s, sconv_s, v_s = _trunk(
        x_sample, cache_mem_k.reshape(depth * bs, N_MEM, MEM_HEADS, MEM_HEAD_DIM),
        cache_mem_v.reshape(depth * bs, N_MEM, MEM_HEADS, MEM_HEAD_DIM), False, states, w)
    return (y_p, y_s, ssm_p, ssm_s, ssmconv_p, ssmconv_s, sconv_p, sconv_s, mem_k_p, mem_v_p, v_s[None])
```

```python
import functools

import jax
import jax.numpy as jnp
from jax import lax
from jax.experimental import pallas as pl
from jax.experimental.pallas import tpu as pltpu

D_MODEL = 1024
EPS = 1e-6
CHUNK = 128
A_GROUPS = 8
A_HEAD = D_MODEL // A_GROUPS
SSM_HEADS = 16
SSM_HEAD_DIM = 64
SSM_GROUPS = 2
SSM_STATE = 128
SSM_CONV = 4
SSM_DIM = D_MODEL
SSM_CONV_DIM = SSM_DIM + 2 * SSM_GROUPS * SSM_STATE
N_MEM = 256
MEM_HEADS = 4
MEM_HEAD_DIM = D_MODEL // MEM_HEADS
N_EXPERTS = 8

LANES = 128
SUBLANES = 8
VMEM_LIMIT = 56 * 1024 * 1024

_BF = jnp.bfloat16
_F32 = jnp.float32
_HI = lax.Precision.HIGHEST
_NT = (((1,), (1,)), ((), ()))


def _cparams(n_axes):
    return pltpu.CompilerParams(dimension_semantics=("arbitrary",) * n_axes, vmem_limit_bytes=VMEM_LIMIT)


def _resident(shape):
    nd = len(shape)
    return pl.BlockSpec(shape, lambda *_: (0,) * nd, pipeline_mode=pl.Buffered(1))


def _rms(x, g):
    ms = jnp.mean(x * x, axis=-1, keepdims=True)
    return x * lax.rsqrt(ms + EPS) * g


def _silu(x):
    return x * jax.nn.sigmoid(x)


def _dot(a, b):
    return jnp.dot(a, b, preferred_element_type=_F32)


def _softplus(x):
    return jnp.maximum(x, 0.0) + jnp.log1p(jnp.exp(-jnp.abs(x)))


def _pos_and_seq(rows, seq):
    r = lax.broadcasted_iota(jnp.int32, (rows, 1), 0)
    if seq & (seq - 1) == 0:
        return r & (seq - 1), r >> (seq.bit_length() - 1)
    return lax.rem(r, seq), lax.div(r, seq)


def _kv_body(m_ref, wk_ref, wv_ref, k_ref, v_ref, kb_ref, vb_ref):
    m = m_ref[0].astype(_BF)
    for w_ref, o_ref, ob_ref in ((wk_ref, k_ref, kb_ref), (wv_ref, v_ref, vb_ref)):
        res = _dot(m, w_ref[0])
        ob_ref[0] = res.astype(_BF)
        for h in range(MEM_HEADS):
            o_ref[0, 0, :, h, :] = res[:, h * MEM_HEAD_DIM:(h + 1) * MEM_HEAD_DIM]


def _mem_kv(mem, wk, wv):
    bsz, n, d = mem.shape
    depth = wk.shape[0]
    out = jax.ShapeDtypeStruct((depth, bsz, n, MEM_HEADS, MEM_HEAD_DIM), _F32)
    out_b = jax.ShapeDtypeStruct((depth * bsz, n, d), _BF)
    wspec = pl.BlockSpec((1, d, d), lambda l, b: (l, 0, 0))
    ospec = pl.BlockSpec((1, 1, n, MEM_HEADS, MEM_HEAD_DIM), lambda l, b: (l, b, 0, 0, 0))
    bspec = pl.BlockSpec((1, n, d), lambda l, b: (l * bsz + b, 0, 0))
    return pl.pallas_call(
        _kv_body, out_shape=(out, out, out_b, out_b), grid=(depth, bsz),
        in_specs=[pl.BlockSpec((1, n, d), lambda l, b: (b, 0, 0)), wspec, wspec],
        out_specs=(ospec, ospec, bspec, bspec), compiler_params=_cparams(2), name="mem_kv",
    )(mem, wk, wv)


_PROJ_CHUNK = 512


def _inproj_ab_body(x_ref, g_ref, w_ref, wdt_ref, u_ref, v_ref, z_ref, xbc_ref, dt_ref):
    hn = _rms(x_ref[...], g_ref[...]).astype(_BF)
    col = 0
    for ref, width, act in ((u_ref, D_MODEL, True), (v_ref, D_MODEL, True), (z_ref, D_MODEL, False),
                            (xbc_ref, SSM_CONV_DIM, False)):
        for c in range(width // _PROJ_CHUNK):
            r = _dot(hn, w_ref[:, col:col + _PROJ_CHUNK])
            if act:
                r = jax.nn.gelu(r)
            ref[:, c * _PROJ_CHUNK:(c + 1) * _PROJ_CHUNK] = r.astype(ref.dtype)
            col += _PROJ_CHUNK
    dt_ref[...] = _dot(hn, wdt_ref[...])


def _inproj_ab(x2d, g, w_main, w_dt, act_dtype, tm):
    m = x2d.shape[0]
    row = lambda width: pl.BlockSpec((tm, width), lambda i: (i, 0))
    outs = (jax.ShapeDtypeStruct((m, D_MODEL), act_dtype),) * 3 + (
        jax.ShapeDtypeStruct((m, SSM_CONV_DIM), _F32), jax.ShapeDtypeStruct((m, LANES), _F32))
    return pl.pallas_call(
        _inproj_ab_body, out_shape=outs, grid=(m // tm,),
        in_specs=[row(D_MODEL), _resident(g.shape), _resident(w_main.shape), _resident(w_dt.shape)],
        out_specs=(row(D_MODEL), row(D_MODEL), row(D_MODEL), row(SSM_CONV_DIM), row(LANES)),
        compiler_params=_cparams(1), name="inproj_ab",
    )(x2d, g, w_main, w_dt)


_PAIR = 2 * SSM_HEAD_DIM
_HPG = SSM_HEADS // SSM_GROUPS
_GRP_DIM = SSM_DIM // SSM_GROUPS


def _mix_ab_prompt_body(u_ref, v_ref, z_ref, xbc_ref, dt_ref, ws_ref, bs_ref, cw_ref, cb_ref, dtb_ref, alog_ref,
                        dskip_ref, gn_ref, y_ref, st_ref, cst_ref, xp_ref, sT_ref):
    q = CHUNK
    c = pl.program_id(1)

    @pl.when(c == 0)
    def _():
        xp_ref[0:SUBLANES, :] = jnp.zeros((SUBLANES, SSM_CONV_DIM), _F32)
        sT_ref[...] = jnp.zeros_like(sT_ref)

    tri = lax.broadcasted_iota(jnp.int32, (q, q), 0) >= lax.broadcasted_iota(jnp.int32, (q, q), 1)

    for g in range(A_GROUPS):
        sl = slice(g * A_HEAD, (g + 1) * A_HEAD)
        w = jnp.where(tri, ws_ref[g], 0.0).astype(_BF)
        s = _dot(w, v_ref[:, sl].astype(_BF)) + bs_ref[g]
        y_ref[:, sl] = (u_ref[:, sl].astype(_F32) * s).astype(y_ref.dtype)

    xbc = xbc_ref[...]
    xp_ref[SUBLANES:SUBLANES + q, :] = xbc
    base = SUBLANES - (SSM_CONV - 1)
    conv = xp_ref[base:base + q, :] * cw_ref[0:1, :]
    for k in range(1, SSM_CONV):
        conv = conv + xp_ref[base + k:base + k + q, :] * cw_ref[k:k + 1, :]
    xp_ref[0:SUBLANES, :] = xbc[q - SUBLANES:q, :]
    act = _silu(conv + cb_ref[...])
    xs = act[:, :SSM_DIM]
    bm = act[:, SSM_DIM:SSM_DIM + SSM_GROUPS * SSM_STATE]
    cm = act[:, SSM_DIM + SSM_GROUPS * SSM_STATE:]

    dt = _softplus(dt_ref[...] + dtb_ref[...])
    d_a = dt * (-jnp.exp(alog_ref[...]))
    la = jnp.dot(tri.astype(_F32), d_a, precision=_HI, preferred_element_type=_F32)
    la_t = la.T
    dt_t = dt.T
    e_la = jnp.exp(la)
    last = la[q - 1:q, :]
    wend = jnp.exp(last - la) * dt
    e_last = jnp.exp(last)
    lo = lax.broadcasted_iota(jnp.int32, (q, _PAIR), 1) < SSM_HEAD_DIM
    lo_row = lo[0:1, :]

    for g in range(SSM_GROUPS):
        bm_t = bm[:, g * SSM_STATE:(g + 1) * SSM_STATE].T.astype(_BF)
        cm_g = cm[:, g * SSM_STATE:(g + 1) * SSM_STATE].astype(_BF)
        cb = _dot(cm_g, bm_t)
        ys = []
        for pp in range(_HPG // 2):
            p = g * (_HPG // 2) + pp
            cols = slice(p * _PAIR, (p + 1) * _PAIR)
            wts = []
            for h in (2 * p, 2 * p + 1):
                seg = la[:, h:h + 1] - la_t[h:h + 1, :]
                wts.append(cb * jnp.exp(jnp.where(tri, seg, -jnp.inf)) * dt_t[h:h + 1, :])
            w2 = jnp.concatenate(wts, axis=1).astype(_BF)
            x_pair = xs[:, cols]
            x2 = jnp.concatenate([jnp.where(lo, x_pair, 0.0), jnp.where(lo, 0.0, x_pair)], axis=0).astype(_BF)
            s_t = sT_ref[:, cols]
            e_pair = jnp.where(lo, e_la[:, 2 * p:2 * p + 1], e_la[:, 2 * p + 1:2 * p + 2])
            y_pair = _dot(w2, x2) + _dot(cm_g, s_t.astype(_BF)) * e_pair
            wd_pair = jnp.where(lo, wend[:, 2 * p:2 * p + 1], wend[:, 2 * p + 1:2 * p + 2])
            el_pair = jnp.where(lo_row, e_last[:, 2 * p:2 * p + 1], e_last[:, 2 * p + 1:2 * p + 2])
            sT_ref[:, cols] = s_t * el_pair + _dot(bm_t, (x_pair * wd_pair).astype(_BF))
            ys.append(y_pair + x_pair * dskip_ref[:, cols])
        gcols = slice(g * _GRP_DIM, (g + 1) * _GRP_DIM)
        gated = jnp.concatenate(ys, axis=1) * _silu(z_ref[:, gcols].astype(_F32))
        y_ref[:, D_MODEL + g * _GRP_DIM:D_MODEL + (g + 1) * _GRP_DIM] = _rms(gated, gn_ref[:, gcols]).astype(y_ref.dtype)

    @pl.when(c == pl.num_programs(1) - 1)
    def _():
        for p in range(SSM_HEADS // 2):
            st_ref[0, p * _PAIR:(p + 1) * _PAIR, :] = sT_ref[:, p * _PAIR:(p + 1) * _PAIR].T
        cst_ref[0] = xp_ref[SUBLANES + q - (SSM_CONV - 1):SUBLANES + q, :]


def _mix_ab_prompt(u, v, z, xbc, dt, bsz, seq, ws, bs, cw, cb, dtb, alog, dskip, gn):
    nc = seq // CHUNK
    row = lambda width: pl.BlockSpec((CHUNK, width), lambda b, c: (b * nc + c, 0))
    small = [ws, bs, cw, cb, dtb, alog, dskip, gn]
    outs = (jax.ShapeDtypeStruct((bsz * seq, 2 * D_MODEL), _BF),
            jax.ShapeDtypeStruct((bsz, SSM_DIM, SSM_STATE), _F32),
            jax.ShapeDtypeStruct((bsz, SSM_CONV - 1, SSM_CONV_DIM), _F32))
    return pl.pallas_call(
        _mix_ab_prompt_body, out_shape=outs, grid=(bsz, nc),
        in_specs=[row(D_MODEL), row(D_MODEL), row(D_MODEL), row(SSM_CONV_DIM), row(LANES)]
        + [_resident(a.shape) for a in small],
        out_specs=(row(2 * D_MODEL),
                   pl.BlockSpec((1, SSM_DIM, SSM_STATE), lambda b, c: (b, 0, 0)),
                   pl.BlockSpec((1, SSM_CONV - 1, SSM_CONV_DIM), lambda b, c: (b, 0, 0))),
        scratch_shapes=[pltpu.VMEM((SUBLANES + CHUNK, SSM_CONV_DIM), _F32), pltpu.VMEM((SSM_STATE, SSM_DIM), _F32)],
        compiler_params=_cparams(2), name="mix_ab_prompt",
    )(u, v, z, xbc, dt, *small)


_SEQ_BLK = 8


def _mix_ab_sample_body(seq, u_ref, v_ref, z_ref, xbc_ref, dt_ref, e_ref, st_ref, coef_ref, sb_ref, cw_ref, cb_ref,
                        dtb_ref, alog_ref, dskip_ref, gn_ref, y_ref, sto_ref, sh_ref):
    rows = _SEQ_BLK * seq
    t, seq_of_row = _pos_and_seq(rows, seq)
    pad_rows = sh_ref.shape[0]
    sh_ref[...] = jnp.zeros((pad_rows, SSM_CONV_DIM), _F32)

    def shifts(val, deltas):
        width = val.shape[1]
        sh_ref[SUBLANES:SUBLANES + rows, 0:width] = val
        return [sh_ref[SUBLANES - d:SUBLANES - d + rows, 0:width] for d in deltas]

    back = tuple(range(1, seq))

    v = v_ref[...]
    s = coef_ref[0] * v + sb_ref[...]
    for d, vs in zip(back, shifts(v, back)):
        s = s + coef_ref[d] * jnp.where(t >= d, vs, 0.0)
    y_ref[:, 0:D_MODEL] = u_ref[...] * s

    xbc = xbc_ref[...]
    conv = xbc * cw_ref[SSM_CONV - 1:SSM_CONV, :]
    for j, xsft in zip(range(1, SSM_CONV), shifts(xbc, tuple(range(1, SSM_CONV)))):
        conv = conv + (jnp.where(t >= j, xsft, 0.0) + e_ref[j - 1]) * cw_ref[SSM_CONV - 1 - j:SSM_CONV - j, :]
    act = _silu(conv + cb_ref[...])
    xs = act[:, :SSM_DIM]
    bm = act[:, SSM_DIM:SSM_DIM + SSM_GROUPS * SSM_STATE]
    cm = act[:, SSM_DIM + SSM_GROUPS * SSM_STATE:]

    dt = _softplus(dt_ref[...] + dtb_ref[...])
    d_a = dt * (-jnp.exp(alog_ref[...]))
    la = d_a
    for d, sft in zip(back, shifts(d_a, back)):
        la = la + jnp.where(t >= d, sft, 0.0)
    la_back = shifts(la, back)
    dt_back = shifts(dt, back)
    last = jnp.where(t == seq - 1, la, 0.0)
    fwd = shifts(la, tuple(-d for d in back))
    for d, sft in zip(back, fwd):
        last = last + jnp.where(t == seq - 1 - d, sft, 0.0)
    wend = jnp.exp(last - la) * dt
    e_last = jnp.exp(last)
    e_la = jnp.exp(la)

    head_lane = lax.broadcasted_iota(jnp.int32, (rows, LANES), 1)
    bm_back = shifts(bm, back)

    def cb_heads(b_other):
        per_g = [jnp.sum(cm[:, g * SSM_STATE:(g + 1) * SSM_STATE] * b_other[:, g * SSM_STATE:(g + 1) * SSM_STATE],
                         axis=-1, keepdims=True) for g in range(SSM_GROUPS)]
        return jnp.where(head_lane < _HPG, per_g[0], per_g[1])

    gs = [cb_heads(bm) * dt]
    for i, d in enumerate(back):
        gd = cb_heads(bm_back[i]) * jnp.exp(la - la_back[i]) * dt_back[i]
        gs.append(jnp.where(t >= d, gd, 0.0))

    head0 = lax.broadcasted_iota(jnp.int32, (LANES, SSM_DIM), 0) * SSM_HEAD_DIM
    chan = lax.broadcasted_iota(jnp.int32, (LANES, SSM_DIM), 1)
    expand = ((chan >= head0) & (chan < head0 + SSM_HEAD_DIM)).astype(_F32)
    stacked = jnp.concatenate([e_la, wend, e_last] + gs, axis=0)
    full = jnp.dot(stacked, expand, precision=_HI, preferred_element_type=_F32)
    e_la_f, wend_f, e_last_f = full[0:rows], full[rows:2 * rows], full[2 * rows:3 * rows]
    y2 = full[3 * rows:4 * rows] * xs
    for i, (d, xsft) in enumerate(zip(back, shifts(xs, back))):
        y2 = y2 + full[(4 + i) * rows:(5 + i) * rows] * jnp.where(t >= d, xsft, 0.0)

    zpad = jnp.zeros((LANES - rows, SSM_DIM), _F32)
    xw_t = jnp.concatenate([xs * wend_f, zpad], axis=0).T.astype(_BF)
    el_t = jnp.concatenate([e_last_f, zpad], axis=0).T
    row_lane = lax.broadcasted_iota(jnp.int32, (_GRP_DIM, LANES), 1)
    bpad = jnp.zeros((LANES - rows, SSM_STATE), _F32)

    y1 = [jnp.zeros((rows, _GRP_DIM), _F32) for _ in range(SSM_GROUPS)]
    for i in range(_SEQ_BLK):
        mine = seq_of_row == i
        for g in range(SSM_GROUPS):
            gr = slice(g * _GRP_DIM, (g + 1) * _GRP_DIM)
            st = st_ref[i, gr, :]
            b_g = jnp.where(mine, bm[:, g * SSM_STATE:(g + 1) * SSM_STATE], 0.0)
            b_g = jnp.concatenate([b_g, bpad], axis=0).astype(_BF)
            decay = jnp.sum(jnp.where(row_lane == i * seq, el_t[gr, :], 0.0), axis=1, keepdims=True)
            sto_ref[i, gr, :] = st * decay + _dot(xw_t[gr, :], b_g)
            c_g = cm[:, g * SSM_STATE:(g + 1) * SSM_STATE].astype(_BF)
            ch = lax.dot_general(c_g, st.astype(_BF), _NT, preferred_element_type=_F32)
            y1[g] = y1[g] + jnp.where(mine, ch, 0.0)

    for g in range(SSM_GROUPS):
        gr = slice(g * _GRP_DIM, (g + 1) * _GRP_DIM)
        ys = y1[g] * e_la_f[:, gr] + y2[:, gr] + xs[:, gr] * dskip_ref[:, gr]
        gated = ys * _silu(z_ref[:, gr])
        y_ref[:, D_MODEL + g * _GRP_DIM:D_MODEL + (g + 1) * _GRP_DIM] = _rms(gated, gn_ref[:, gr])


def _mix_ab_sample(u, v, z, xbc, dt, e_buf, state, seq, coef, sb, cw, cb, dtb, alog, dskip, gn):
    bsz = state.shape[0]
    rows = _SEQ_BLK * seq
    row = lambda width: pl.BlockSpec((rows, width), lambda i: (i, 0))
    small = [coef, sb, cw, cb, dtb, alog, dskip, gn]
    st_spec = pl.BlockSpec((_SEQ_BLK, SSM_DIM, SSM_STATE), lambda i: (i, 0, 0))
    outs = (jax.ShapeDtypeStruct((bsz * seq, 2 * D_MODEL), _F32), jax.ShapeDtypeStruct(state.shape, _F32))
    return pl.pallas_call(
        functools.partial(_mix_ab_sample_body, seq), out_shape=outs, grid=(bsz // _SEQ_BLK,),
        in_specs=[row(D_MODEL), row(D_MODEL), row(D_MODEL), row(SSM_CONV_DIM), row(LANES),
                  pl.BlockSpec((SSM_CONV - 1, rows, SSM_CONV_DIM), lambda i: (0, i, 0)), st_spec]
        + [_resident(a.shape) for a in small],
        out_specs=(row(2 * D_MODEL), st_spec),
        scratch_shapes=[pltpu.VMEM((rows + 2 * SUBLANES, SSM_CONV_DIM), _F32)],
        compiler_params=_cparams(1), name="mix_ab_sample",
    )(u, v, z, xbc, dt, e_buf, state, *small)


def _out_q_body(a_ref, x_ref, wo_ref, gm_ref, wq_ref, xo_ref, q_ref):
    xn = x_ref[...] + _dot(a_ref[...].astype(_BF), wo_ref[...])
    xo_ref[...] = xn
    q_ref[...] = _dot(_rms(xn, gm_ref[...]).astype(_BF), wq_ref[...]).astype(q_ref.dtype)


def _out_q(a, x2d, wo, gm, wq, q_dtype, tm):
    m = x2d.shape[0]
    row = lambda width: pl.BlockSpec((tm, width), lambda i: (i, 0))
    outs = (jax.ShapeDtypeStruct((m, D_MODEL), _F32), jax.ShapeDtypeStruct((m, D_MODEL), q_dtype))
    return pl.pallas_call(
        _out_q_body, out_shape=outs, grid=(m // tm,),
        in_specs=[row(a.shape[1]), row(D_MODEL), _resident(wo.shape), _resident(gm.shape), _resident(wq.shape)],
        out_specs=(row(D_MODEL), row(D_MODEL)), compiler_params=_cparams(1), name="out_q",
    )(a, x2d, wo, gm, wq)


_ATT_SCALE = MEM_HEAD_DIM ** -0.5


def _attend(q, k, v):
    outs = []
    for h in range(MEM_HEADS):
        sl = slice(h * MEM_HEAD_DIM, (h + 1) * MEM_HEAD_DIM)
        s = lax.dot_general(q[:, sl], k[:, sl], _NT, preferred_element_type=_F32) * _ATT_SCALE
        p = jnp.exp(s - jnp.max(s, axis=-1, keepdims=True))
        p = p / jnp.sum(p, axis=-1, keepdims=True)
        outs.append(_dot(p.astype(_BF), v[:, sl]))
    return jnp.concatenate(outs, axis=1)


def _attn_prompt_body(q_ref, k_ref, v_ref, x_ref, wo_ref, o_ref):
    o = _attend(q_ref[...], k_ref[0], v_ref[0])
    o_ref[...] = x_ref[...] + _dot(o.astype(_BF), wo_ref[...])


def _attn_prompt(q, k, v, x2d, wo, layer, bsz, seq, tq):
    nt = seq // tq
    row = pl.BlockSpec((tq, D_MODEL), lambda b, j: (b * nt + j, 0))
    kv = pl.BlockSpec((1, N_MEM, D_MODEL), lambda b, j: (layer * bsz + b, 0, 0))
    return pl.pallas_call(
        _attn_prompt_body, out_shape=jax.ShapeDtypeStruct(x2d.shape, _F32), grid=(bsz, nt),
        in_specs=[row, kv, kv, row, _resident(wo.shape)], out_specs=row,
        compiler_params=_cparams(2), name="attn_prompt",
    )(q, k, v, x2d, wo)


def _attn_sample_body(seq, q_ref, k_ref, v_ref, x_ref, wo_ref, o_ref, o_scr):
    per_tile = SUBLANES // seq
    _, seq_of_row = _pos_and_seq(SUBLANES, seq)
    n_lt = MEM_HEAD_DIM // LANES
    rows_mh = N_MEM * MEM_HEADS
    col = lax.broadcasted_iota(jnp.int32, (MEM_HEADS * SUBLANES, rows_mh), 1)
    row = lax.broadcasted_iota(jnp.int32, (MEM_HEADS * SUBLANES, rows_mh), 0)
    same_head = (col & (MEM_HEADS - 1)) == (row >> (SUBLANES.bit_length() - 1))
    for tile in range(_SEQ_BLK // per_tile):
        q8 = q_ref[tile * SUBLANES:(tile + 1) * SUBLANES, :].astype(_BF)
        o = None
        for j in range(per_tile):
            i = tile * per_tile + j
            s = None
            for lt in range(n_lt):
                qh = jnp.concatenate([q8[:, h * MEM_HEAD_DIM + lt * LANES:h * MEM_HEAD_DIM + (lt + 1) * LANES]
                                      for h in range(MEM_HEADS)], axis=0)
                k_lt = k_ref[i, :, :, lt * LANES:(lt + 1) * LANES].reshape(rows_mh, LANES).astype(_BF)
                part = lax.dot_general(qh, k_lt, _NT, preferred_element_type=_F32)
                s = part if s is None else s + part
            s = jnp.where(same_head, s * _ATT_SCALE, -jnp.inf)
            p = jnp.exp(s - jnp.max(s, axis=-1, keepdims=True))
            p = (p / jnp.sum(p, axis=-1, keepdims=True)).astype(_BF)
            pieces = [None] * (MEM_HEADS * n_lt)
            for lt in range(n_lt):
                v_lt = v_ref[i, :, :, lt * LANES:(lt + 1) * LANES].reshape(rows_mh, LANES).astype(_BF)
                o_lt = _dot(p, v_lt)
                for h in range(MEM_HEADS):
                    pieces[h * n_lt + lt] = o_lt[h * SUBLANES:(h + 1) * SUBLANES, :]
            oi = jnp.concatenate(pieces, axis=1)
            o = oi if o is None else jnp.where(seq_of_row == j, oi, o)
        o_scr[tile * SUBLANES:(tile + 1) * SUBLANES, :] = o
    o_ref[...] = x_ref[...] + _dot(o_scr[...].astype(_BF), wo_ref[...])


def _attn_sample(q, k, v, x2d, wo, layer, bsz, seq):
    rows = _SEQ_BLK * seq
    nb = bsz // _SEQ_BLK
    row = pl.BlockSpec((rows, D_MODEL), lambda i: (i, 0))
    kv = pl.BlockSpec((_SEQ_BLK, N_MEM, MEM_HEADS, MEM_HEAD_DIM), lambda i: (layer * nb + i, 0, 0, 0))
    return pl.pallas_call(
        functools.partial(_attn_sample_body, seq), out_shape=jax.ShapeDtypeStruct(x2d.shape, _F32), grid=(nb,),
        in_specs=[row, kv, kv, row, _resident(wo.shape)], out_specs=row,
        scratch_shapes=[pltpu.VMEM((rows, D_MODEL), _F32)],
        compiler_params=_cparams(1), name="attn_sample",
    )(q, k, v, x2d, wo)


_FF_GRID_CHUNKS = 2
_FF_SUB = 512
_MOE_TILE = 1024


def _swiglu_chunk(h, wg_ref, wu_ref, wd_ref):
    width = wg_ref.shape[2]
    out = None
    for lo in range(0, width, _FF_SUB):
        hi = min(lo + _FF_SUB, width)
        hid = _silu(_dot(h, wg_ref[0, :, lo:hi])) * _dot(h, wu_ref[0, :, lo:hi])
        part = _dot(hid.astype(_BF), wd_ref[0, lo:hi, :])
        out = part if out is None else out + part
    return out


def _ffn_body(x_ref, g_ref, wg_ref, wu_ref, wd_ref, o_ref, hf_ref):
    c = pl.program_id(1)

    @pl.when(c == 0)
    def _():
        x = x_ref[...]
        hf_ref[...] = _rms(x, g_ref[...]).astype(_BF)
        o_ref[...] = x

    o_ref[...] += _swiglu_chunk(hf_ref[...], wg_ref, wu_ref, wd_ref)


def _ffn(x2d, g, wg, wu, wd, tm):
    m = x2d.shape[0]
    d_ff = wg.shape[2]
    fc = d_ff // _FF_GRID_CHUNKS
    row = pl.BlockSpec((tm, D_MODEL), lambda i, c: (i, 0))
    up = pl.BlockSpec((1, D_MODEL, fc), lambda i, c: (0, 0, c))
    down = pl.BlockSpec((1, fc, D_MODEL), lambda i, c: (0, c, 0))
    return pl.pallas_call(
        _ffn_body, out_shape=jax.ShapeDtypeStruct(x2d.shape, _F32), grid=(m // tm, _FF_GRID_CHUNKS),
        in_specs=[row, _resident(g.shape), up, up, down], out_specs=row,
        scratch_shapes=[pltpu.VMEM((tm, D_MODEL), _BF)],
        compiler_params=_cparams(2), name="ffn_dense",
    )(x2d, g, wg, wu, wd)


_MOE_PICK_PROB = 2.0 / N_EXPERTS
_MOE_CAP_STEPS = (1.0, 1.25, 1.5)


_BF16_ROWS = 16


def _moe_capacities(tm):
    mean = tm * _MOE_PICK_PROB
    return tuple(int(-(-(mean * s) // _BF16_ROWS) * _BF16_ROWS) for s in _MOE_CAP_STEPS)


def _moe_body(caps, final_norm, x_ref, g_ref, wr_ref, wg_ref, wu_ref, wd_ref, gf_ref, o_ref, hf_ref, comb_ref,
              rank_ref, rank_t_ref, cnt_ref):
    e = pl.program_id(1)
    tm = x_ref.shape[0]
    lane = lax.broadcasted_iota(jnp.int32, (tm, LANES), 1)

    @pl.when(e == 0)
    def _():
        x = x_ref[...]
        hf = _rms(x, g_ref[...])
        hf_ref[...] = hf.astype(_BF)
        o_ref[...] = x
        logits = jnp.dot(hf, wr_ref[...], precision=_HI, preferred_element_type=_F32)
        lg = jnp.where(lane < N_EXPERTS, logits, -jnp.inf)
        v1 = jnp.max(lg, axis=-1, keepdims=True)
        i1 = jnp.min(jnp.where(lg == v1, lane, LANES), axis=-1, keepdims=True)
        lg2 = jnp.where(lane == i1, -jnp.inf, lg)
        v2 = jnp.max(lg2, axis=-1, keepdims=True)
        i2 = jnp.min(jnp.where(lg2 == v2, lane, LANES), axis=-1, keepdims=True)
        e2 = jnp.exp(v2 - v1)
        comb_ref[...] = jnp.where(lane == i1, 1.0 / (1.0 + e2), 0.0) + jnp.where(lane == i2, e2 / (1.0 + e2), 0.0)
        picked = (lane == i1) | (lane == i2)
        picked_b = picked.astype(_BF)
        blk_r = lax.broadcasted_iota(jnp.int32, (LANES, LANES), 0)
        blk_c = lax.broadcasted_iota(jnp.int32, (LANES, LANES), 1)
        earlier = (blk_c < blk_r).astype(_BF)
        running = jnp.zeros((1, LANES), _F32)
        for b in range(tm // LANES):
            rows = slice(b * LANES, (b + 1) * LANES)
            pb = picked_b[rows, :]
            rank_b = jnp.where(picked[rows, :], _dot(earlier, pb) + running, -1.0)
            rank_ref[rows, :] = rank_b
            rank_t_ref[:, rows] = rank_b.T
            running = running + jnp.sum(pb.astype(_F32), axis=0, keepdims=True)
        cnt_ref[...] = running

    lane_row = lax.broadcasted_iota(jnp.int32, (1, LANES), 1)
    count = jnp.sum(jnp.where(lane_row == e, cnt_ref[...], 0.0)).astype(jnp.int32)

    def compact_path(cap):
        cap_pad = -(-cap // LANES) * LANES
        rank_row = rank_t_ref[pl.ds(e, 1), :]
        slot_col = lax.broadcasted_iota(jnp.int32, (cap, 1), 0).astype(_F32)
        pick = (rank_row == slot_col).astype(_BF)
        xs = _dot(pick, hf_ref[...]).astype(_BF)
        y = _swiglu_chunk(xs, wg_ref, wu_ref, wd_ref).astype(_BF)
        if cap_pad > cap:
            y = jnp.concatenate([y, jnp.zeros((cap_pad - cap, D_MODEL), _BF)], axis=0)
        rank_col = jnp.sum(jnp.where(lane == e, rank_ref[...], 0.0), axis=1, keepdims=True)
        comb_col = jnp.sum(jnp.where(lane == e, comb_ref[...], 0.0), axis=1, keepdims=True)
        slot_row = lax.broadcasted_iota(jnp.int32, (1, cap_pad), 1).astype(_F32)
        put = (rank_col == slot_row).astype(_BF)
        o_ref[...] += comb_col * _dot(put, y)

    lower = 0
    for cap in caps:
        pl.when((count > lower) & (count <= cap))(functools.partial(compact_path, cap))
        lower = cap

    @pl.when(count > caps[-1])
    def _():
        comb_col = jnp.sum(jnp.where(lane == e, comb_ref[...], 0.0), axis=1, keepdims=True)
        o_ref[...] += comb_col * _swiglu_chunk(hf_ref[...], wg_ref, wu_ref, wd_ref)

    if final_norm:
        @pl.when(e == pl.num_programs(1) - 1)
        def _():
            o_ref[...] = _rms(o_ref[...], gf_ref[...])


def _moe(x2d, g, wr, wg, wu, wd, gf, final_norm, tm):
    m = x2d.shape[0]
    n_e, _, d_ff = wg.shape
    assert tm % LANES == 0 and m % tm == 0, (m, tm)
    caps = _moe_capacities(tm)
    row_once = pl.BlockSpec((tm, D_MODEL), lambda i, e: (i, 0), pipeline_mode=pl.Buffered(1))
    up = pl.BlockSpec((1, D_MODEL, d_ff), lambda i, e: (e, 0, 0))
    down = pl.BlockSpec((1, d_ff, D_MODEL), lambda i, e: (e, 0, 0))
    return pl.pallas_call(
        functools.partial(_moe_body, caps, final_norm), out_shape=jax.ShapeDtypeStruct(x2d.shape, _F32),
        grid=(m // tm, n_e),
        in_specs=[row_once, _resident(g.shape), _resident(wr.shape), up, up, down, _resident(gf.shape)],
        out_specs=row_once,
        scratch_shapes=[pltpu.VMEM((tm, D_MODEL), _BF), pltpu.VMEM((tm, LANES), _F32), pltpu.VMEM((tm, LANES), _F32),
                        pltpu.VMEM((LANES, tm), _F32), pltpu.VMEM((1, LANES), _F32)],
        compiler_params=_cparams(2), name="moe",
    )(x2d, g, wr, wg, wu, wd, gf)


_SC_CONV = 3


def _mix_c_body(per_seq, seq, x_ref, g_ref, win_ref, cw_ref, e_ref, wout_ref, gm_ref, wq_ref, xo_ref, q_ref, p_ref,
                pp_ref):
    tm = x_ref.shape[0]
    x = x_ref[...]
    hn = _rms(x, g_ref[...]).astype(_BF)
    bg = _dot(hn, win_ref[:, 0:D_MODEL])
    p = _dot(hn, win_ref[:, D_MODEL:2 * D_MODEL]) * _dot(hn, win_ref[:, 2 * D_MODEL:3 * D_MODEL])
    if per_seq:
        @pl.when(pl.program_id(1) == 0)
        def _():
            pp_ref[0:SUBLANES, :] = jnp.zeros((SUBLANES, D_MODEL), _F32)
    else:
        pp_ref[0:SUBLANES, :] = jnp.zeros((SUBLANES, D_MODEL), _F32)
    pp_ref[SUBLANES:SUBLANES + tm, :] = p
    p1 = pp_ref[SUBLANES - 1:SUBLANES - 1 + tm, :]
    p2 = pp_ref[SUBLANES - 2:SUBLANES - 2 + tm, :]
    if per_seq:
        pp_ref[0:SUBLANES, :] = p[tm - SUBLANES:tm, :]
    else:
        t, _ = _pos_and_seq(tm, seq)
        p1 = jnp.where(t >= 1, p1, 0.0) + e_ref[0]
        p2 = jnp.where(t >= 2, p2, 0.0) + e_ref[1]
    conv = p2 * cw_ref[0:1, :] + p1 * cw_ref[1:2, :] + p * cw_ref[2:3, :]
    xn = x + _dot((bg * conv).astype(_BF), wout_ref[...])
    xo_ref[...] = xn
    q_ref[...] = _dot(_rms(xn, gm_ref[...]).astype(_BF), wq_ref[...]).astype(q_ref.dtype)
    if per_seq:
        @pl.when(pl.program_id(1) == pl.num_programs(1) - 1)
        def _():
            p_ref[0] = pp_ref[SUBLANES + tm - (_SC_CONV - 1):SUBLANES + tm, :]
    else:
        p_ref[...] = p


def _mix_c(x2d, g, win, cw, e_buf, wout, gm, wq, q_dtype, per_seq, bsz, seq, tm):
    m = x2d.shape[0]
    if per_seq:
        nt = seq // tm
        grid = (bsz, nt)
        imap = lambda b, j: (b * nt + j, 0)
        e_spec = pl.BlockSpec((_SC_CONV - 1, SUBLANES, D_MODEL), lambda b, j: (0, 0, 0))
        p_shape = jax.ShapeDtypeStruct((bsz, _SC_CONV - 1, D_MODEL), _F32)
        p_spec = pl.BlockSpec((1, _SC_CONV - 1, D_MODEL), lambda b, j: (b, 0, 0))
    else:
        grid = (m // tm, 1)
        imap = lambda i, j: (i, 0)
        e_spec = pl.BlockSpec((_SC_CONV - 1, tm, D_MODEL), lambda i, j: (0, i, 0))
        p_shape = jax.ShapeDtypeStruct((m, D_MODEL), _F32)
        p_spec = pl.BlockSpec((tm, D_MODEL), imap)
    row = pl.BlockSpec((tm, D_MODEL), imap)
    outs = (jax.ShapeDtypeStruct((m, D_MODEL), _F32), jax.ShapeDtypeStruct((m, D_MODEL), q_dtype), p_shape)
    return pl.pallas_call(
        functools.partial(_mix_c_body, per_seq, seq), out_shape=outs, grid=grid,
        in_specs=[row, _resident(g.shape), _resident(win.shape), _resident(cw.shape), e_spec, _resident(wout.shape),
                  _resident(gm.shape), _resident(wq.shape)],
        out_specs=(row, row, p_spec),
        scratch_shapes=[pltpu.VMEM((SUBLANES + tm, D_MODEL), _F32)],
        compiler_params=_cparams(2), name="mix_c",
    )(x2d, g, win, cw, e_buf, wout, gm, wq)


def _expand_buf(buf, seq, n_back):
    bsz, km1, ch = buf.shape
    outs = []
    for j in range(1, n_back + 1):
        rows = [buf[:, km1 - j + t] if t < j else jnp.zeros((bsz, ch), buf.dtype) for t in range(seq)]
        outs.append(jnp.stack(rows, axis=1).reshape(bsz * seq, ch))
    return jnp.stack(outs)


def _row(vec, width=None):
    vec = vec.reshape(1, -1).astype(_F32)
    if width is not None and vec.shape[1] < width:
        vec = jnp.pad(vec, ((0, 0), (0, width - vec.shape[1])))
    return vec


def _trunk(x, k_all, v_all, prompt, states, w):
    bsz, seq, _ = x.shape
    m = bsz * seq
    x2d = x.reshape(m, D_MODEL)
    tm = min(512, m)
    dskip = _row(jnp.repeat(w["d_skip"][0], SSM_HEAD_DIM))
    common = (w["conv_w_ssm"][0], _row(w["conv_b_ssm"][0]), _row(w["dt_bias"][0], LANES), _row(w["a_log"][0], LANES),
              dskip, _row(w["g_ssm_norm"][0]))

    act_dtype = _BF if prompt else _F32
    u, v, z, xbc, dt = _inproj_ab(x2d, _row(w["g_mix"][0]), w["w_in_main"], w["w_in_dt"], act_dtype, tm)
    if prompt:
        y_ab, ssm_state, conv_state = _mix_ab_prompt(
            u, v, z, xbc, dt, bsz, seq, w["w_spatial"][0], w["b_spatial"][0][:, :, None], *common)
        v_out = None
    else:
        ws = w["w_spatial"][0][:, :seq, :seq]
        coef = []
        for d in range(seq):
            per_t = [ws[:, t, t - d] if t >= d else jnp.zeros((A_GROUPS,), _F32) for t in range(seq)]
            coef.append(jnp.tile(jnp.repeat(jnp.stack(per_t), A_HEAD, axis=1), (_SEQ_BLK, 1)))
        sb = jnp.tile(jnp.repeat(w["b_spatial"][0][:, :seq].T, A_HEAD, axis=1), (_SEQ_BLK, 1))
        e_buf = _expand_buf(states["ssm_conv"], seq, SSM_CONV - 1)
        y_ab, ssm_state = _mix_ab_sample(u, v, z, xbc, dt, e_buf, states["ssm"].reshape(bsz, SSM_DIM, SSM_STATE), seq,
                                         jnp.stack(coef), sb, *common)
        conv_state = xbc.reshape(bsz, seq, SSM_CONV_DIM)[:, seq - (SSM_CONV - 1):]
        v_out = v.reshape(bsz, seq, D_MODEL)
    x2d, q = _out_q(y_ab, x2d, w["w_out_ab"], _row(w["g_mem"][0]), w["w_mem_q"][0], act_dtype, tm)

    def attend(q, x2d, layer):
        if prompt:
            return _attn_prompt(q, k_all, v_all, x2d, w["w_mem_o"][layer], layer, bsz, seq, tm)
        return _attn_sample(q, k_all, v_all, x2d, w["w_mem_o"][layer], layer, bsz, seq)

    x2d = attend(q, x2d, 0)
    x2d = _ffn(x2d, _row(w["g_ffn"][0]), w["w_ffn_gate"], w["w_ffn_up"], w["w_ffn_down"], min(_MOE_TILE, m))

    if prompt:
        e_buf = jnp.zeros((_SC_CONV - 1, SUBLANES, D_MODEL), _F32)
        x2d, q, sconv_state = _mix_c(x2d, _row(w["g_mix"][1]), w["w_in_c"], w["conv_w_c"][0], e_buf, w["w_out_c"],
                                     _row(w["g_mem"][1]), w["w_mem_q"][1], act_dtype, True, bsz, seq, tm)
    else:
        e_buf = _expand_buf(states["sconv"], seq, _SC_CONV - 1)
        x2d, q, p = _mix_c(x2d, _row(w["g_mix"][1]), w["w_in_c"], w["conv_w_c"][0], e_buf, w["w_out_c"],
                           _row(w["g_mem"][1]), w["w_mem_q"][1], act_dtype, False, bsz, seq, tm)
        sconv_state = p.reshape(bsz, seq, D_MODEL)[:, seq - (_SC_CONV - 1):]
    x2d = attend(q, x2d, 1)
    x2d = _moe(x2d, _row(w["g_ffn"][1]), w["w_router"], w["w_exp_gate"], w["w_exp_up"], w["w_exp_down"],
               _row(w["g_final"]), True, min(_MOE_TILE, m))
    y = x2d.reshape(bsz, seq, D_MODEL)
    ssm_state = ssm_state.reshape(1, bsz, SSM_HEADS, SSM_HEAD_DIM, SSM_STATE)
    return y, ssm_state, conv_state[None], sconv_state[None], v_out


def kernel(x_prompt, x_sample, mem_prompt, state_ssm, state_ssm_conv, state_sconv, cache_mem_k, cache_mem_v, g_mix, g_mem, g_ffn, g_final, w_in_ab, w_spatial, b_spatial, conv_w_ssm, conv_b_ssm, dt_bias, a_log, d_skip, g_ssm_norm, w_out_ab, w_ffn_gate, w_ffn_up, w_ffn_down, w_in_c, conv_w_c, w_out_c, w_router, w_exp_gate, w_exp_up, w_exp_down, w_mem_q, w_mem_k, w_mem_v, w_mem_o):
    depth = w_mem_q.shape[0]
    bp = x_prompt.shape[0]
    bs = x_sample.shape[0]
    n_main = 3 * D_MODEL + SSM_CONV_DIM
    bf = lambda a: a.astype(_BF)
    w = dict(
        g_mix=g_mix, g_mem=g_mem, g_ffn=g_ffn, g_final=g_final, w_spatial=w_spatial, b_spatial=b_spatial,
        conv_w_ssm=conv_w_ssm, conv_b_ssm=conv_b_ssm, dt_bias=dt_bias, a_log=a_log, d_skip=d_skip, g_ssm_norm=g_ssm_norm,
        conv_w_c=conv_w_c,
        w_in_main=bf(w_in_ab[0][:, :n_main]),
        w_in_dt=bf(jnp.pad(w_in_ab[0][:, n_main:], ((0, 0), (0, LANES - SSM_HEADS)))),
        w_out_ab=bf(w_out_ab[0]), w_ffn_gate=bf(w_ffn_gate), w_ffn_up=bf(w_ffn_up), w_ffn_down=bf(w_ffn_down),
        w_in_c=bf(w_in_c[0]), w_out_c=bf(w_out_c[0]),
        w_router=jnp.pad(w_router[0], ((0, 0), (0, LANES - N_EXPERTS))),
        w_exp_gate=bf(w_exp_gate[0]), w_exp_up=bf(w_exp_up[0]), w_exp_down=bf(w_exp_down[0]),
        w_mem_q=bf(w_mem_q), w_mem_o=bf(w_mem_o),
    )
    mem_k_p, mem_v_p, k_bf, v_bf = _mem_kv(mem_prompt, bf(w_mem_k), bf(w_mem_v))
    y_p, ssm_p, ssmconv_p, sconv_p, _ = _trunk(x_prompt, k_bf, v_bf, True, None, w)
    states = dict(ssm=state_ssm[0], ssm_conv=state_ssm_conv[0], sconv=state_sconv[0])
    y_s, ssm_s, ssmconv_s, sconv_s, v_s = _trunk(
        x_sample, cache_mem_k.reshape(depth * bs, N_MEM, MEM_HEADS, MEM_HEAD_DIM),
        cache_mem_v.reshape(depth * bs, N_MEM, MEM_HEADS, MEM_HEAD_DIM), False, states, w)
    return (y_p, y_s, ssm_p, ssm_s, ssmconv_p, ssmconv_s, sconv_p, sconv_s, mem_k_p, mem_v_p, v_s[None])
```

```python
import functools

import jax
import jax.numpy as jnp
from jax import lax
from jax.experimental import pallas as pl
from jax.experimental.pallas import tpu as pltpu

D_MODEL = 1024
EPS = 1e-6
CHUNK = 128
A_GROUPS = 8
A_HEAD = D_MODEL // A_GROUPS
SSM_HEADS = 16
SSM_HEAD_DIM = 64
SSM_GROUPS = 2
SSM_STATE = 128
SSM_CONV = 4
SSM_DIM = D_MODEL
SSM_CONV_DIM = SSM_DIM + 2 * SSM_GROUPS * SSM_STATE
N_MEM = 256
MEM_HEADS = 4
MEM_HEAD_DIM = D_MODEL // MEM_HEADS
N_EXPERTS = 8

LANES = 128
SUBLANES = 8
VMEM_LIMIT = 56 * 1024 * 1024
_TOKEN_TILE = 1024

_BF = jnp.bfloat16
_F32 = jnp.float32
_HI = lax.Precision.HIGHEST
_NT = (((1,), (1,)), ((), ()))


def _cparams(n_axes):
    return pltpu.CompilerParams(dimension_semantics=("arbitrary",) * n_axes, vmem_limit_bytes=VMEM_LIMIT)


def _resident(shape):
    nd = len(shape)
    return pl.BlockSpec(shape, lambda *_: (0,) * nd, pipeline_mode=pl.Buffered(1))


def _rms(x, g):
    ms = jnp.mean(x * x, axis=-1, keepdims=True)
    return x * lax.rsqrt(ms + EPS) * g


def _silu(x):
    return x * jax.nn.sigmoid(x)


def _dot(a, b):
    return jnp.dot(a, b, preferred_element_type=_F32)


def _softplus(x):
    return jnp.maximum(x, 0.0) + jnp.log1p(jnp.exp(-jnp.abs(x)))


def _pos_and_seq(rows, seq):
    r = lax.broadcasted_iota(jnp.int32, (rows, 1), 0)
    if seq & (seq - 1) == 0:
        return r & (seq - 1), r >> (seq.bit_length() - 1)
    return lax.rem(r, seq), lax.div(r, seq)


def _kv_body(m_ref, wk_ref, wv_ref, k_ref, v_ref, kb_ref, vb_ref):
    m = m_ref[0].astype(_BF)
    for w_ref, o_ref, ob_ref in ((wk_ref, k_ref, kb_ref), (wv_ref, v_ref, vb_ref)):
        res = _dot(m, w_ref[0])
        ob_ref[0] = res.astype(_BF)
        for h in range(MEM_HEADS):
            o_ref[0, 0, :, h, :] = res[:, h * MEM_HEAD_DIM:(h + 1) * MEM_HEAD_DIM]


def _mem_kv(mem, wk, wv):
    bsz, n, d = mem.shape
    depth = wk.shape[0]
    out = jax.ShapeDtypeStruct((depth, bsz, n, MEM_HEADS, MEM_HEAD_DIM), _F32)
    out_b = jax.ShapeDtypeStruct((depth * bsz, n, d), _BF)
    wspec = pl.BlockSpec((1, d, d), lambda l, b: (l, 0, 0))
    ospec = pl.BlockSpec((1, 1, n, MEM_HEADS, MEM_HEAD_DIM), lambda l, b: (l, b, 0, 0, 0))
    bspec = pl.BlockSpec((1, n, d), lambda l, b: (l * bsz + b, 0, 0))
    return pl.pallas_call(
        _kv_body, out_shape=(out, out, out_b, out_b), grid=(depth, bsz),
        in_specs=[pl.BlockSpec((1, n, d), lambda l, b: (b, 0, 0)), wspec, wspec],
        out_specs=(ospec, ospec, bspec, bspec), compiler_params=_cparams(2), name="mem_kv",
    )(mem, wk, wv)


_PROJ_CHUNK = 512


def _inproj_ab_body(x_ref, g_ref, w_ref, u_ref, v_ref, z_ref, xbc_ref, dt_ref):
    hn = _rms(x_ref[...], g_ref[...]).astype(_BF)
    col = 0
    for ref, width, act in ((u_ref, D_MODEL, True), (v_ref, D_MODEL, True), (z_ref, D_MODEL, False),
                            (xbc_ref, SSM_CONV_DIM, False)):
        for c in range(width // _PROJ_CHUNK):
            r = _dot(hn, w_ref[:, col:col + _PROJ_CHUNK])
            if act:
                r = jax.nn.gelu(r)
            ref[:, c * _PROJ_CHUNK:(c + 1) * _PROJ_CHUNK] = r.astype(ref.dtype)
            col += _PROJ_CHUNK
    dt_ref[...] = _dot(hn, w_ref[:, col:col + LANES])


def _inproj_ab(x2d, g, w_in, act_dtype, tm):
    m = x2d.shape[0]
    row = lambda width: pl.BlockSpec((tm, width), lambda i: (i, 0))
    outs = (jax.ShapeDtypeStruct((m, D_MODEL), act_dtype),) * 3 + (
        jax.ShapeDtypeStruct((m, SSM_CONV_DIM), _F32), jax.ShapeDtypeStruct((m, LANES), _F32))
    return pl.pallas_call(
        _inproj_ab_body, out_shape=outs, grid=(m // tm,),
        in_specs=[row(D_MODEL), _resident(g.shape), _resident(w_in.shape)],
        out_specs=(row(D_MODEL), row(D_MODEL), row(D_MODEL), row(SSM_CONV_DIM), row(LANES)),
        compiler_params=_cparams(1), name="inproj_ab",
    )(x2d, g, w_in)


_PAIR = 2 * SSM_HEAD_DIM
_HPG = SSM_HEADS // SSM_GROUPS
_GRP_DIM = SSM_DIM // SSM_GROUPS


def _mix_ab_prompt_body(u_ref, v_ref, z_ref, xbc_ref, dt_ref, ws_ref, bs_ref, cw_ref, cb_ref, dtb_ref, alog_ref,
                        dskip_ref, gn_ref, y_ref, st_ref, cst_ref, xp_ref, sT_ref):
    q = CHUNK
    c = pl.program_id(1)

    @pl.when(c == 0)
    def _():
        xp_ref[0:SUBLANES, :] = jnp.zeros((SUBLANES, SSM_CONV_DIM), _F32)
        sT_ref[...] = jnp.zeros_like(sT_ref)

    tri = lax.broadcasted_iota(jnp.int32, (q, q), 0) >= lax.broadcasted_iota(jnp.int32, (q, q), 1)

    for g in range(A_GROUPS):
        sl = slice(g * A_HEAD, (g + 1) * A_HEAD)
        w = jnp.where(tri, ws_ref[g], 0.0).astype(_BF)
        s = _dot(w, v_ref[:, sl].astype(_BF)) + bs_ref[g]
        y_ref[:, sl] = (u_ref[:, sl].astype(_F32) * s).astype(y_ref.dtype)

    xbc = xbc_ref[...]
    xp_ref[SUBLANES:SUBLANES + q, :] = xbc
    base = SUBLANES - (SSM_CONV - 1)
    conv = xp_ref[base:base + q, :] * cw_ref[0:1, :]
    for k in range(1, SSM_CONV):
        conv = conv + xp_ref[base + k:base + k + q, :] * cw_ref[k:k + 1, :]
    xp_ref[0:SUBLANES, :] = xbc[q - SUBLANES:q, :]
    act = _silu(conv + cb_ref[...])
    xs = act[:, :SSM_DIM]
    bm = act[:, SSM_DIM:SSM_DIM + SSM_GROUPS * SSM_STATE]
    cm = act[:, SSM_DIM + SSM_GROUPS * SSM_STATE:]

    dt = _softplus(dt_ref[...] + dtb_ref[...])
    d_a = dt * (-jnp.exp(alog_ref[...]))
    la = jnp.dot(tri.astype(_F32), d_a, precision=_HI, preferred_element_type=_F32)
    la_t = la.T
    dt_t = dt.T
    e_la = jnp.exp(la)
    last = la[q - 1:q, :]
    wend = jnp.exp(last - la) * dt
    e_last = jnp.exp(last)
    lo = lax.broadcasted_iota(jnp.int32, (q, _PAIR), 1) < SSM_HEAD_DIM
    lo_row = lo[0:1, :]

    for g in range(SSM_GROUPS):
        bm_t = bm[:, g * SSM_STATE:(g + 1) * SSM_STATE].T.astype(_BF)
        cm_g = cm[:, g * SSM_STATE:(g + 1) * SSM_STATE].astype(_BF)
        cb = _dot(cm_g, bm_t)
        ys = []
        for pp in range(_HPG // 2):
            p = g * (_HPG // 2) + pp
            cols = slice(p * _PAIR, (p + 1) * _PAIR)
            wts = []
            for h in (2 * p, 2 * p + 1):
                seg = la[:, h:h + 1] - la_t[h:h + 1, :]
                wts.append(cb * jnp.exp(jnp.where(tri, seg, -jnp.inf)) * dt_t[h:h + 1, :])
            w2 = jnp.concatenate(wts, axis=1).astype(_BF)
            x_pair = xs[:, cols]
            x2 = jnp.concatenate([jnp.where(lo, x_pair, 0.0), jnp.where(lo, 0.0, x_pair)], axis=0).astype(_BF)
            s_t = sT_ref[:, cols]
            e_pair = jnp.where(lo, e_la[:, 2 * p:2 * p + 1], e_la[:, 2 * p + 1:2 * p + 2])
            y_pair = _dot(w2, x2) + _dot(cm_g, s_t.astype(_BF)) * e_pair
            wd_pair = jnp.where(lo, wend[:, 2 * p:2 * p + 1], wend[:, 2 * p + 1:2 * p + 2])
            el_pair = jnp.where(lo_row, e_last[:, 2 * p:2 * p + 1], e_last[:, 2 * p + 1:2 * p + 2])
            sT_ref[:, cols] = s_t * el_pair + _dot(bm_t, (x_pair * wd_pair).astype(_BF))
            ys.append(y_pair + x_pair * dskip_ref[:, cols])
        gcols = slice(g * _GRP_DIM, (g + 1) * _GRP_DIM)
        gated = jnp.concatenate(ys, axis=1) * _silu(z_ref[:, gcols].astype(_F32))
        y_ref[:, D_MODEL + g * _GRP_DIM:D_MODEL + (g + 1) * _GRP_DIM] = _rms(gated, gn_ref[:, gcols]).astype(y_ref.dtype)

    @pl.when(c == pl.num_programs(1) - 1)
    def _():
        for p in range(SSM_HEADS // 2):
            st_ref[0, p * _PAIR:(p + 1) * _PAIR, :] = sT_ref[:, p * _PAIR:(p + 1) * _PAIR].T
        cst_ref[0] = xp_ref[SUBLANES + q - (SSM_CONV - 1):SUBLANES + q, :]


def _mix_ab_prompt(u, v, z, xbc, dt, bsz, seq, ws, bs, cw, cb, dtb, alog, dskip, gn):
    nc = seq // CHUNK
    row = lambda width: pl.BlockSpec((CHUNK, width), lambda b, c: (b * nc + c, 0))
    small = [ws, bs, cw, cb, dtb, alog, dskip, gn]
    outs = (jax.ShapeDtypeStruct((bsz * seq, 2 * D_MODEL), _BF),
            jax.ShapeDtypeStruct((bsz, SSM_DIM, SSM_STATE), _F32),
            jax.ShapeDtypeStruct((bsz, SSM_CONV - 1, SSM_CONV_DIM), _F32))
    return pl.pallas_call(
        _mix_ab_prompt_body, out_shape=outs, grid=(bsz, nc),
        in_specs=[row(D_MODEL), row(D_MODEL), row(D_MODEL), row(SSM_CONV_DIM), row(LANES)]
        + [_resident(a.shape) for a in small],
        out_specs=(row(2 * D_MODEL),
                   pl.BlockSpec((1, SSM_DIM, SSM_STATE), lambda b, c: (b, 0, 0)),
                   pl.BlockSpec((1, SSM_CONV - 1, SSM_CONV_DIM), lambda b, c: (b, 0, 0))),
        scratch_shapes=[pltpu.VMEM((SUBLANES + CHUNK, SSM_CONV_DIM), _F32), pltpu.VMEM((SSM_STATE, SSM_DIM), _F32)],
        compiler_params=_cparams(2), name="mix_ab_prompt",
    )(u, v, z, xbc, dt, *small)


_SEQ_BLK = 8


def _mix_ab_sample_body(seq, u_ref, v_ref, z_ref, xbc_ref, dt_ref, e_ref, st_ref, coef_ref, sb_ref, cw_ref, cb_ref,
                        dtb_ref, alog_ref, dskip_ref, gn_ref, y_ref, sto_ref, sh_ref):
    rows = _SEQ_BLK * seq
    t, seq_of_row = _pos_and_seq(rows, seq)
    pad_rows = sh_ref.shape[0]
    sh_ref[...] = jnp.zeros((pad_rows, SSM_CONV_DIM), _F32)

    def shifts(val, deltas):
        width = val.shape[1]
        sh_ref[SUBLANES:SUBLANES + rows, 0:width] = val
        return [sh_ref[SUBLANES - d:SUBLANES - d + rows, 0:width] for d in deltas]

    back = tuple(range(1, seq))

    v = v_ref[...]
    s = coef_ref[0] * v + sb_ref[...]
    for d, vs in zip(back, shifts(v, back)):
        s = s + coef_ref[d] * jnp.where(t >= d, vs, 0.0)
    y_ref[:, 0:D_MODEL] = u_ref[...] * s

    xbc = xbc_ref[...]
    conv = xbc * cw_ref[SSM_CONV - 1:SSM_CONV, :]
    for j, xsft in zip(range(1, SSM_CONV), shifts(xbc, tuple(range(1, SSM_CONV)))):
        conv = conv + (jnp.where(t >= j, xsft, 0.0) + e_ref[j - 1]) * cw_ref[SSM_CONV - 1 - j:SSM_CONV - j, :]
    act = _silu(conv + cb_ref[...])
    xs = act[:, :SSM_DIM]
    bm = act[:, SSM_DIM:SSM_DIM + SSM_GROUPS * SSM_STATE]
    cm = act[:, SSM_DIM + SSM_GROUPS * SSM_STATE:]

    dt = _softplus(dt_ref[...] + dtb_ref[...])
    d_a = dt * (-jnp.exp(alog_ref[...]))
    la = d_a
    for d, sft in zip(back, shifts(d_a, back)):
        la = la + jnp.where(t >= d, sft, 0.0)
    la_back = shifts(la, back)
    dt_back = shifts(dt, back)
    last = jnp.where(t == seq - 1, la, 0.0)
    fwd = shifts(la, tuple(-d for d in back))
    for d, sft in zip(back, fwd):
        last = last + jnp.where(t == seq - 1 - d, sft, 0.0)
    wend = jnp.exp(last - la) * dt
    e_last = jnp.exp(last)
    e_la = jnp.exp(la)

    head_lane = lax.broadcasted_iota(jnp.int32, (rows, LANES), 1)
    bm_back = shifts(bm, back)

    def cb_heads(b_other):
        per_g = [jnp.sum(cm[:, g * SSM_STATE:(g + 1) * SSM_STATE] * b_other[:, g * SSM_STATE:(g + 1) * SSM_STATE],
                         axis=-1, keepdims=True) for g in range(SSM_GROUPS)]
        return jnp.where(head_lane < _HPG, per_g[0], per_g[1])

    gs = [cb_heads(bm) * dt]
    for i, d in enumerate(back):
        gd = cb_heads(bm_back[i]) * jnp.exp(la - la_back[i]) * dt_back[i]
        gs.append(jnp.where(t >= d, gd, 0.0))

    head0 = lax.broadcasted_iota(jnp.int32, (LANES, SSM_DIM), 0) * SSM_HEAD_DIM
    chan = lax.broadcasted_iota(jnp.int32, (LANES, SSM_DIM), 1)
    expand = ((chan >= head0) & (chan < head0 + SSM_HEAD_DIM)).astype(_F32)
    stacked = jnp.concatenate([e_la, wend, e_last] + gs, axis=0)
    full = jnp.dot(stacked, expand, precision=_HI, preferred_element_type=_F32)
    e_la_f, wend_f, e_last_f = full[0:rows], full[rows:2 * rows], full[2 * rows:3 * rows]
    y2 = full[3 * rows:4 * rows] * xs
    for i, (d, xsft) in enumerate(zip(back, shifts(xs, back))):
        y2 = y2 + full[(4 + i) * rows:(5 + i) * rows] * jnp.where(t >= d, xsft, 0.0)

    zpad = jnp.zeros((LANES - rows, SSM_DIM), _F32)
    xw_t = jnp.concatenate([xs * wend_f, zpad], axis=0).T.astype(_BF)
    el_t = jnp.concatenate([e_last_f, zpad], axis=0).T
    row_lane = lax.broadcasted_iota(jnp.int32, (_GRP_DIM, LANES), 1)
    bpad = jnp.zeros((LANES - rows, SSM_STATE), _F32)

    y1 = [jnp.zeros((rows, _GRP_DIM), _F32) for _ in range(SSM_GROUPS)]
    for i in range(_SEQ_BLK):
        mine = seq_of_row == i
        for g in range(SSM_GROUPS):
            gr = slice(g * _GRP_DIM, (g + 1) * _GRP_DIM)
            st = st_ref[i, gr, :]
            b_g = jnp.where(mine, bm[:, g * SSM_STATE:(g + 1) * SSM_STATE], 0.0)
            b_g = jnp.concatenate([b_g, bpad], axis=0).astype(_BF)
            decay = jnp.sum(jnp.where(row_lane == i * seq, el_t[gr, :], 0.0), axis=1, keepdims=True)
            sto_ref[i, gr, :] = st * decay + _dot(xw_t[gr, :], b_g)
            c_g = cm[:, g * SSM_STATE:(g + 1) * SSM_STATE].astype(_BF)
            ch = lax.dot_general(c_g, st.astype(_BF), _NT, preferred_element_type=_F32)
            y1[g] = y1[g] + jnp.where(mine, ch, 0.0)

    for g in range(SSM_GROUPS):
        gr = slice(g * _GRP_DIM, (g + 1) * _GRP_DIM)
        ys = y1[g] * e_la_f[:, gr] + y2[:, gr] + xs[:, gr] * dskip_ref[:, gr]
        gated = ys * _silu(z_ref[:, gr])
        y_ref[:, D_MODEL + g * _GRP_DIM:D_MODEL + (g + 1) * _GRP_DIM] = _rms(gated, gn_ref[:, gr])


def _mix_ab_sample(u, v, z, xbc, dt, e_buf, state, seq, coef, sb, cw, cb, dtb, alog, dskip, gn):
    bsz = state.shape[0]
    rows = _SEQ_BLK * seq
    row = lambda width: pl.BlockSpec((rows, width), lambda i: (i, 0))
    small = [coef, sb, cw, cb, dtb, alog, dskip, gn]
    st_spec = pl.BlockSpec((_SEQ_BLK, SSM_DIM, SSM_STATE), lambda i: (i, 0, 0))
    outs = (jax.ShapeDtypeStruct((bsz * seq, 2 * D_MODEL), _F32), jax.ShapeDtypeStruct(state.shape, _F32))
    return pl.pallas_call(
        functools.partial(_mix_ab_sample_body, seq), out_shape=outs, grid=(bsz // _SEQ_BLK,),
        in_specs=[row(D_MODEL), row(D_MODEL), row(D_MODEL), row(SSM_CONV_DIM), row(LANES),
                  pl.BlockSpec((SSM_CONV - 1, rows, SSM_CONV_DIM), lambda i: (0, i, 0)), st_spec]
        + [_resident(a.shape) for a in small],
        out_specs=(row(2 * D_MODEL), st_spec),
        scratch_shapes=[pltpu.VMEM((rows + 2 * SUBLANES, SSM_CONV_DIM), _F32)],
        compiler_params=_cparams(1), name="mix_ab_sample",
    )(u, v, z, xbc, dt, e_buf, state, *small)


def _out_q_body(a_ref, x_ref, wo_ref, gm_ref, wq_ref, xo_ref, q_ref):
    xn = x_ref[...] + _dot(a_ref[...].astype(_BF), wo_ref[...])
    xo_ref[...] = xn
    q_ref[...] = _dot(_rms(xn, gm_ref[...]).astype(_BF), wq_ref[...]).astype(q_ref.dtype)


def _out_q(a, x2d, wo, gm, wq, q_dtype, tm):
    m = x2d.shape[0]
    row = lambda width: pl.BlockSpec((tm, width), lambda i: (i, 0))
    outs = (jax.ShapeDtypeStruct((m, D_MODEL), _F32), jax.ShapeDtypeStruct((m, D_MODEL), q_dtype))
    return pl.pallas_call(
        _out_q_body, out_shape=outs, grid=(m // tm,),
        in_specs=[row(a.shape[1]), row(D_MODEL), _resident(wo.shape), _resident(gm.shape), _resident(wq.shape)],
        out_specs=(row(D_MODEL), row(D_MODEL)), compiler_params=_cparams(1), name="out_q",
    )(a, x2d, wo, gm, wq)


_ATT_SCALE = MEM_HEAD_DIM ** -0.5


def _attend(q, k, v):
    outs = []
    for h in range(MEM_HEADS):
        sl = slice(h * MEM_HEAD_DIM, (h + 1) * MEM_HEAD_DIM)
        s = lax.dot_general(q[:, sl], k[:, sl], _NT, preferred_element_type=_F32) * _ATT_SCALE
        p = jnp.exp(s - jnp.max(s, axis=-1, keepdims=True))
        p = p / jnp.sum(p, axis=-1, keepdims=True)
        outs.append(_dot(p.astype(_BF), v[:, sl]))
    return jnp.concatenate(outs, axis=1)


def _attn_prompt_body(q_ref, k_ref, v_ref, x_ref, wo_ref, o_ref):
    o = _attend(q_ref[...], k_ref[0], v_ref[0])
    o_ref[...] = x_ref[...] + _dot(o.astype(_BF), wo_ref[...])


def _attn_prompt(q, k, v, x2d, wo, layer, bsz, seq, tq):
    nt = seq // tq
    row = pl.BlockSpec((tq, D_MODEL), lambda b, j: (b * nt + j, 0))
    kv = pl.BlockSpec((1, N_MEM, D_MODEL), lambda b, j: (layer * bsz + b, 0, 0))
    return pl.pallas_call(
        _attn_prompt_body, out_shape=jax.ShapeDtypeStruct(x2d.shape, _F32), grid=(bsz, nt),
        in_specs=[row, kv, kv, row, _resident(wo.shape)], out_specs=row,
        compiler_params=_cparams(2), name="attn_prompt",
    )(q, k, v, x2d, wo)


def _attn_sample_body(seq, q_ref, k_ref, v_ref, x_ref, wo_ref, o_ref, o_scr):
    per_tile = SUBLANES // seq
    _, seq_of_row = _pos_and_seq(SUBLANES, seq)
    n_lt = MEM_HEAD_DIM // LANES
    rows_mh = N_MEM * MEM_HEADS
    col = lax.broadcasted_iota(jnp.int32, (MEM_HEADS * SUBLANES, rows_mh), 1)
    row = lax.broadcasted_iota(jnp.int32, (MEM_HEADS * SUBLANES, rows_mh), 0)
    same_head = (col & (MEM_HEADS - 1)) == (row >> (SUBLANES.bit_length() - 1))
    for tile in range(_SEQ_BLK // per_tile):
        q8 = q_ref[tile * SUBLANES:(tile + 1) * SUBLANES, :].astype(_BF)
        o = None
        for j in range(per_tile):
            i = tile * per_tile + j
            s = None
            for lt in range(n_lt):
                qh = jnp.concatenate([q8[:, h * MEM_HEAD_DIM + lt * LANES:h * MEM_HEAD_DIM + (lt + 1) * LANES]
                                      for h in range(MEM_HEADS)], axis=0)
                k_lt = k_ref[i, :, :, lt * LANES:(lt + 1) * LANES].reshape(rows_mh, LANES).astype(_BF)
                part = lax.dot_general(qh, k_lt, _NT, preferred_element_type=_F32)
                s = part if s is None else s + part
            s = jnp.where(same_head, s * _ATT_SCALE, -jnp.inf)
            p = jnp.exp(s - jnp.max(s, axis=-1, keepdims=True))
            p = (p / jnp.sum(p, axis=-1, keepdims=True)).astype(_BF)
            pieces = [None] * (MEM_HEADS * n_lt)
            for lt in range(n_lt):
                v_lt = v_ref[i, :, :, lt * LANES:(lt + 1) * LANES].reshape(rows_mh, LANES).astype(_BF)
                o_lt = _dot(p, v_lt)
                for h in range(MEM_HEADS):
                    pieces[h * n_lt + lt] = o_lt[h * SUBLANES:(h + 1) * SUBLANES, :]
            oi = jnp.concatenate(pieces, axis=1)
            o = oi if o is None else jnp.where(seq_of_row == j, oi, o)
        o_scr[tile * SUBLANES:(tile + 1) * SUBLANES, :] = o
    o_ref[...] = x_ref[...] + _dot(o_scr[...].astype(_BF), wo_ref[...])


def _attn_sample(q, k, v, x2d, wo, layer, bsz, seq):
    rows = _SEQ_BLK * seq
    nb = bsz // _SEQ_BLK
    row = pl.BlockSpec((rows, D_MODEL), lambda i: (i, 0))
    kv = pl.BlockSpec((_SEQ_BLK, N_MEM, MEM_HEADS, MEM_HEAD_DIM), lambda i: (layer * nb + i, 0, 0, 0))
    return pl.pallas_call(
        functools.partial(_attn_sample_body, seq), out_shape=jax.ShapeDtypeStruct(x2d.shape, _F32), grid=(nb,),
        in_specs=[row, kv, kv, row, _resident(wo.shape)], out_specs=row,
        scratch_shapes=[pltpu.VMEM((rows, D_MODEL), _F32)],
        compiler_params=_cparams(1), name="attn_sample",
    )(q, k, v, x2d, wo)


_FF_GRID_CHUNKS = 2
_FF_SUB = 512
_MOE_TILE = 1024


def _swiglu_chunk(h, wg_ref, wu_ref, wd_ref):
    width = wg_ref.shape[2]
    out = None
    for lo in range(0, width, _FF_SUB):
        hi = min(lo + _FF_SUB, width)
        hid = _silu(_dot(h, wg_ref[0, :, lo:hi])) * _dot(h, wu_ref[0, :, lo:hi])
        part = _dot(hid.astype(_BF), wd_ref[0, lo:hi, :])
        out = part if out is None else out + part
    return out


def _ffn_body(x_ref, g_ref, wg_ref, wu_ref, wd_ref, o_ref, hf_ref):
    c = pl.program_id(1)

    @pl.when(c == 0)
    def _():
        x = x_ref[...]
        hf_ref[...] = _rms(x, g_ref[...]).astype(_BF)
        o_ref[...] = x

    o_ref[...] += _swiglu_chunk(hf_ref[...], wg_ref, wu_ref, wd_ref)


def _ffn(x2d, g, wg, wu, wd, tm):
    m = x2d.shape[0]
    d_ff = wg.shape[2]
    fc = d_ff // _FF_GRID_CHUNKS
    row = pl.BlockSpec((tm, D_MODEL), lambda i, c: (i, 0))
    up = pl.BlockSpec((1, D_MODEL, fc), lambda i, c: (0, 0, c))
    down = pl.BlockSpec((1, fc, D_MODEL), lambda i, c: (0, c, 0))
    return pl.pallas_call(
        _ffn_body, out_shape=jax.ShapeDtypeStruct(x2d.shape, _F32), grid=(m // tm, _FF_GRID_CHUNKS),
        in_specs=[row, _resident(g.shape), up, up, down], out_specs=row,
        scratch_shapes=[pltpu.VMEM((tm, D_MODEL), _BF)],
        compiler_params=_cparams(2), name="ffn_dense",
    )(x2d, g, wg, wu, wd)


_MOE_PICK_PROB = 2.0 / N_EXPERTS
_MOE_CAP_STEPS = (1.0, 1.25, 1.5)


_BF16_ROWS = 16


def _moe_capacities(tm):
    mean = tm * _MOE_PICK_PROB
    return tuple(int(-(-(mean * s) // _BF16_ROWS) * _BF16_ROWS) for s in _MOE_CAP_STEPS)


def _moe_body(caps, final_norm, x_ref, g_ref, wr_ref, wg_ref, wu_ref, wd_ref, gf_ref, o_ref, hf_ref, comb_ref,
              rank_ref, rank_t_ref, cnt_ref):
    e = pl.program_id(1)
    tm = x_ref.shape[0]
    lane = lax.broadcasted_iota(jnp.int32, (tm, LANES), 1)

    @pl.when(e == 0)
    def _():
        x = x_ref[...]
        hf = _rms(x, g_ref[...])
        hf_hi = hf.astype(_BF)
        hf_ref[...] = hf_hi
        o_ref[...] = x
        hf_lo = (hf - hf_hi.astype(_F32)).astype(_BF)
        wr = wr_ref[...]
        wr_hi = wr.astype(_BF)
        wr_lo = (wr - wr_hi.astype(_F32)).astype(_BF)
        logits = _dot(hf_hi, wr_hi) + (_dot(hf_hi, wr_lo) + _dot(hf_lo, wr_hi))
        lg = jnp.where(lane < N_EXPERTS, logits, -jnp.inf)
        v1 = jnp.max(lg, axis=-1, keepdims=True)
        i1 = jnp.min(jnp.where(lg == v1, lane, LANES), axis=-1, keepdims=True)
        lg2 = jnp.where(lane == i1, -jnp.inf, lg)
        v2 = jnp.max(lg2, axis=-1, keepdims=True)
        i2 = jnp.min(jnp.where(lg2 == v2, lane, LANES), axis=-1, keepdims=True)
        e2 = jnp.exp(v2 - v1)
        comb_ref[...] = jnp.where(lane == i1, 1.0 / (1.0 + e2), 0.0) + jnp.where(lane == i2, e2 / (1.0 + e2), 0.0)
        picked = (lane == i1) | (lane == i2)
        picked_b = picked.astype(_BF)
        blk_r = lax.broadcasted_iota(jnp.int32, (LANES, LANES), 0)
        blk_c = lax.broadcasted_iota(jnp.int32, (LANES, LANES), 1)
        earlier = (blk_c < blk_r).astype(_BF)
        running = jnp.zeros((1, LANES), _F32)
        for b in range(tm // LANES):
            rows = slice(b * LANES, (b + 1) * LANES)
            pb = picked_b[rows, :]
            rank_b = jnp.where(picked[rows, :], _dot(earlier, pb) + running, -1.0)
            rank_ref[rows, :] = rank_b
            rank_t_ref[:, rows] = rank_b.T
            running = running + jnp.sum(pb.astype(_F32), axis=0, keepdims=True)
        cnt_ref[...] = running

    lane_row = lax.broadcasted_iota(jnp.int32, (1, LANES), 1)
    count = jnp.sum(jnp.where(lane_row == e, cnt_ref[...], 0.0)).astype(jnp.int32)

    def compact_path(cap):
        cap_pad = -(-cap // LANES) * LANES
        rank_row = rank_t_ref[pl.ds(e, 1), :]
        slot_col = lax.broadcasted_iota(jnp.int32, (cap, 1), 0).astype(_F32)
        pick = (rank_row == slot_col).astype(_BF)
        xs = _dot(pick, hf_ref[...]).astype(_BF)
        y = _swiglu_chunk(xs, wg_ref, wu_ref, wd_ref).astype(_BF)
        if cap_pad > cap:
            y = jnp.concatenate([y, jnp.zeros((cap_pad - cap, D_MODEL), _BF)], axis=0)
        rank_col = jnp.sum(jnp.where(lane == e, rank_ref[...], 0.0), axis=1, keepdims=True)
        comb_col = jnp.sum(jnp.where(lane == e, comb_ref[...], 0.0), axis=1, keepdims=True)
        slot_row = lax.broadcasted_iota(jnp.int32, (1, cap_pad), 1).astype(_F32)
        put = (rank_col == slot_row).astype(_BF)
        o_ref[...] += comb_col * _dot(put, y)

    lower = 0
    for cap in caps:
        pl.when((count > lower) & (count <= cap))(functools.partial(compact_path, cap))
        lower = cap

    @pl.when(count > caps[-1])
    def _():
        comb_col = jnp.sum(jnp.where(lane == e, comb_ref[...], 0.0), axis=1, keepdims=True)
        o_ref[...] += comb_col * _swiglu_chunk(hf_ref[...], wg_ref, wu_ref, wd_ref)

    if final_norm:
        @pl.when(e == pl.num_programs(1) - 1)
        def _():
            o_ref[...] = _rms(o_ref[...], gf_ref[...])


def _moe(x2d, g, wr, wg, wu, wd, gf, final_norm, tm):
    m = x2d.shape[0]
    n_e, _, d_ff = wg.shape
    assert tm % LANES == 0 and m % tm == 0, (m, tm)
    caps = _moe_capacities(tm)
    row_once = pl.BlockSpec((tm, D_MODEL), lambda i, e: (i, 0), pipeline_mode=pl.Buffered(1))
    up = pl.BlockSpec((1, D_MODEL, d_ff), lambda i, e: (e, 0, 0))
    down = pl.BlockSpec((1, d_ff, D_MODEL), lambda i, e: (e, 0, 0))
    return pl.pallas_call(
        functools.partial(_moe_body, caps, final_norm), out_shape=jax.ShapeDtypeStruct(x2d.shape, _F32),
        grid=(m // tm, n_e),
        in_specs=[row_once, _resident(g.shape), _resident(wr.shape), up, up, down, _resident(gf.shape)],
        out_specs=row_once,
        scratch_shapes=[pltpu.VMEM((tm, D_MODEL), _BF), pltpu.VMEM((tm, LANES), _F32), pltpu.VMEM((tm, LANES), _F32),
                        pltpu.VMEM((LANES, tm), _F32), pltpu.VMEM((1, LANES), _F32)],
        compiler_params=_cparams(2), name="moe",
    )(x2d, g, wr, wg, wu, wd, gf)


_SC_CONV = 3


def _mix_c_body(per_seq, seq, x_ref, g_ref, win_ref, cw_ref, e_ref, wout_ref, gm_ref, wq_ref, xo_ref, q_ref, p_ref,
                pp_ref):
    tm = x_ref.shape[0]
    x = x_ref[...]
    hn = _rms(x, g_ref[...]).astype(_BF)
    bg = _dot(hn, win_ref[:, 0:D_MODEL])
    p = _dot(hn, win_ref[:, D_MODEL:2 * D_MODEL]) * _dot(hn, win_ref[:, 2 * D_MODEL:3 * D_MODEL])
    if per_seq:
        @pl.when(pl.program_id(1) == 0)
        def _():
            pp_ref[0:SUBLANES, :] = jnp.zeros((SUBLANES, D_MODEL), _F32)
    else:
        pp_ref[0:SUBLANES, :] = jnp.zeros((SUBLANES, D_MODEL), _F32)
    pp_ref[SUBLANES:SUBLANES + tm, :] = p
    p1 = pp_ref[SUBLANES - 1:SUBLANES - 1 + tm, :]
    p2 = pp_ref[SUBLANES - 2:SUBLANES - 2 + tm, :]
    if per_seq:
        pp_ref[0:SUBLANES, :] = p[tm - SUBLANES:tm, :]
    else:
        t, _ = _pos_and_seq(tm, seq)
        p1 = jnp.where(t >= 1, p1, 0.0) + e_ref[0]
        p2 = jnp.where(t >= 2, p2, 0.0) + e_ref[1]
    conv = p2 * cw_ref[0:1, :] + p1 * cw_ref[1:2, :] + p * cw_ref[2:3, :]
    xn = x + _dot((bg * conv).astype(_BF), wout_ref[...])
    xo_ref[...] = xn
    q_ref[...] = _dot(_rms(xn, gm_ref[...]).astype(_BF), wq_ref[...]).astype(q_ref.dtype)
    if per_seq:
        @pl.when(pl.program_id(1) == pl.num_programs(1) - 1)
        def _():
            p_ref[0] = pp_ref[SUBLANES + tm - (_SC_CONV - 1):SUBLANES + tm, :]
    else:
        p_ref[...] = p


def _mix_c(x2d, g, win, cw, e_buf, wout, gm, wq, q_dtype, per_seq, bsz, seq, tm):
    m = x2d.shape[0]
    if per_seq:
        nt = seq // tm
        grid = (bsz, nt)
        imap = lambda b, j: (b * nt + j, 0)
        e_spec = pl.BlockSpec((_SC_CONV - 1, SUBLANES, D_MODEL), lambda b, j: (0, 0, 0))
        p_shape = jax.ShapeDtypeStruct((bsz, _SC_CONV - 1, D_MODEL), _F32)
        p_spec = pl.BlockSpec((1, _SC_CONV - 1, D_MODEL), lambda b, j: (b, 0, 0))
    else:
        grid = (m // tm, 1)
        imap = lambda i, j: (i, 0)
        e_spec = pl.BlockSpec((_SC_CONV - 1, tm, D_MODEL), lambda i, j: (0, i, 0))
        p_shape = jax.ShapeDtypeStruct((m, D_MODEL), _F32)
        p_spec = pl.BlockSpec((tm, D_MODEL), imap)
    row = pl.BlockSpec((tm, D_MODEL), imap)
    outs = (jax.ShapeDtypeStruct((m, D_MODEL), _F32), jax.ShapeDtypeStruct((m, D_MODEL), q_dtype), p_shape)
    return pl.pallas_call(
        functools.partial(_mix_c_body, per_seq, seq), out_shape=outs, grid=grid,
        in_specs=[row, _resident(g.shape), _resident(win.shape), _resident(cw.shape), e_spec, _resident(wout.shape),
                  _resident(gm.shape), _resident(wq.shape)],
        out_specs=(row, row, p_spec),
        scratch_shapes=[pltpu.VMEM((SUBLANES + tm, D_MODEL), _F32)],
        compiler_params=_cparams(2), name="mix_c",
    )(x2d, g, win, cw, e_buf, wout, gm, wq)


def _expand_buf(buf, seq, n_back):
    bsz, km1, ch = buf.shape
    outs = []
    for j in range(1, n_back + 1):
        rows = [buf[:, km1 - j + t] if t < j else jnp.zeros((bsz, ch), buf.dtype) for t in range(seq)]
        outs.append(jnp.stack(rows, axis=1).reshape(bsz * seq, ch))
    return jnp.stack(outs)


def _row(vec, width=None):
    vec = vec.reshape(1, -1).astype(_F32)
    if width is not None and vec.shape[1] < width:
        vec = jnp.pad(vec, ((0, 0), (0, width - vec.shape[1])))
    return vec


def _trunk(x, k_all, v_all, prompt, states, w):
    bsz, seq, _ = x.shape
    m = bsz * seq
    x2d = x.reshape(m, D_MODEL)
    tm = min(_TOKEN_TILE, m)
    dskip = _row(jnp.repeat(w["d_skip"][0], SSM_HEAD_DIM))
    common = (w["conv_w_ssm"][0], _row(w["conv_b_ssm"][0]), _row(w["dt_bias"][0], LANES), _row(w["a_log"][0], LANES),
              dskip, _row(w["g_ssm_norm"][0]))

    act_dtype = _BF if prompt else _F32
    u, v, z, xbc, dt = _inproj_ab(x2d, _row(w["g_mix"][0]), w["w_in_ab"], act_dtype, tm)
    if prompt:
        y_ab, ssm_state, conv_state = _mix_ab_prompt(
            u, v, z, xbc, dt, bsz, seq, w["w_spatial"][0], w["b_spatial"][0][:, :, None], *common)
        v_out = None
    else:
        ws = w["w_spatial"][0][:, :seq, :seq]
        coef = []
        for d in range(seq):
            per_t = [ws[:, t, t - d] if t >= d else jnp.zeros((A_GROUPS,), _F32) for t in range(seq)]
            coef.append(jnp.tile(jnp.repeat(jnp.stack(per_t), A_HEAD, axis=1), (_SEQ_BLK, 1)))
        sb = jnp.tile(jnp.repeat(w["b_spatial"][0][:, :seq].T, A_HEAD, axis=1), (_SEQ_BLK, 1))
        e_buf = _expand_buf(states["ssm_conv"], seq, SSM_CONV - 1)
        y_ab, ssm_state = _mix_ab_sample(u, v, z, xbc, dt, e_buf, states["ssm"].reshape(bsz, SSM_DIM, SSM_STATE), seq,
                                         jnp.stack(coef), sb, *common)
        conv_state = xbc.reshape(bsz, seq, SSM_CONV_DIM)[:, seq - (SSM_CONV - 1):]
        v_out = v.reshape(bsz, seq, D_MODEL)
    x2d, q = _out_q(y_ab, x2d, w["w_out_ab"], _row(w["g_mem"][0]), w["w_mem_q"][0], act_dtype, tm)

    def attend(q, x2d, layer):
        if prompt:
            return _attn_prompt(q, k_all, v_all, x2d, w["w_mem_o"][layer], layer, bsz, seq, tm)
        return _attn_sample(q, k_all, v_all, x2d, w["w_mem_o"][layer], layer, bsz, seq)

    x2d = attend(q, x2d, 0)
    x2d = _ffn(x2d, _row(w["g_ffn"][0]), w["w_ffn_gate"], w["w_ffn_up"], w["w_ffn_down"], min(_MOE_TILE, m))

    if prompt:
        e_buf = jnp.zeros((_SC_CONV - 1, SUBLANES, D_MODEL), _F32)
        x2d, q, sconv_state = _mix_c(x2d, _row(w["g_mix"][1]), w["w_in_c"], w["conv_w_c"][0], e_buf, w["w_out_c"],
                                     _row(w["g_mem"][1]), w["w_mem_q"][1], act_dtype, True, bsz, seq, tm)
    else:
        e_buf = _expand_buf(states["sconv"], seq, _SC_CONV - 1)
        x2d, q, p = _mix_c(x2d, _row(w["g_mix"][1]), w["w_in_c"], w["conv_w_c"][0], e_buf, w["w_out_c"],
                           _row(w["g_mem"][1]), w["w_mem_q"][1], act_dtype, False, bsz, seq, tm)
        sconv_state = p.reshape(bsz, seq, D_MODEL)[:, seq - (_SC_CONV - 1):]
    x2d = attend(q, x2d, 1)
    x2d = _moe(x2d, _row(w["g_ffn"][1]), w["w_router"], w["w_exp_gate"], w["w_exp_up"], w["w_exp_down"],
               _row(w["g_final"]), True, min(_MOE_TILE, m))
    y = x2d.reshape(bsz, seq, D_MODEL)
    ssm_state = ssm_state.reshape(1, bsz, SSM_HEADS, SSM_HEAD_DIM, SSM_STATE)
    return y, ssm_state, conv_state[None], sconv_state[None], v_out


def kernel(x_prompt, x_sample, mem_prompt, state_ssm, state_ssm_conv, state_sconv, cache_mem_k, cache_mem_v, g_mix, g_mem, g_ffn, g_final, w_in_ab, w_spatial, b_spatial, conv_w_ssm, conv_b_ssm, dt_bias, a_log, d_skip, g_ssm_norm, w_out_ab, w_ffn_gate, w_ffn_up, w_ffn_down, w_in_c, conv_w_c, w_out_c, w_router, w_exp_gate, w_exp_up, w_exp_down, w_mem_q, w_mem_k, w_mem_v, w_mem_o):
    depth = w_mem_q.shape[0]
    bp = x_prompt.shape[0]
    bs = x_sample.shape[0]
    bf = lambda a: a.astype(_BF)
    w = dict(
        g_mix=g_mix, g_mem=g_mem, g_ffn=g_ffn, g_final=g_final, w_spatial=w_spatial, b_spatial=b_spatial,
        conv_w_ssm=conv_w_ssm, conv_b_ssm=conv_b_ssm, dt_bias=dt_bias, a_log=a_log, d_skip=d_skip, g_ssm_norm=g_ssm_norm,
        conv_w_c=conv_w_c,
        w_in_ab=jnp.pad(bf(w_in_ab[0]), ((0, 0), (0, LANES - SSM_HEADS))),
        w_out_ab=bf(w_out_ab[0]), w_ffn_gate=bf(w_ffn_gate), w_ffn_up=bf(w_ffn_up), w_ffn_down=bf(w_ffn_down),
        w_in_c=bf(w_in_c[0]), w_out_c=bf(w_out_c[0]),
        w_router=jnp.pad(w_router[0], ((0, 0), (0, LANES - N_EXPERTS))),
        w_exp_gate=bf(w_exp_gate[0]), w_exp_up=bf(w_exp_up[0]), w_exp_down=bf(w_exp_down[0]),
        w_mem_q=bf(w_mem_q), w_mem_o=bf(w_mem_o),
    )
    mem_k_p, mem_v_p, k_bf, v_bf = _mem_kv(mem_prompt, bf(w_mem_k), bf(w_mem_v))
    y_p, ssm_p, ssmconv_p, sconv_p, _ = _trunk(x_prompt, k_bf, v_bf, True, None, w)
    states = dict(ssm=state_ssm[0], ssm_conv=state_ssm_conv[0], sconv=state_sconv[0])
    y_s, ssm_s, ssmconv_s, sconv_s, v_s = _trunk(
        x_sample, cache_mem_k.reshape(depth * bs, N_MEM, MEM_HEADS, MEM_HEAD_DIM),
        cache_mem_v.reshape(depth * bs, N_MEM, MEM_HEADS, MEM_HEAD_DIM), False, states, w)
    return (y_p, y_s, ssm_p, ssm_s, ssmconv_p, ssmconv_s, sconv_p, sconv_s, mem_k_p, mem_v_p, v_s[None])
```

```python
import functools

import jax
import jax.numpy as jnp
from jax import lax
from jax.experimental import pallas as pl
from jax.experimental.pallas import tpu as pltpu

D_MODEL = 1024
EPS = 1e-6
CHUNK = 128
A_GROUPS = 8
A_HEAD = D_MODEL // A_GROUPS
SSM_HEADS = 16
SSM_HEAD_DIM = 64
SSM_GROUPS = 2
SSM_STATE = 128
SSM_CONV = 4
SSM_DIM = D_MODEL
SSM_CONV_DIM = SSM_DIM + 2 * SSM_GROUPS * SSM_STATE
N_MEM = 256
MEM_HEADS = 4
MEM_HEAD_DIM = D_MODEL // MEM_HEADS
N_EXPERTS = 8

LANES = 128
SUBLANES = 8
VMEM_LIMIT = 56 * 1024 * 1024
_MOE_VMEM_LIMIT = 58 * 1024 * 1024
_TOKEN_TILE = 1024

_BF = jnp.bfloat16
_F32 = jnp.float32
_NT = (((1,), (1,)), ((), ()))


def _cparams(n_axes, vmem_limit=VMEM_LIMIT):
    return pltpu.CompilerParams(dimension_semantics=("arbitrary",) * n_axes, vmem_limit_bytes=vmem_limit)


def _resident(shape):
    nd = len(shape)
    return pl.BlockSpec(shape, lambda *_: (0,) * nd, pipeline_mode=pl.Buffered(1))


def _rms(x, g):
    ms = jnp.mean(x * x, axis=-1, keepdims=True)
    return x * lax.rsqrt(ms + EPS) * g


def _silu(x):
    return x * jax.nn.sigmoid(x)


def _dot(a, b):
    return jnp.dot(a, b, preferred_element_type=_F32)


def _softplus(x):
    return jnp.maximum(x, 0.0) + jnp.log(1.0 + jnp.exp(-jnp.abs(x)))


def _split3(x):
    p1 = x.astype(_BF)
    r1 = x - p1.astype(_F32)
    p2 = r1.astype(_BF)
    p3 = (r1 - p2.astype(_F32)).astype(_BF)
    return p1, p2, p3


def _dot_sel_lhs(sel, x):
    p1, p2, p3 = _split3(x)
    return _dot(sel, p1) + _dot(sel, p2) + _dot(sel, p3)


def _dot_sel_rhs(x, sel):
    p1, p2, p3 = _split3(x)
    return _dot(p1, sel) + _dot(p2, sel) + _dot(p3, sel)


def _pos_and_seq(rows, seq):
    r = lax.broadcasted_iota(jnp.int32, (rows, 1), 0)
    if seq & (seq - 1) == 0:
        return r & (seq - 1), r >> (seq.bit_length() - 1)
    return lax.rem(r, seq), lax.div(r, seq)


def _kv_body(m_ref, wk_ref, wv_ref, k_ref, v_ref, kb_ref, vb_ref):
    m = m_ref[0].astype(_BF)
    for w_ref, o_ref, ob_ref in ((wk_ref, k_ref, kb_ref), (wv_ref, v_ref, vb_ref)):
        res = _dot(m, w_ref[0])
        ob_ref[0] = res.astype(_BF)
        for h in range(MEM_HEADS):
            o_ref[0, 0, :, h, :] = res[:, h * MEM_HEAD_DIM:(h + 1) * MEM_HEAD_DIM]


def _mem_kv(mem, wk, wv):
    bsz, n, d = mem.shape
    depth = wk.shape[0]
    out = jax.ShapeDtypeStruct((depth, bsz, n, MEM_HEADS, MEM_HEAD_DIM), _F32)
    out_b = jax.ShapeDtypeStruct((depth * bsz, n, d), _BF)
    wspec = pl.BlockSpec((1, d, d), lambda l, b: (l, 0, 0))
    ospec = pl.BlockSpec((1, 1, n, MEM_HEADS, MEM_HEAD_DIM), lambda l, b: (l, b, 0, 0, 0))
    bspec = pl.BlockSpec((1, n, d), lambda l, b: (l * bsz + b, 0, 0))
    return pl.pallas_call(
        _kv_body, out_shape=(out, out, out_b, out_b), grid=(depth, bsz),
        in_specs=[pl.BlockSpec((1, n, d), lambda l, b: (b, 0, 0)), wspec, wspec],
        out_specs=(ospec, ospec, bspec, bspec), compiler_params=_cparams(2), name="mem_kv",
    )(mem, wk, wv)


_PROJ_CHUNK = 512


def _inproj_ab_body(x_ref, g_ref, w_ref, u_ref, v_ref, z_ref, xbc_ref, dt_ref):
    hn = _rms(x_ref[...], g_ref[...]).astype(_BF)
    col = 0
    for ref, width, act in ((u_ref, D_MODEL, True), (v_ref, D_MODEL, True), (z_ref, D_MODEL, False),
                            (xbc_ref, SSM_CONV_DIM, False)):
        for c in range(width // _PROJ_CHUNK):
            r = _dot(hn, w_ref[:, col:col + _PROJ_CHUNK])
            if act:
                r = jax.nn.gelu(r)
            ref[:, c * _PROJ_CHUNK:(c + 1) * _PROJ_CHUNK] = r.astype(ref.dtype)
            col += _PROJ_CHUNK
    dt_ref[...] = _dot(hn, w_ref[:, col:col + LANES])


def _inproj_ab(x2d, g, w_in, act_dtype, tm):
    m = x2d.shape[0]
    row = lambda width: pl.BlockSpec((tm, width), lambda i: (i, 0))
    outs = (jax.ShapeDtypeStruct((m, D_MODEL), act_dtype),) * 3 + (
        jax.ShapeDtypeStruct((m, SSM_CONV_DIM), _F32), jax.ShapeDtypeStruct((m, LANES), _F32))
    return pl.pallas_call(
        _inproj_ab_body, out_shape=outs, grid=(m // tm,),
        in_specs=[row(D_MODEL), _resident(g.shape), _resident(w_in.shape)],
        out_specs=(row(D_MODEL), row(D_MODEL), row(D_MODEL), row(SSM_CONV_DIM), row(LANES)),
        compiler_params=_cparams(1), name="inproj_ab",
    )(x2d, g, w_in)


_PAIR = 2 * SSM_HEAD_DIM
_HPG = SSM_HEADS // SSM_GROUPS
_GRP_DIM = SSM_DIM // SSM_GROUPS


def _mix_ab_prompt_body(n_sub, u_ref, v_ref, z_ref, xbc_ref, dt_ref, ws_ref, bs_ref, cw_ref, cb_ref, dtb_ref, alog_ref,
                        dskip_ref, gn_ref, y_ref, st_ref, cst_ref, xp_ref, sT_ref, wsb_ref):
    q = CHUNK
    c = pl.program_id(1)
    tri = lax.broadcasted_iota(jnp.int32, (q, q), 0) >= lax.broadcasted_iota(jnp.int32, (q, q), 1)

    @pl.when((pl.program_id(0) == 0) & (c == 0))
    def _():
        for g in range(A_GROUPS):
            wsb_ref[g] = jnp.where(tri, ws_ref[g], 0.0).astype(_BF)

    @pl.when(c == 0)
    def _():
        xp_ref[0, 0:SUBLANES, :] = jnp.zeros((SUBLANES, SSM_CONV_DIM), _F32)
        sT_ref[...] = jnp.zeros_like(sT_ref)

    for sub in range(n_sub):
        _mix_ab_prompt_chunk(sub, n_sub, tri, u_ref, v_ref, z_ref, xbc_ref, dt_ref, bs_ref, cw_ref, cb_ref, dtb_ref,
                             alog_ref, dskip_ref, gn_ref, y_ref, xp_ref, sT_ref, wsb_ref)

    @pl.when(c == pl.num_programs(1) - 1)
    def _():
        for p in range(SSM_HEADS // 2):
            st_ref[0, p * _PAIR:(p + 1) * _PAIR, :] = sT_ref[:, p * _PAIR:(p + 1) * _PAIR].T
        cst_ref[0] = xp_ref[n_sub - 1, SUBLANES + q - (SSM_CONV - 1):SUBLANES + q, :]


def _mix_ab_prompt_chunk(sub, n_sub, tri, u_ref, v_ref, z_ref, xbc_ref, dt_ref, bs_ref, cw_ref, cb_ref, dtb_ref,
                         alog_ref, dskip_ref, gn_ref, y_ref, xp_ref, sT_ref, wsb_ref):
    q = CHUNK
    rows = slice(sub * q, (sub + 1) * q)

    for g in range(A_GROUPS):
        sl = slice(g * A_HEAD, (g + 1) * A_HEAD)
        s = _dot(wsb_ref[g], v_ref[rows, sl].astype(_BF)) + bs_ref[g]
        y_ref[rows, sl] = (u_ref[rows, sl].astype(_F32) * s).astype(y_ref.dtype)

    xbc = xbc_ref[rows, :]
    xp_ref[sub, SUBLANES:SUBLANES + q, :] = xbc
    base = SUBLANES - (SSM_CONV - 1)
    conv = xp_ref[sub, base:base + q, :] * cw_ref[0:1, :]
    for k in range(1, SSM_CONV):
        conv = conv + xp_ref[sub, base + k:base + k + q, :] * cw_ref[k:k + 1, :]
    xp_ref[(sub + 1) % n_sub, 0:SUBLANES, :] = xbc[q - SUBLANES:q, :]
    act = _silu(conv + cb_ref[...])
    xs = act[:, :SSM_DIM]
    bm = act[:, SSM_DIM:SSM_DIM + SSM_GROUPS * SSM_STATE]
    cm = act[:, SSM_DIM + SSM_GROUPS * SSM_STATE:]

    dt = _softplus(dt_ref[rows, :] + dtb_ref[...])
    d_a = dt * (-jnp.exp(alog_ref[...]))
    la = _dot_sel_lhs(tri.astype(_BF), d_a)
    la_t = la.T
    dt_t = dt.T
    e_la = jnp.exp(la)
    last = la[q - 1:q, :]
    wend = jnp.exp(last - la) * dt
    e_last = jnp.exp(last)
    lo = lax.broadcasted_iota(jnp.int32, (q, _PAIR), 1) < SSM_HEAD_DIM
    lo_row = lo[0:1, :]

    for g in range(SSM_GROUPS):
        bm_t = bm[:, g * SSM_STATE:(g + 1) * SSM_STATE].T.astype(_BF)
        cm_g = cm[:, g * SSM_STATE:(g + 1) * SSM_STATE].astype(_BF)
        cb = _dot(cm_g, bm_t)
        ys = []
        for pp in range(_HPG // 2):
            p = g * (_HPG // 2) + pp
            cols = slice(p * _PAIR, (p + 1) * _PAIR)
            wts = []
            for h in (2 * p, 2 * p + 1):
                seg = la[:, h:h + 1] - la_t[h:h + 1, :]
                wts.append(cb * jnp.exp(jnp.where(tri, seg, -jnp.inf)) * dt_t[h:h + 1, :])
            w2 = jnp.concatenate(wts, axis=1).astype(_BF)
            x_pair = xs[:, cols]
            x2 = jnp.concatenate([jnp.where(lo, x_pair, 0.0), jnp.where(lo, 0.0, x_pair)], axis=0).astype(_BF)
            s_t = sT_ref[:, cols]
            e_pair = jnp.where(lo, e_la[:, 2 * p:2 * p + 1], e_la[:, 2 * p + 1:2 * p + 2])
            y_pair = _dot(w2, x2) + _dot(cm_g, s_t.astype(_BF)) * e_pair
            wd_pair = jnp.where(lo, wend[:, 2 * p:2 * p + 1], wend[:, 2 * p + 1:2 * p + 2])
            el_pair = jnp.where(lo_row, e_last[:, 2 * p:2 * p + 1], e_last[:, 2 * p + 1:2 * p + 2])
            sT_ref[:, cols] = s_t * el_pair + _dot(bm_t, (x_pair * wd_pair).astype(_BF))
            ys.append(y_pair + x_pair * dskip_ref[:, cols])
        gcols = slice(g * _GRP_DIM, (g + 1) * _GRP_DIM)
        gated = jnp.concatenate(ys, axis=1) * _silu(z_ref[rows, gcols].astype(_F32))
        y_ref[rows, D_MODEL + g * _GRP_DIM:D_MODEL + (g + 1) * _GRP_DIM] = _rms(gated, gn_ref[:, gcols]).astype(y_ref.dtype)


_MIX_SUB_CHUNKS = 2


def _mix_ab_prompt(u, v, z, xbc, dt, bsz, seq, ws, bs, cw, cb, dtb, alog, dskip, gn):
    n_sub = _MIX_SUB_CHUNKS
    nc = seq // (CHUNK * n_sub)
    row = lambda width: pl.BlockSpec((CHUNK * n_sub, width), lambda b, c: (b * nc + c, 0))
    small = [ws, bs, cw, cb, dtb, alog, dskip, gn]
    outs = (jax.ShapeDtypeStruct((bsz * seq, 2 * D_MODEL), _BF),
            jax.ShapeDtypeStruct((bsz, SSM_DIM, SSM_STATE), _F32),
            jax.ShapeDtypeStruct((bsz, SSM_CONV - 1, SSM_CONV_DIM), _F32))
    return pl.pallas_call(
        functools.partial(_mix_ab_prompt_body, n_sub), out_shape=outs, grid=(bsz, nc),
        in_specs=[row(D_MODEL), row(D_MODEL), row(D_MODEL), row(SSM_CONV_DIM), row(LANES)]
        + [_resident(a.shape) for a in small],
        out_specs=(row(2 * D_MODEL),
                   pl.BlockSpec((1, SSM_DIM, SSM_STATE), lambda b, c: (b, 0, 0)),
                   pl.BlockSpec((1, SSM_CONV - 1, SSM_CONV_DIM), lambda b, c: (b, 0, 0))),
        scratch_shapes=[pltpu.VMEM((n_sub, SUBLANES + CHUNK, SSM_CONV_DIM), _F32),
                        pltpu.VMEM((SSM_STATE, SSM_DIM), _F32), pltpu.VMEM((A_GROUPS, CHUNK, CHUNK), _BF)],
        compiler_params=_cparams(2), name="mix_ab_prompt",
    )(u, v, z, xbc, dt, *small)


_SEQ_BLK = 8


def _mix_ab_sample_body(seq, u_ref, v_ref, z_ref, xbc_ref, dt_ref, e_ref, st_ref, coef_ref, sb_ref, cw_ref, cb_ref,
                        dtb_ref, alog_ref, dskip_ref, gn_ref, y_ref, sto_ref, sh_ref):
    rows = _SEQ_BLK * seq
    t, seq_of_row = _pos_and_seq(rows, seq)
    pad_rows = sh_ref.shape[0]
    sh_ref[...] = jnp.zeros((pad_rows, SSM_CONV_DIM), _F32)

    def shifts(val, deltas):
        width = val.shape[1]
        sh_ref[SUBLANES:SUBLANES + rows, 0:width] = val
        return [sh_ref[SUBLANES - d:SUBLANES - d + rows, 0:width] for d in deltas]

    back = tuple(range(1, seq))

    v = v_ref[...]
    s = coef_ref[0] * v + sb_ref[...]
    for d, vs in zip(back, shifts(v, back)):
        s = s + coef_ref[d] * jnp.where(t >= d, vs, 0.0)
    y_ref[:, 0:D_MODEL] = u_ref[...] * s

    xbc = xbc_ref[...]
    conv = xbc * cw_ref[SSM_CONV - 1:SSM_CONV, :]
    for j, xsft in zip(range(1, SSM_CONV), shifts(xbc, tuple(range(1, SSM_CONV)))):
        conv = conv + (jnp.where(t >= j, xsft, 0.0) + e_ref[j - 1]) * cw_ref[SSM_CONV - 1 - j:SSM_CONV - j, :]
    act = _silu(conv + cb_ref[...])
    xs = act[:, :SSM_DIM]
    bm = act[:, SSM_DIM:SSM_DIM + SSM_GROUPS * SSM_STATE]
    cm = act[:, SSM_DIM + SSM_GROUPS * SSM_STATE:]

    dt = _softplus(dt_ref[...] + dtb_ref[...])
    d_a = dt * (-jnp.exp(alog_ref[...]))
    la = d_a
    for d, sft in zip(back, shifts(d_a, back)):
        la = la + jnp.where(t >= d, sft, 0.0)
    la_back = shifts(la, back)
    dt_back = shifts(dt, back)
    last = jnp.where(t == seq - 1, la, 0.0)
    fwd = shifts(la, tuple(-d for d in back))
    for d, sft in zip(back, fwd):
        last = last + jnp.where(t == seq - 1 - d, sft, 0.0)
    wend = jnp.exp(last - la) * dt
    e_last = jnp.exp(last)
    e_la = jnp.exp(la)

    head_lane = lax.broadcasted_iota(jnp.int32, (rows, LANES), 1)
    bm_back = shifts(bm, back)

    def cb_heads(b_other):
        per_g = [jnp.sum(cm[:, g * SSM_STATE:(g + 1) * SSM_STATE] * b_other[:, g * SSM_STATE:(g + 1) * SSM_STATE],
                         axis=-1, keepdims=True) for g in range(SSM_GROUPS)]
        return jnp.where(head_lane < _HPG, per_g[0], per_g[1])

    gs = [cb_heads(bm) * dt]
    for i, d in enumerate(back):
        gd = cb_heads(bm_back[i]) * jnp.exp(la - la_back[i]) * dt_back[i]
        gs.append(jnp.where(t >= d, gd, 0.0))

    head0 = lax.broadcasted_iota(jnp.int32, (LANES, SSM_DIM), 0) * SSM_HEAD_DIM
    chan = lax.broadcasted_iota(jnp.int32, (LANES, SSM_DIM), 1)
    expand = ((chan >= head0) & (chan < head0 + SSM_HEAD_DIM)).astype(_BF)
    stacked = jnp.concatenate([e_la, wend, e_last] + gs, axis=0)
    full = _dot_sel_rhs(stacked, expand)
    e_la_f, wend_f, e_last_f = full[0:rows], full[rows:2 * rows], full[2 * rows:3 * rows]
    y2 = full[3 * rows:4 * rows] * xs
    for i, (d, xsft) in enumerate(zip(back, shifts(xs, back))):
        y2 = y2 + full[(4 + i) * rows:(5 + i) * rows] * jnp.where(t >= d, xsft, 0.0)

    zpad = jnp.zeros((LANES - rows, SSM_DIM), _F32)
    xw_t = jnp.concatenate([xs * wend_f, zpad], axis=0).T.astype(_BF)
    el_t = jnp.concatenate([e_last_f, zpad], axis=0).T
    row_lane = lax.broadcasted_iota(jnp.int32, (_GRP_DIM, LANES), 1)
    bpad = jnp.zeros((LANES - rows, SSM_STATE), _F32)

    y1 = [jnp.zeros((rows, _GRP_DIM), _F32) for _ in range(SSM_GROUPS)]
    for i in range(_SEQ_BLK):
        mine = seq_of_row == i
        for g in range(SSM_GROUPS):
            gr = slice(g * _GRP_DIM, (g + 1) * _GRP_DIM)
            st = st_ref[i, gr, :]
            b_g = jnp.where(mine, bm[:, g * SSM_STATE:(g + 1) * SSM_STATE], 0.0)
            b_g = jnp.concatenate([b_g, bpad], axis=0).astype(_BF)
            decay = jnp.sum(jnp.where(row_lane == i * seq, el_t[gr, :], 0.0), axis=1, keepdims=True)
            sto_ref[i, gr, :] = st * decay + _dot(xw_t[gr, :], b_g)
            c_g = cm[:, g * SSM_STATE:(g + 1) * SSM_STATE].astype(_BF)
            ch = lax.dot_general(c_g, st.astype(_BF), _NT, preferred_element_type=_F32)
            y1[g] = y1[g] + jnp.where(mine, ch, 0.0)

    for g in range(SSM_GROUPS):
        gr = slice(g * _GRP_DIM, (g + 1) * _GRP_DIM)
        ys = y1[g] * e_la_f[:, gr] + y2[:, gr] + xs[:, gr] * dskip_ref[:, gr]
        gated = ys * _silu(z_ref[:, gr])
        y_ref[:, D_MODEL + g * _GRP_DIM:D_MODEL + (g + 1) * _GRP_DIM] = _rms(gated, gn_ref[:, gr])


def _mix_ab_sample(u, v, z, xbc, dt, e_buf, state, seq, coef, sb, cw, cb, dtb, alog, dskip, gn):
    bsz = state.shape[0]
    rows = _SEQ_BLK * seq
    row = lambda width: pl.BlockSpec((rows, width), lambda i: (i, 0))
    small = [coef, sb, cw, cb, dtb, alog, dskip, gn]
    st_spec = pl.BlockSpec((_SEQ_BLK, SSM_DIM, SSM_STATE), lambda i: (i, 0, 0))
    outs = (jax.ShapeDtypeStruct((bsz * seq, 2 * D_MODEL), _F32), jax.ShapeDtypeStruct(state.shape, _F32))
    return pl.pallas_call(
        functools.partial(_mix_ab_sample_body, seq), out_shape=outs, grid=(bsz // _SEQ_BLK,),
        in_specs=[row(D_MODEL), row(D_MODEL), row(D_MODEL), row(SSM_CONV_DIM), row(LANES),
                  pl.BlockSpec((SSM_CONV - 1, rows, SSM_CONV_DIM), lambda i: (0, i, 0)), st_spec]
        + [_resident(a.shape) for a in small],
        out_specs=(row(2 * D_MODEL), st_spec),
        scratch_shapes=[pltpu.VMEM((rows + 2 * SUBLANES, SSM_CONV_DIM), _F32)],
        compiler_params=_cparams(1), name="mix_ab_sample",
    )(u, v, z, xbc, dt, e_buf, state, *small)


def _out_q_body(a_ref, x_ref, wo_ref, gm_ref, wq_ref, xo_ref, q_ref):
    xn = x_ref[...] + _dot(a_ref[...].astype(_BF), wo_ref[...])
    xo_ref[...] = xn
    q_ref[...] = _dot(_rms(xn, gm_ref[...]).astype(_BF), wq_ref[...]).astype(q_ref.dtype)


def _out_q(a, x2d, wo, gm, wq, q_dtype, tm):
    m = x2d.shape[0]
    row = lambda width: pl.BlockSpec((tm, width), lambda i: (i, 0))
    outs = (jax.ShapeDtypeStruct((m, D_MODEL), _F32), jax.ShapeDtypeStruct((m, D_MODEL), q_dtype))
    return pl.pallas_call(
        _out_q_body, out_shape=outs, grid=(m // tm,),
        in_specs=[row(a.shape[1]), row(D_MODEL), _resident(wo.shape), _resident(gm.shape), _resident(wq.shape)],
        out_specs=(row(D_MODEL), row(D_MODEL)), compiler_params=_cparams(1), name="out_q",
    )(a, x2d, wo, gm, wq)


_ATT_SCALE = MEM_HEAD_DIM ** -0.5


def _attend(q, k, v):
    outs = []
    for h in range(MEM_HEADS):
        sl = slice(h * MEM_HEAD_DIM, (h + 1) * MEM_HEAD_DIM)
        s = lax.dot_general(q[:, sl], k[:, sl], _NT, preferred_element_type=_F32) * _ATT_SCALE
        p = jnp.exp(s - jnp.max(s, axis=-1, keepdims=True))
        p = p / jnp.sum(p, axis=-1, keepdims=True)
        outs.append(_dot(p.astype(_BF), v[:, sl]))
    return jnp.concatenate(outs, axis=1)


def _attn_prompt_body(q_ref, k_ref, v_ref, x_ref, wo_ref, o_ref):
    o = _attend(q_ref[...], k_ref[0], v_ref[0])
    o_ref[...] = x_ref[...] + _dot(o.astype(_BF), wo_ref[...])


def _attn_prompt(q, k, v, x2d, wo, layer, bsz, seq, tq):
    nt = seq // tq
    row = pl.BlockSpec((tq, D_MODEL), lambda b, j: (b * nt + j, 0))
    kv = pl.BlockSpec((1, N_MEM, D_MODEL), lambda b, j: (layer * bsz + b, 0, 0))
    return pl.pallas_call(
        _attn_prompt_body, out_shape=jax.ShapeDtypeStruct(x2d.shape, _F32), grid=(bsz, nt),
        in_specs=[row, kv, kv, row, _resident(wo.shape)], out_specs=row,
        compiler_params=_cparams(2), name="attn_prompt",
    )(q, k, v, x2d, wo)


def _attn_sample_body(seq, q_ref, k_ref, v_ref, x_ref, wo_ref, o_ref, o_scr):
    per_tile = SUBLANES // seq
    _, seq_of_row = _pos_and_seq(SUBLANES, seq)
    n_lt = MEM_HEAD_DIM // LANES
    rows_mh = N_MEM * MEM_HEADS
    col = lax.broadcasted_iota(jnp.int32, (MEM_HEADS * SUBLANES, rows_mh), 1)
    row = lax.broadcasted_iota(jnp.int32, (MEM_HEADS * SUBLANES, rows_mh), 0)
    same_head = (col & (MEM_HEADS - 1)) == (row >> (SUBLANES.bit_length() - 1))
    for tile in range(_SEQ_BLK // per_tile):
        q8 = q_ref[tile * SUBLANES:(tile + 1) * SUBLANES, :].astype(_BF)
        o = None
        for j in range(per_tile):
            i = tile * per_tile + j
            s = None
            for lt in range(n_lt):
                qh = jnp.concatenate([q8[:, h * MEM_HEAD_DIM + lt * LANES:h * MEM_HEAD_DIM + (lt + 1) * LANES]
                                      for h in range(MEM_HEADS)], axis=0)
                k_lt = k_ref[i, :, :, lt * LANES:(lt + 1) * LANES].reshape(rows_mh, LANES).astype(_BF)
                part = lax.dot_general(qh, k_lt, _NT, preferred_element_type=_F32)
                s = part if s is None else s + part
            s = jnp.where(same_head, s * _ATT_SCALE, -jnp.inf)
            p = jnp.exp(s - jnp.max(s, axis=-1, keepdims=True))
            p = (p / jnp.sum(p, axis=-1, keepdims=True)).astype(_BF)
            pieces = [None] * (MEM_HEADS * n_lt)
            for lt in range(n_lt):
                v_lt = v_ref[i, :, :, lt * LANES:(lt + 1) * LANES].reshape(rows_mh, LANES).astype(_BF)
                o_lt = _dot(p, v_lt)
                for h in range(MEM_HEADS):
                    pieces[h * n_lt + lt] = o_lt[h * SUBLANES:(h + 1) * SUBLANES, :]
            oi = jnp.concatenate(pieces, axis=1)
            o = oi if o is None else jnp.where(seq_of_row == j, oi, o)
        o_scr[tile * SUBLANES:(tile + 1) * SUBLANES, :] = o
    o_ref[...] = x_ref[...] + _dot(o_scr[...].astype(_BF), wo_ref[...])


def _attn_sample(q, k, v, x2d, wo, layer, bsz, seq):
    rows = _SEQ_BLK * seq
    nb = bsz // _SEQ_BLK
    row = pl.BlockSpec((rows, D_MODEL), lambda i: (i, 0))
    kv = pl.BlockSpec((_SEQ_BLK, N_MEM, MEM_HEADS, MEM_HEAD_DIM), lambda i: (layer * nb + i, 0, 0, 0))
    return pl.pallas_call(
        functools.partial(_attn_sample_body, seq), out_shape=jax.ShapeDtypeStruct(x2d.shape, _F32), grid=(nb,),
        in_specs=[row, kv, kv, row, _resident(wo.shape)], out_specs=row,
        scratch_shapes=[pltpu.VMEM((rows, D_MODEL), _F32)],
        compiler_params=_cparams(1), name="attn_sample",
    )(q, k, v, x2d, wo)


_FF_GRID_CHUNKS = 2
_FF_SUB = 512
_MOE_TILE = 1024


def _swiglu_chunk(h, wg_ref, wu_ref, wd_ref):
    width = wg_ref.shape[2]
    out = None
    for lo in range(0, width, _FF_SUB):
        hi = min(lo + _FF_SUB, width)
        hid = _silu(_dot(h, wg_ref[0, :, lo:hi])) * _dot(h, wu_ref[0, :, lo:hi])
        part = _dot(hid.astype(_BF), wd_ref[0, lo:hi, :])
        out = part if out is None else out + part
    return out


def _ffn_body(x_ref, g_ref, wg_ref, wu_ref, wd_ref, o_ref, hf_ref):
    c = pl.program_id(1)

    @pl.when(c == 0)
    def _():
        x = x_ref[...]
        hf_ref[...] = _rms(x, g_ref[...]).astype(_BF)
        o_ref[...] = x

    o_ref[...] += _swiglu_chunk(hf_ref[...], wg_ref, wu_ref, wd_ref)


def _ffn(x2d, g, wg, wu, wd, tm):
    m = x2d.shape[0]
    d_ff = wg.shape[2]
    fc = d_ff // _FF_GRID_CHUNKS
    row = pl.BlockSpec((tm, D_MODEL), lambda i, c: (i, 0))
    up = pl.BlockSpec((1, D_MODEL, fc), lambda i, c: (0, 0, c))
    down = pl.BlockSpec((1, fc, D_MODEL), lambda i, c: (0, c, 0))
    return pl.pallas_call(
        _ffn_body, out_shape=jax.ShapeDtypeStruct(x2d.shape, _F32), grid=(m // tm, _FF_GRID_CHUNKS),
        in_specs=[row, _resident(g.shape), up, up, down], out_specs=row,
        scratch_shapes=[pltpu.VMEM((tm, D_MODEL), _BF)],
        compiler_params=_cparams(2), name="ffn_dense",
    )(x2d, g, wg, wu, wd)


_MOE_PICK_PROB = 2.0 / N_EXPERTS
_MOE_CAP_STEPS = (1.0, 1.25, 1.5)


_BF16_ROWS = 16
_MOE_DENSE_ROWS = 256


def _moe_capacities(tm):
    mean = tm * _MOE_PICK_PROB
    return tuple(int(-(-(mean * s) // _BF16_ROWS) * _BF16_ROWS) for s in _MOE_CAP_STEPS)


def _moe_body(caps, final_norm, x_ref, g_ref, wr_ref, wg_ref, wu_ref, wd_ref, gf_ref, o_ref, hf_ref, comb_ref,
              rank_ref, rank_t_ref, cnt_ref):
    e = pl.program_id(1)
    tm = x_ref.shape[0]
    lane = lax.broadcasted_iota(jnp.int32, (tm, LANES), 1)

    @pl.when(e == 0)
    def _():
        x = x_ref[...]
        hf = _rms(x, g_ref[...])
        hf_hi = hf.astype(_BF)
        hf_ref[...] = hf_hi
        o_ref[...] = x
        hf_lo = (hf - hf_hi.astype(_F32)).astype(_BF)
        wr = wr_ref[...]
        wr_hi = wr.astype(_BF)
        wr_lo = (wr - wr_hi.astype(_F32)).astype(_BF)
        logits = _dot(hf_hi, wr_hi) + (_dot(hf_hi, wr_lo) + _dot(hf_lo, wr_hi))
        lg = jnp.where(lane < N_EXPERTS, logits, -jnp.inf)
        v1 = jnp.max(lg, axis=-1, keepdims=True)
        i1 = jnp.min(jnp.where(lg == v1, lane, LANES), axis=-1, keepdims=True)
        lg2 = jnp.where(lane == i1, -jnp.inf, lg)
        v2 = jnp.max(lg2, axis=-1, keepdims=True)
        i2 = jnp.min(jnp.where(lg2 == v2, lane, LANES), axis=-1, keepdims=True)
        e2 = jnp.exp(v2 - v1)
        comb_ref[...] = jnp.where(lane == i1, 1.0 / (1.0 + e2), 0.0) + jnp.where(lane == i2, e2 / (1.0 + e2), 0.0)
        picked = (lane == i1) | (lane == i2)
        picked_b = picked.astype(_BF)
        blk_r = lax.broadcasted_iota(jnp.int32, (LANES, LANES), 0)
        blk_c = lax.broadcasted_iota(jnp.int32, (LANES, LANES), 1)
        earlier = (blk_c < blk_r).astype(_BF)
        running = jnp.zeros((1, LANES), _F32)
        for b in range(tm // LANES):
            rows = slice(b * LANES, (b + 1) * LANES)
            pb = picked_b[rows, :]
            rank_b = jnp.where(picked[rows, :], _dot(earlier, pb) + running, -1.0)
            rank_ref[rows, :] = rank_b
            rank_t_ref[:, rows] = rank_b.T
            running = running + jnp.sum(pb.astype(_F32), axis=0, keepdims=True)
        cnt_ref[...] = running

    lane_row = lax.broadcasted_iota(jnp.int32, (1, LANES), 1)
    count = jnp.sum(jnp.where(lane_row == e, cnt_ref[...], 0.0)).astype(jnp.int32)

    def compact_path(cap):
        cap_pad = -(-cap // LANES) * LANES
        rank_row = rank_t_ref[pl.ds(e, 1), :]
        slot_col = lax.broadcasted_iota(jnp.int32, (cap, 1), 0).astype(_F32)
        pick = (rank_row == slot_col).astype(_BF)
        xs = _dot(pick, hf_ref[...]).astype(_BF)
        y = _swiglu_chunk(xs, wg_ref, wu_ref, wd_ref).astype(_BF)
        if cap_pad > cap:
            y = jnp.concatenate([y, jnp.zeros((cap_pad - cap, D_MODEL), _BF)], axis=0)
        rank_col = jnp.sum(jnp.where(lane == e, rank_ref[...], 0.0), axis=1, keepdims=True)
        comb_col = jnp.sum(jnp.where(lane == e, comb_ref[...], 0.0), axis=1, keepdims=True)
        slot_row = lax.broadcasted_iota(jnp.int32, (1, cap_pad), 1).astype(_F32)
        put = (rank_col == slot_row).astype(_BF)
        o_ref[...] += comb_col * _dot(put, y)

    lower = 0
    for cap in caps:
        pl.when((count > lower) & (count <= cap))(functools.partial(compact_path, cap))
        lower = cap

    @pl.when(count > caps[-1])
    def _():
        blk = min(tm, _MOE_DENSE_ROWS)
        blk_lane = lax.broadcasted_iota(jnp.int32, (blk, LANES), 1)

        def dense_rows(r, carry):
            rows = pl.ds(pl.multiple_of(r * blk, blk), blk)
            comb_col = jnp.sum(jnp.where(blk_lane == e, comb_ref[rows, :], 0.0), axis=1, keepdims=True)
            o_ref[rows, :] += comb_col * _swiglu_chunk(hf_ref[rows, :], wg_ref, wu_ref, wd_ref)
            return carry

        lax.fori_loop(0, tm // blk, dense_rows, 0)

    if final_norm:
        @pl.when(e == pl.num_programs(1) - 1)
        def _():
            o_ref[...] = _rms(o_ref[...], gf_ref[...])


def _moe(x2d, g, wr, wg, wu, wd, gf, final_norm, tm):
    m = x2d.shape[0]
    n_e, _, d_ff = wg.shape
    assert tm % LANES == 0 and m % tm == 0, (m, tm)
    caps = _moe_capacities(tm)
    row_once = pl.BlockSpec((tm, D_MODEL), lambda i, e: (i, 0), pipeline_mode=pl.Buffered(1))
    up = pl.BlockSpec((1, D_MODEL, d_ff), lambda i, e: (e, 0, 0))
    down = pl.BlockSpec((1, d_ff, D_MODEL), lambda i, e: (e, 0, 0))
    return pl.pallas_call(
        functools.partial(_moe_body, caps, final_norm), out_shape=jax.ShapeDtypeStruct(x2d.shape, _F32),
        grid=(m // tm, n_e),
        in_specs=[row_once, _resident(g.shape), _resident(wr.shape), up, up, down, _resident(gf.shape)],
        out_specs=pl.BlockSpec((tm, D_MODEL), lambda i, e: (i, 0)),
        scratch_shapes=[pltpu.VMEM((tm, D_MODEL), _BF), pltpu.VMEM((tm, LANES), _F32), pltpu.VMEM((tm, LANES), _F32),
                        pltpu.VMEM((LANES, tm), _F32), pltpu.VMEM((1, LANES), _F32)],
        compiler_params=_cparams(2, _MOE_VMEM_LIMIT), name="moe",
    )(x2d, g, wr, wg, wu, wd, gf)


_SC_CONV = 3


def _mix_c_body(per_seq, seq, x_ref, g_ref, win_ref, cw_ref, e_ref, wout_ref, gm_ref, wq_ref, xo_ref, q_ref, p_ref,
                pp_ref):
    tm = x_ref.shape[0]
    x = x_ref[...]
    hn = _rms(x, g_ref[...]).astype(_BF)
    bg = _dot(hn, win_ref[:, 0:D_MODEL])
    p = _dot(hn, win_ref[:, D_MODEL:2 * D_MODEL]) * _dot(hn, win_ref[:, 2 * D_MODEL:3 * D_MODEL])
    if per_seq:
        @pl.when(pl.program_id(1) == 0)
        def _():
            pp_ref[0:SUBLANES, :] = jnp.zeros((SUBLANES, D_MODEL), _F32)
    else:
        pp_ref[0:SUBLANES, :] = jnp.zeros((SUBLANES, D_MODEL), _F32)
    pp_ref[SUBLANES:SUBLANES + tm, :] = p
    p1 = pp_ref[SUBLANES - 1:SUBLANES - 1 + tm, :]
    p2 = pp_ref[SUBLANES - 2:SUBLANES - 2 + tm, :]
    if per_seq:
        pp_ref[0:SUBLANES, :] = p[tm - SUBLANES:tm, :]
    else:
        t, _ = _pos_and_seq(tm, seq)
        p1 = jnp.where(t >= 1, p1, 0.0) + e_ref[0]
        p2 = jnp.where(t >= 2, p2, 0.0) + e_ref[1]
    conv = p2 * cw_ref[0:1, :] + p1 * cw_ref[1:2, :] + p * cw_ref[2:3, :]
    xn = x + _dot((bg * conv).astype(_BF), wout_ref[...])
    xo_ref[...] = xn
    q_ref[...] = _dot(_rms(xn, gm_ref[...]).astype(_BF), wq_ref[...]).astype(q_ref.dtype)
    if per_seq:
        @pl.when(pl.program_id(1) == pl.num_programs(1) - 1)
        def _():
            p_ref[0] = pp_ref[SUBLANES + tm - (_SC_CONV - 1):SUBLANES + tm, :]
    else:
        p_ref[...] = p


def _mix_c(x2d, g, win, cw, e_buf, wout, gm, wq, q_dtype, per_seq, bsz, seq, tm):
    m = x2d.shape[0]
    if per_seq:
        nt = seq // tm
        grid = (bsz, nt)
        imap = lambda b, j: (b * nt + j, 0)
        e_spec = pl.BlockSpec((_SC_CONV - 1, SUBLANES, D_MODEL), lambda b, j: (0, 0, 0))
        p_shape = jax.ShapeDtypeStruct((bsz, _SC_CONV - 1, D_MODEL), _F32)
        p_spec = pl.BlockSpec((1, _SC_CONV - 1, D_MODEL), lambda b, j: (b, 0, 0))
    else:
        grid = (m // tm, 1)
        imap = lambda i, j: (i, 0)
        e_spec = pl.BlockSpec((_SC_CONV - 1, tm, D_MODEL), lambda i, j: (0, i, 0))
        p_shape = jax.ShapeDtypeStruct((m, D_MODEL), _F32)
        p_spec = pl.BlockSpec((tm, D_MODEL), imap)
    row = pl.BlockSpec((tm, D_MODEL), imap)
    outs = (jax.ShapeDtypeStruct((m, D_MODEL), _F32), jax.ShapeDtypeStruct((m, D_MODEL), q_dtype), p_shape)
    return pl.pallas_call(
        functools.partial(_mix_c_body, per_seq, seq), out_shape=outs, grid=grid,
        in_specs=[row, _resident(g.shape), _resident(win.shape), _resident(cw.shape), e_spec, _resident(wout.shape),
                  _resident(gm.shape), _resident(wq.shape)],
        out_specs=(row, row, p_spec),
        scratch_shapes=[pltpu.VMEM((SUBLANES + tm, D_MODEL), _F32)],
        compiler_params=_cparams(2), name="mix_c",
    )(x2d, g, win, cw, e_buf, wout, gm, wq)


def _expand_buf(buf, seq, n_back):
    bsz, km1, ch = buf.shape
    outs = []
    for j in range(1, n_back + 1):
        rows = [buf[:, km1 - j + t] if t < j else jnp.zeros((bsz, ch), buf.dtype) for t in range(seq)]
        outs.append(jnp.stack(rows, axis=1).reshape(bsz * seq, ch))
    return jnp.stack(outs)


def _row(vec, width=None):
    vec = vec.reshape(1, -1).astype(_F32)
    if width is not None and vec.shape[1] < width:
        vec = jnp.pad(vec, ((0, 0), (0, width - vec.shape[1])))
    return vec


def _trunk(x, k_all, v_all, prompt, states, w):
    bsz, seq, _ = x.shape
    m = bsz * seq
    x2d = x.reshape(m, D_MODEL)
    tm = min(_TOKEN_TILE, m)
    dskip = _row(jnp.repeat(w["d_skip"][0], SSM_HEAD_DIM))
    common = (w["conv_w_ssm"][0], _row(w["conv_b_ssm"][0]), _row(w["dt_bias"][0], LANES), _row(w["a_log"][0], LANES),
              dskip, _row(w["g_ssm_norm"][0]))

    act_dtype = _BF if prompt else _F32
    u, v, z, xbc, dt = _inproj_ab(x2d, _row(w["g_mix"][0]), w["w_in_ab"], act_dtype, tm)
    if prompt:
        y_ab, ssm_state, conv_state = _mix_ab_prompt(
            u, v, z, xbc, dt, bsz, seq, w["w_spatial"][0], w["b_spatial"][0][:, :, None], *common)
        v_out = None
    else:
        ws = w["w_spatial"][0][:, :seq, :seq]
        coef = []
        for d in range(seq):
            per_t = [ws[:, t, t - d] if t >= d else jnp.zeros((A_GROUPS,), _F32) for t in range(seq)]
            coef.append(jnp.tile(jnp.repeat(jnp.stack(per_t), A_HEAD, axis=1), (_SEQ_BLK, 1)))
        sb = jnp.tile(jnp.repeat(w["b_spatial"][0][:, :seq].T, A_HEAD, axis=1), (_SEQ_BLK, 1))
        e_buf = _expand_buf(states["ssm_conv"], seq, SSM_CONV - 1)
        y_ab, ssm_state = _mix_ab_sample(u, v, z, xbc, dt, e_buf, states["ssm"].reshape(bsz, SSM_DIM, SSM_STATE), seq,
                                         jnp.stack(coef), sb, *common)
        conv_state = xbc.reshape(bsz, seq, SSM_CONV_DIM)[:, seq - (SSM_CONV - 1):]
        v_out = v.reshape(bsz, seq, D_MODEL)
    x2d, q = _out_q(y_ab, x2d, w["w_out_ab"], _row(w["g_mem"][0]), w["w_mem_q"][0], act_dtype, tm)

    def attend(q, x2d, layer):
        if prompt:
            return _attn_prompt(q, k_all, v_all, x2d, w["w_mem_o"][layer], layer, bsz, seq, tm)
        return _attn_sample(q, k_all, v_all, x2d, w["w_mem_o"][layer], layer, bsz, seq)

    x2d = attend(q, x2d, 0)
    x2d = _ffn(x2d, _row(w["g_ffn"][0]), w["w_ffn_gate"], w["w_ffn_up"], w["w_ffn_down"], min(_MOE_TILE, m))

    if prompt:
        e_buf = jnp.zeros((_SC_CONV - 1, SUBLANES, D_MODEL), _F32)
        x2d, q, sconv_state = _mix_c(x2d, _row(w["g_mix"][1]), w["w_in_c"], w["conv_w_c"][0], e_buf, w["w_out_c"],
                                     _row(w["g_mem"][1]), w["w_mem_q"][1], act_dtype, True, bsz, seq, tm)
    else:
        e_buf = _expand_buf(states["sconv"], seq, _SC_CONV - 1)
        x2d, q, p = _mix_c(x2d, _row(w["g_mix"][1]), w["w_in_c"], w["conv_w_c"][0], e_buf, w["w_out_c"],
                           _row(w["g_mem"][1]), w["w_mem_q"][1], act_dtype, False, bsz, seq, tm)
        sconv_state = p.reshape(bsz, seq, D_MODEL)[:, seq - (_SC_CONV - 1):]
    x2d = attend(q, x2d, 1)
    x2d = _moe(x2d, _row(w["g_ffn"][1]), w["w_router"], w["w_exp_gate"], w["w_exp_up"], w["w_exp_down"],
               _row(w["g_final"]), True, min(_MOE_TILE, m))
    y = x2d.reshape(bsz, seq, D_MODEL)
    ssm_state = ssm_state.reshape(1, bsz, SSM_HEADS, SSM_HEAD_DIM, SSM_STATE)
    return y, ssm_state, conv_state[None], sconv_state[None], v_out


def kernel(x_prompt, x_sample, mem_prompt, state_ssm, state_ssm_conv, state_sconv, cache_mem_k, cache_mem_v, g_mix, g_mem, g_ffn, g_final, w_in_ab, w_spatial, b_spatial, conv_w_ssm, conv_b_ssm, dt_bias, a_log, d_skip, g_ssm_norm, w_out_ab, w_ffn_gate, w_ffn_up, w_ffn_down, w_in_c, conv_w_c, w_out_c, w_router, w_exp_gate, w_exp_up, w_exp_down, w_mem_q, w_mem_k, w_mem_v, w_mem_o):
    depth = w_mem_q.shape[0]
    bp = x_prompt.shape[0]
    bs = x_sample.shape[0]
    bf = lambda a: a.astype(_BF)
    w = dict(
        g_mix=g_mix, g_mem=g_mem, g_ffn=g_ffn, g_final=g_final, w_spatial=w_spatial, b_spatial=b_spatial,
        conv_w_ssm=conv_w_ssm, conv_b_ssm=conv_b_ssm, dt_bias=dt_bias, a_log=a_log, d_skip=d_skip, g_ssm_norm=g_ssm_norm,
        conv_w_c=conv_w_c,
        w_in_ab=jnp.pad(bf(w_in_ab[0]), ((0, 0), (0, LANES - SSM_HEADS))),
        w_out_ab=bf(w_out_ab[0]), w_ffn_gate=bf(w_ffn_gate), w_ffn_up=bf(w_ffn_up), w_ffn_down=bf(w_ffn_down),
        w_in_c=bf(w_in_c[0]), w_out_c=bf(w_out_c[0]),
        w_router=jnp.pad(w_router[0], ((0, 0), (0, LANES - N_EXPERTS))),
        w_exp_gate=bf(w_exp_gate[0]), w_exp_up=bf(w_exp_up[0]), w_exp_down=bf(w_exp_down[0]),
        w_mem_q=bf(w_mem_q), w_mem_o=bf(w_mem_o),
    )
    mem_k_p, mem_v_p, k_bf, v_bf = _mem_kv(mem_prompt, bf(w_mem_k), bf(w_mem_v))
    y_p, ssm_p, ssmconv_p, sconv_p, _ = _trunk(x_prompt, k_bf, v_bf, True, None, w)
    states = dict(ssm=state_ssm[0], ssm_conv=state_ssm_conv[0], sconv=state_sconv[0])
    y_s, ssm_s, ssmconv_s, sconv_s, v_s = _trunk(
        x_sample, cache_mem_k.reshape(depth * bs, N_MEM, MEM_HEADS, MEM_HEAD_DIM),
        cache_mem_v.reshape(depth * bs, N_MEM, MEM_HEADS, MEM_HEAD_DIM), False, states, w)
    return (y_p, y_s, ssm_p, ssm_s, ssmconv_p, ssmconv_s, sconv_p, sconv_s, mem_k_p, mem_v_p, v_s[None])
```

```python
import functools

import jax
import jax.numpy as jnp
from jax import lax
from jax.experimental import pallas as pl
from jax.experimental.pallas import tpu as pltpu

D_MODEL = 1024
EPS = 1e-6
CHUNK = 128
A_GROUPS = 8
A_HEAD = D_MODEL // A_GROUPS
SSM_HEADS = 16
SSM_HEAD_DIM = 64
SSM_GROUPS = 2
SSM_STATE = 128
SSM_CONV = 4
SSM_DIM = D_MODEL
SSM_CONV_DIM = SSM_DIM + 2 * SSM_GROUPS * SSM_STATE
N_MEM = 256
MEM_HEADS = 4
MEM_HEAD_DIM = D_MODEL // MEM_HEADS
N_EXPERTS = 8

LANES = 128
SUBLANES = 8
VMEM_LIMIT = 56 * 1024 * 1024
_MOE_VMEM_LIMIT = 58 * 1024 * 1024
_TOKEN_TILE = 1024

_BF = jnp.bfloat16
_F32 = jnp.float32
_NT = (((1,), (1,)), ((), ()))


def _cparams(n_axes, vmem_limit=VMEM_LIMIT):
    return pltpu.CompilerParams(dimension_semantics=("arbitrary",) * n_axes, vmem_limit_bytes=vmem_limit)


def _resident(shape):
    nd = len(shape)
    return pl.BlockSpec(shape, lambda *_: (0,) * nd, pipeline_mode=pl.Buffered(1))


def _rms(x, g):
    ms = jnp.mean(x * x, axis=-1, keepdims=True)
    return x * lax.rsqrt(ms + EPS) * g


def _silu(x):
    return x * jax.nn.sigmoid(x)


def _dot(a, b):
    return jnp.dot(a, b, preferred_element_type=_F32)


def _softplus(x):
    return jnp.maximum(x, 0.0) + jnp.log(1.0 + jnp.exp(-jnp.abs(x)))


def _split3(x):
    p1 = x.astype(_BF)
    r1 = x - p1.astype(_F32)
    p2 = r1.astype(_BF)
    p3 = (r1 - p2.astype(_F32)).astype(_BF)
    return p1, p2, p3


def _dot_sel_lhs(sel, x):
    p1, p2, p3 = _split3(x)
    return _dot(sel, p1) + _dot(sel, p2) + _dot(sel, p3)


def _dot_sel_rhs(x, sel):
    p1, p2, p3 = _split3(x)
    return _dot(p1, sel) + _dot(p2, sel) + _dot(p3, sel)


def _pos_and_seq(rows, seq):
    r = lax.broadcasted_iota(jnp.int32, (rows, 1), 0)
    if seq & (seq - 1) == 0:
        return r & (seq - 1), r >> (seq.bit_length() - 1)
    return lax.rem(r, seq), lax.div(r, seq)


def _kv_body(m_ref, wk_ref, wv_ref, k_ref, v_ref, kb_ref, vb_ref):
    m = m_ref[0].astype(_BF)
    for w_ref, o_ref, ob_ref in ((wk_ref, k_ref, kb_ref), (wv_ref, v_ref, vb_ref)):
        res = _dot(m, w_ref[0])
        ob_ref[0] = res.astype(_BF)
        for h in range(MEM_HEADS):
            o_ref[0, 0, :, h, :] = res[:, h * MEM_HEAD_DIM:(h + 1) * MEM_HEAD_DIM]


def _mem_kv(mem, wk, wv):
    bsz, n, d = mem.shape
    depth = wk.shape[0]
    out = jax.ShapeDtypeStruct((depth, bsz, n, MEM_HEADS, MEM_HEAD_DIM), _F32)
    out_b = jax.ShapeDtypeStruct((depth * bsz, n, d), _BF)
    wspec = pl.BlockSpec((1, d, d), lambda l, b: (l, 0, 0))
    ospec = pl.BlockSpec((1, 1, n, MEM_HEADS, MEM_HEAD_DIM), lambda l, b: (l, b, 0, 0, 0))
    bspec = pl.BlockSpec((1, n, d), lambda l, b: (l * bsz + b, 0, 0))
    return pl.pallas_call(
        _kv_body, out_shape=(out, out, out_b, out_b), grid=(depth, bsz),
        in_specs=[pl.BlockSpec((1, n, d), lambda l, b: (b, 0, 0)), wspec, wspec],
        out_specs=(ospec, ospec, bspec, bspec), compiler_params=_cparams(2), name="mem_kv",
    )(mem, wk, wv)


_PROJ_CHUNK = 512


def _inproj_ab_body(x_ref, g_ref, w_ref, u_ref, v_ref, z_ref, xbc_ref, dt_ref):
    hn = _rms(x_ref[...], g_ref[...]).astype(_BF)
    col = 0
    for ref, width, act in ((u_ref, D_MODEL, True), (v_ref, D_MODEL, True), (z_ref, D_MODEL, False),
                            (xbc_ref, SSM_CONV_DIM, False)):
        for c in range(width // _PROJ_CHUNK):
            r = _dot(hn, w_ref[:, col:col + _PROJ_CHUNK])
            if act:
                r = jax.nn.gelu(r)
            ref[:, c * _PROJ_CHUNK:(c + 1) * _PROJ_CHUNK] = r.astype(ref.dtype)
            col += _PROJ_CHUNK
    dt_ref[...] = _dot(hn, w_ref[:, col:col + LANES])


def _inproj_ab(x2d, g, w_in, act_dtype, tm):
    m = x2d.shape[0]
    row = lambda width: pl.BlockSpec((tm, width), lambda i: (i, 0))
    outs = (jax.ShapeDtypeStruct((m, D_MODEL), act_dtype),) * 3 + (
        jax.ShapeDtypeStruct((m, SSM_CONV_DIM), _F32), jax.ShapeDtypeStruct((m, LANES), _F32))
    return pl.pallas_call(
        _inproj_ab_body, out_shape=outs, grid=(m // tm,),
        in_specs=[row(D_MODEL), _resident(g.shape), _resident(w_in.shape)],
        out_specs=(row(D_MODEL), row(D_MODEL), row(D_MODEL), row(SSM_CONV_DIM), row(LANES)),
        compiler_params=_cparams(1), name="inproj_ab",
    )(x2d, g, w_in)


_PAIR = 2 * SSM_HEAD_DIM
_HPG = SSM_HEADS // SSM_GROUPS
_GRP_DIM = SSM_DIM // SSM_GROUPS


def _mix_ab_prompt_body(n_sub, u_ref, v_ref, z_ref, xbc_ref, dt_ref, ws_ref, bs_ref, cw_ref, cb_ref, dtb_ref, alog_ref,
                        dskip_ref, gn_ref, y_ref, st_ref, cst_ref, xp_ref, sT_ref, wsb_ref):
    q = CHUNK
    c = pl.program_id(1)
    tri = lax.broadcasted_iota(jnp.int32, (q, q), 0) >= lax.broadcasted_iota(jnp.int32, (q, q), 1)

    @pl.when((pl.program_id(0) == 0) & (c == 0))
    def _():
        for g in range(A_GROUPS):
            wsb_ref[g] = jnp.where(tri, ws_ref[g], 0.0).astype(_BF)

    @pl.when(c == 0)
    def _():
        xp_ref[0, 0:SUBLANES, :] = jnp.zeros((SUBLANES, SSM_CONV_DIM), _F32)
        sT_ref[...] = jnp.zeros_like(sT_ref)

    for sub in range(n_sub):
        _mix_ab_prompt_chunk(sub, n_sub, tri, u_ref, v_ref, z_ref, xbc_ref, dt_ref, bs_ref, cw_ref, cb_ref, dtb_ref,
                             alog_ref, dskip_ref, gn_ref, y_ref, xp_ref, sT_ref, wsb_ref)

    @pl.when(c == pl.num_programs(1) - 1)
    def _():
        for p in range(SSM_HEADS // 2):
            st_ref[0, p * _PAIR:(p + 1) * _PAIR, :] = sT_ref[:, p * _PAIR:(p + 1) * _PAIR].T
        cst_ref[0] = xp_ref[n_sub - 1, SUBLANES + q - (SSM_CONV - 1):SUBLANES + q, :]


def _mix_ab_prompt_chunk(sub, n_sub, tri, u_ref, v_ref, z_ref, xbc_ref, dt_ref, bs_ref, cw_ref, cb_ref, dtb_ref,
                         alog_ref, dskip_ref, gn_ref, y_ref, xp_ref, sT_ref, wsb_ref):
    q = CHUNK
    rows = slice(sub * q, (sub + 1) * q)

    for g in range(A_GROUPS):
        sl = slice(g * A_HEAD, (g + 1) * A_HEAD)
        s = _dot(wsb_ref[g], v_ref[rows, sl].astype(_BF)) + bs_ref[g]
        y_ref[rows, sl] = (u_ref[rows, sl].astype(_F32) * s).astype(y_ref.dtype)

    xbc = xbc_ref[rows, :]
    xp_ref[sub, SUBLANES:SUBLANES + q, :] = xbc
    base = SUBLANES - (SSM_CONV - 1)
    conv = xp_ref[sub, base:base + q, :] * cw_ref[0:1, :]
    for k in range(1, SSM_CONV):
        conv = conv + xp_ref[sub, base + k:base + k + q, :] * cw_ref[k:k + 1, :]
    xp_ref[(sub + 1) % n_sub, 0:SUBLANES, :] = xbc[q - SUBLANES:q, :]
    act = _silu(conv + cb_ref[...])
    xs = act[:, :SSM_DIM]
    bm = act[:, SSM_DIM:SSM_DIM + SSM_GROUPS * SSM_STATE]
    cm = act[:, SSM_DIM + SSM_GROUPS * SSM_STATE:]

    dt = _softplus(dt_ref[rows, :] + dtb_ref[...])
    d_a = dt * (-jnp.exp(alog_ref[...]))
    la = _dot_sel_lhs(tri.astype(_BF), d_a)
    la_t = la.T
    dt_t = dt.T
    e_la = jnp.exp(la)
    last = la[q - 1:q, :]
    wend = jnp.exp(last - la) * dt
    e_last = jnp.exp(last)
    lo = lax.broadcasted_iota(jnp.int32, (q, _PAIR), 1) < SSM_HEAD_DIM
    lo_row = lo[0:1, :]

    for g in range(SSM_GROUPS):
        bm_t = bm[:, g * SSM_STATE:(g + 1) * SSM_STATE].T.astype(_BF)
        cm_g = cm[:, g * SSM_STATE:(g + 1) * SSM_STATE].astype(_BF)
        cb = _dot(cm_g, bm_t)
        ys = []
        for pp in range(_HPG // 2):
            p = g * (_HPG // 2) + pp
            cols = slice(p * _PAIR, (p + 1) * _PAIR)
            wts = []
            for h in (2 * p, 2 * p + 1):
                seg = la[:, h:h + 1] - la_t[h:h + 1, :]
                wts.append(cb * jnp.exp(jnp.where(tri, seg, -jnp.inf)) * dt_t[h:h + 1, :])
            w2 = jnp.concatenate(wts, axis=1).astype(_BF)
            x_pair = xs[:, cols]
            x2 = jnp.concatenate([jnp.where(lo, x_pair, 0.0), jnp.where(lo, 0.0, x_pair)], axis=0).astype(_BF)
            s_t = sT_ref[:, cols]
            e_pair = jnp.where(lo, e_la[:, 2 * p:2 * p + 1], e_la[:, 2 * p + 1:2 * p + 2])
            y_pair = _dot(w2, x2) + _dot(cm_g, s_t.astype(_BF)) * e_pair
            wd_pair = jnp.where(lo, wend[:, 2 * p:2 * p + 1], wend[:, 2 * p + 1:2 * p + 2])
            el_pair = jnp.where(lo_row, e_last[:, 2 * p:2 * p + 1], e_last[:, 2 * p + 1:2 * p + 2])
            sT_ref[:, cols] = s_t * el_pair + _dot(bm_t, (x_pair * wd_pair).astype(_BF))
            ys.append(y_pair + x_pair * dskip_ref[:, cols])
        gcols = slice(g * _GRP_DIM, (g + 1) * _GRP_DIM)
        gated = jnp.concatenate(ys, axis=1) * _silu(z_ref[rows, gcols].astype(_F32))
        y_ref[rows, D_MODEL + g * _GRP_DIM:D_MODEL + (g + 1) * _GRP_DIM] = _rms(gated, gn_ref[:, gcols]).astype(y_ref.dtype)


_MIX_SUB_CHUNKS = 2


def _mix_ab_prompt(u, v, z, xbc, dt, bsz, seq, ws, bs, cw, cb, dtb, alog, dskip, gn):
    n_sub = _MIX_SUB_CHUNKS
    nc = seq // (CHUNK * n_sub)
    row = lambda width: pl.BlockSpec((CHUNK * n_sub, width), lambda b, c: (b * nc + c, 0))
    small = [ws, bs, cw, cb, dtb, alog, dskip, gn]
    outs = (jax.ShapeDtypeStruct((bsz * seq, 2 * D_MODEL), _BF),
            jax.ShapeDtypeStruct((bsz, SSM_DIM, SSM_STATE), _F32),
            jax.ShapeDtypeStruct((bsz, SSM_CONV - 1, SSM_CONV_DIM), _F32))
    return pl.pallas_call(
        functools.partial(_mix_ab_prompt_body, n_sub), out_shape=outs, grid=(bsz, nc),
        in_specs=[row(D_MODEL), row(D_MODEL), row(D_MODEL), row(SSM_CONV_DIM), row(LANES)]
        + [_resident(a.shape) for a in small],
        out_specs=(row(2 * D_MODEL),
                   pl.BlockSpec((1, SSM_DIM, SSM_STATE), lambda b, c: (b, 0, 0)),
                   pl.BlockSpec((1, SSM_CONV - 1, SSM_CONV_DIM), lambda b, c: (b, 0, 0))),
        scratch_shapes=[pltpu.VMEM((n_sub, SUBLANES + CHUNK, SSM_CONV_DIM), _F32),
                        pltpu.VMEM((SSM_STATE, SSM_DIM), _F32), pltpu.VMEM((A_GROUPS, CHUNK, CHUNK), _BF)],
        compiler_params=_cparams(2), name="mix_ab_prompt",
    )(u, v, z, xbc, dt, *small)


_SEQ_BLK = 8


def _mix_ab_sample_body(seq, u_ref, v_ref, z_ref, xbc_ref, dt_ref, e_ref, st_ref, coef_ref, sb_ref, cw_ref, cb_ref,
                        dtb_ref, alog_ref, dskip_ref, gn_ref, y_ref, sto_ref, sh_ref):
    rows = _SEQ_BLK * seq
    t, seq_of_row = _pos_and_seq(rows, seq)
    pad_rows = sh_ref.shape[0]
    sh_ref[...] = jnp.zeros((pad_rows, SSM_CONV_DIM), _F32)

    def shifts(val, deltas):
        width = val.shape[1]
        sh_ref[SUBLANES:SUBLANES + rows, 0:width] = val
        return [sh_ref[SUBLANES - d:SUBLANES - d + rows, 0:width] for d in deltas]

    back = tuple(range(1, seq))

    v = v_ref[...]
    s = coef_ref[0] * v + sb_ref[...]
    for d, vs in zip(back, shifts(v, back)):
        s = s + coef_ref[d] * jnp.where(t >= d, vs, 0.0)
    y_ref[:, 0:D_MODEL] = u_ref[...] * s

    xbc = xbc_ref[...]
    conv = xbc * cw_ref[SSM_CONV - 1:SSM_CONV, :]
    for j, xsft in zip(range(1, SSM_CONV), shifts(xbc, tuple(range(1, SSM_CONV)))):
        conv = conv + (jnp.where(t >= j, xsft, 0.0) + e_ref[j - 1]) * cw_ref[SSM_CONV - 1 - j:SSM_CONV - j, :]
    act = _silu(conv + cb_ref[...])
    xs = act[:, :SSM_DIM]
    bm = act[:, SSM_DIM:SSM_DIM + SSM_GROUPS * SSM_STATE]
    cm = act[:, SSM_DIM + SSM_GROUPS * SSM_STATE:]

    dt = _softplus(dt_ref[...] + dtb_ref[...])
    d_a = dt * (-jnp.exp(alog_ref[...]))
    la = d_a
    for d, sft in zip(back, shifts(d_a, back)):
        la = la + jnp.where(t >= d, sft, 0.0)
    la_back = shifts(la, back)
    dt_back = shifts(dt, back)
    last = jnp.where(t == seq - 1, la, 0.0)
    fwd = shifts(la, tuple(-d for d in back))
    for d, sft in zip(back, fwd):
        last = last + jnp.where(t == seq - 1 - d, sft, 0.0)
    wend = jnp.exp(last - la) * dt
    e_last = jnp.exp(last)
    e_la = jnp.exp(la)

    head_lane = lax.broadcasted_iota(jnp.int32, (rows, LANES), 1)
    bm_back = shifts(bm, back)

    def cb_heads(b_other):
        per_g = [jnp.sum(cm[:, g * SSM_STATE:(g + 1) * SSM_STATE] * b_other[:, g * SSM_STATE:(g + 1) * SSM_STATE],
                         axis=-1, keepdims=True) for g in range(SSM_GROUPS)]
        return jnp.where(head_lane < _HPG, per_g[0], per_g[1])

    gs = [cb_heads(bm) * dt]
    for i, d in enumerate(back):
        gd = cb_heads(bm_back[i]) * jnp.exp(la - la_back[i]) * dt_back[i]
        gs.append(jnp.where(t >= d, gd, 0.0))

    head0 = lax.broadcasted_iota(jnp.int32, (LANES, SSM_DIM), 0) * SSM_HEAD_DIM
    chan = lax.broadcasted_iota(jnp.int32, (LANES, SSM_DIM), 1)
    expand = ((chan >= head0) & (chan < head0 + SSM_HEAD_DIM)).astype(_BF)
    stacked = jnp.concatenate([e_la, wend, e_last] + gs, axis=0)
    full = _dot_sel_rhs(stacked, expand)
    e_la_f, wend_f, e_last_f = full[0:rows], full[rows:2 * rows], full[2 * rows:3 * rows]
    y2 = full[3 * rows:4 * rows] * xs
    for i, (d, xsft) in enumerate(zip(back, shifts(xs, back))):
        y2 = y2 + full[(4 + i) * rows:(5 + i) * rows] * jnp.where(t >= d, xsft, 0.0)

    zpad = jnp.zeros((LANES - rows, SSM_DIM), _F32)
    xw_t = jnp.concatenate([xs * wend_f, zpad], axis=0).T.astype(_BF)
    el_t = jnp.concatenate([e_last_f, zpad], axis=0).T
    row_lane = lax.broadcasted_iota(jnp.int32, (_GRP_DIM, LANES), 1)
    bpad = jnp.zeros((LANES - rows, SSM_STATE), _F32)

    y1 = [jnp.zeros((rows, _GRP_DIM), _F32) for _ in range(SSM_GROUPS)]
    for i in range(_SEQ_BLK):
        mine = seq_of_row == i
        for g in range(SSM_GROUPS):
            gr = slice(g * _GRP_DIM, (g + 1) * _GRP_DIM)
            st = st_ref[i, gr, :]
            b_g = jnp.where(mine, bm[:, g * SSM_STATE:(g + 1) * SSM_STATE], 0.0)
            b_g = jnp.concatenate([b_g, bpad], axis=0).astype(_BF)
            decay = jnp.sum(jnp.where(row_lane == i * seq, el_t[gr, :], 0.0), axis=1, keepdims=True)
            sto_ref[i, gr, :] = st * decay + _dot(xw_t[gr, :], b_g)
            c_g = cm[:, g * SSM_STATE:(g + 1) * SSM_STATE].astype(_BF)
            ch = lax.dot_general(c_g, st.astype(_BF), _NT, preferred_element_type=_F32)
            y1[g] = y1[g] + jnp.where(mine, ch, 0.0)

    for g in range(SSM_GROUPS):
        gr = slice(g * _GRP_DIM, (g + 1) * _GRP_DIM)
        ys = y1[g] * e_la_f[:, gr] + y2[:, gr] + xs[:, gr] * dskip_ref[:, gr]
        gated = ys * _silu(z_ref[:, gr])
        y_ref[:, D_MODEL + g * _GRP_DIM:D_MODEL + (g + 1) * _GRP_DIM] = _rms(gated, gn_ref[:, gr])


def _mix_ab_sample(u, v, z, xbc, dt, e_buf, state, seq, coef, sb, cw, cb, dtb, alog, dskip, gn):
    bsz = state.shape[0]
    rows = _SEQ_BLK * seq
    row = lambda width: pl.BlockSpec((rows, width), lambda i: (i, 0))
    small = [coef, sb, cw, cb, dtb, alog, dskip, gn]
    st_spec = pl.BlockSpec((_SEQ_BLK, SSM_DIM, SSM_STATE), lambda i: (i, 0, 0))
    outs = (jax.ShapeDtypeStruct((bsz * seq, 2 * D_MODEL), _F32), jax.ShapeDtypeStruct(state.shape, _F32))
    return pl.pallas_call(
        functools.partial(_mix_ab_sample_body, seq), out_shape=outs, grid=(bsz // _SEQ_BLK,),
        in_specs=[row(D_MODEL), row(D_MODEL), row(D_MODEL), row(SSM_CONV_DIM), row(LANES),
                  pl.BlockSpec((SSM_CONV - 1, rows, SSM_CONV_DIM), lambda i: (0, i, 0)), st_spec]
        + [_resident(a.shape) for a in small],
        out_specs=(row(2 * D_MODEL), st_spec),
        scratch_shapes=[pltpu.VMEM((rows + 2 * SUBLANES, SSM_CONV_DIM), _F32)],
        compiler_params=_cparams(1), name="mix_ab_sample",
    )(u, v, z, xbc, dt, e_buf, state, *small)


def _out_q_body(a_ref, x_ref, wo_ref, gm_ref, wq_ref, xo_ref, q_ref):
    xn = x_ref[...] + _dot(a_ref[...].astype(_BF), wo_ref[...])
    xo_ref[...] = xn
    q_ref[...] = _dot(_rms(xn, gm_ref[...]).astype(_BF), wq_ref[...]).astype(q_ref.dtype)


def _out_q(a, x2d, wo, gm, wq, q_dtype, tm):
    m = x2d.shape[0]
    row = lambda width: pl.BlockSpec((tm, width), lambda i: (i, 0))
    outs = (jax.ShapeDtypeStruct((m, D_MODEL), _F32), jax.ShapeDtypeStruct((m, D_MODEL), q_dtype))
    return pl.pallas_call(
        _out_q_body, out_shape=outs, grid=(m // tm,),
        in_specs=[row(a.shape[1]), row(D_MODEL), _resident(wo.shape), _resident(gm.shape), _resident(wq.shape)],
        out_specs=(row(D_MODEL), row(D_MODEL)), compiler_params=_cparams(1), name="out_q",
    )(a, x2d, wo, gm, wq)


_ATT_SCALE = MEM_HEAD_DIM ** -0.5


def _attend(q, k, v):
    outs = []
    for h in range(MEM_HEADS):
        sl = slice(h * MEM_HEAD_DIM, (h + 1) * MEM_HEAD_DIM)
        s = lax.dot_general(q[:, sl], k[:, sl], _NT, preferred_element_type=_F32) * _ATT_SCALE
        p = jnp.exp(s - jnp.max(s, axis=-1, keepdims=True))
        p = p / jnp.sum(p, axis=-1, keepdims=True)
        outs.append(_dot(p.astype(_BF), v[:, sl]))
    return jnp.concatenate(outs, axis=1)


def _attn_prompt_body(q_ref, k_ref, v_ref, x_ref, wo_ref, o_ref):
    o = _attend(q_ref[...], k_ref[0], v_ref[0])
    o_ref[...] = x_ref[...] + _dot(o.astype(_BF), wo_ref[...])


def _attn_prompt(q, k, v, x2d, wo, layer, bsz, seq, tq):
    nt = seq // tq
    row = pl.BlockSpec((tq, D_MODEL), lambda b, j: (b * nt + j, 0))
    kv = pl.BlockSpec((1, N_MEM, D_MODEL), lambda b, j: (layer * bsz + b, 0, 0))
    return pl.pallas_call(
        _attn_prompt_body, out_shape=jax.ShapeDtypeStruct(x2d.shape, _F32), grid=(bsz, nt),
        in_specs=[row, kv, kv, row, _resident(wo.shape)], out_specs=row,
        compiler_params=_cparams(2), name="attn_prompt",
    )(q, k, v, x2d, wo)


def _attn_sample_body(seq, q_ref, k_ref, v_ref, x_ref, wo_ref, o_ref, o_scr):
    per_tile = SUBLANES // seq
    _, seq_of_row = _pos_and_seq(SUBLANES, seq)
    n_lt = MEM_HEAD_DIM // LANES
    rows_mh = N_MEM * MEM_HEADS
    col = lax.broadcasted_iota(jnp.int32, (MEM_HEADS * SUBLANES, rows_mh), 1)
    row = lax.broadcasted_iota(jnp.int32, (MEM_HEADS * SUBLANES, rows_mh), 0)
    same_head = (col & (MEM_HEADS - 1)) == (row >> (SUBLANES.bit_length() - 1))
    for tile in range(_SEQ_BLK // per_tile):
        q8 = q_ref[tile * SUBLANES:(tile + 1) * SUBLANES, :].astype(_BF)
        o = None
        for j in range(per_tile):
            i = tile * per_tile + j
            s = None
            for lt in range(n_lt):
                qh = jnp.concatenate([q8[:, h * MEM_HEAD_DIM + lt * LANES:h * MEM_HEAD_DIM + (lt + 1) * LANES]
                                      for h in range(MEM_HEADS)], axis=0)
                k_lt = k_ref[i, :, :, lt * LANES:(lt + 1) * LANES].reshape(rows_mh, LANES).astype(_BF)
                part = lax.dot_general(qh, k_lt, _NT, preferred_element_type=_F32)
                s = part if s is None else s + part
            s = jnp.where(same_head, s * _ATT_SCALE, -jnp.inf)
            p = jnp.exp(s - jnp.max(s, axis=-1, keepdims=True))
            p = (p / jnp.sum(p, axis=-1, keepdims=True)).astype(_BF)
            pieces = [None] * (MEM_HEADS * n_lt)
            for lt in range(n_lt):
                v_lt = v_ref[i, :, :, lt * LANES:(lt + 1) * LANES].reshape(rows_mh, LANES).astype(_BF)
                o_lt = _dot(p, v_lt)
                for h in range(MEM_HEADS):
                    pieces[h * n_lt + lt] = o_lt[h * SUBLANES:(h + 1) * SUBLANES, :]
            oi = jnp.concatenate(pieces, axis=1)
            o = oi if o is None else jnp.where(seq_of_row == j, oi, o)
        o_scr[tile * SUBLANES:(tile + 1) * SUBLANES, :] = o
    o_ref[...] = x_ref[...] + _dot(o_scr[...].astype(_BF), wo_ref[...])


def _attn_sample(q, k, v, x2d, wo, layer, bsz, seq):
    rows = _SEQ_BLK * seq
    nb = bsz // _SEQ_BLK
    row = pl.BlockSpec((rows, D_MODEL), lambda i: (i, 0))
    kv = pl.BlockSpec((_SEQ_BLK, N_MEM, MEM_HEADS, MEM_HEAD_DIM), lambda i: (layer * nb + i, 0, 0, 0))
    return pl.pallas_call(
        functools.partial(_attn_sample_body, seq), out_shape=jax.ShapeDtypeStruct(x2d.shape, _F32), grid=(nb,),
        in_specs=[row, kv, kv, row, _resident(wo.shape)], out_specs=row,
        scratch_shapes=[pltpu.VMEM((rows, D_MODEL), _F32)],
        compiler_params=_cparams(1), name="attn_sample",
    )(q, k, v, x2d, wo)


_FF_GRID_CHUNKS = 2
_FF_SUB = 512
_MOE_TILE = 1024


def _swiglu_chunk(h, wg_ref, wu_ref, wd_ref):
    width = wg_ref.shape[2]
    hids = []
    for lo in range(0, width, _FF_SUB):
        hi = min(lo + _FF_SUB, width)
        hids.append((_silu(_dot(h, wg_ref[0, :, lo:hi])) * _dot(h, wu_ref[0, :, lo:hi])).astype(_BF))
    return _dot(jnp.concatenate(hids, axis=1), wd_ref[0])


def _ffn_body(x_ref, g_ref, wg_ref, wu_ref, wd_ref, o_ref, hf_ref):
    c = pl.program_id(1)

    @pl.when(c == 0)
    def _():
        x = x_ref[...]
        hf_ref[...] = _rms(x, g_ref[...]).astype(_BF)
        o_ref[...] = x

    o_ref[...] += _swiglu_chunk(hf_ref[...], wg_ref, wu_ref, wd_ref)


def _ffn(x2d, g, wg, wu, wd, tm):
    m = x2d.shape[0]
    d_ff = wg.shape[2]
    fc = d_ff // _FF_GRID_CHUNKS
    row = pl.BlockSpec((tm, D_MODEL), lambda i, c: (i, 0))
    up = pl.BlockSpec((1, D_MODEL, fc), lambda i, c: (0, 0, c))
    down = pl.BlockSpec((1, fc, D_MODEL), lambda i, c: (0, c, 0))
    return pl.pallas_call(
        _ffn_body, out_shape=jax.ShapeDtypeStruct(x2d.shape, _F32), grid=(m // tm, _FF_GRID_CHUNKS),
        in_specs=[row, _resident(g.shape), up, up, down], out_specs=row,
        scratch_shapes=[pltpu.VMEM((tm, D_MODEL), _BF)],
        compiler_params=_cparams(2), name="ffn_dense",
    )(x2d, g, wg, wu, wd)


_MOE_PICK_PROB = 2.0 / N_EXPERTS
_MOE_CAP_STEPS = (1.0, 1.25, 1.5)


_BF16_ROWS = 16
_MOE_DENSE_ROWS = 256


def _moe_capacities(tm):
    mean = tm * _MOE_PICK_PROB
    return tuple(int(-(-(mean * s) // _BF16_ROWS) * _BF16_ROWS) for s in _MOE_CAP_STEPS)


def _moe_body(caps, final_norm, x_ref, g_ref, wr_ref, wg_ref, wu_ref, wd_ref, gf_ref, o_ref, hf_ref, comb_ref,
              rank_ref, rank_t_ref, cnt_ref):
    e = pl.program_id(1)
    tm = x_ref.shape[0]
    lane = lax.broadcasted_iota(jnp.int32, (tm, LANES), 1)

    @pl.when(e == 0)
    def _():
        x = x_ref[...]
        hf = _rms(x, g_ref[...])
        hf_hi = hf.astype(_BF)
        hf_ref[...] = hf_hi
        o_ref[...] = x
        hf_lo = (hf - hf_hi.astype(_F32)).astype(_BF)
        wr = wr_ref[...]
        wr_hi = wr.astype(_BF)
        wr_lo = (wr - wr_hi.astype(_F32)).astype(_BF)
        logits = _dot(hf_hi, wr_hi) + (_dot(hf_hi, wr_lo) + _dot(hf_lo, wr_hi))
        lg = jnp.where(lane < N_EXPERTS, logits, -jnp.inf)
        v1 = jnp.max(lg, axis=-1, keepdims=True)
        i1 = jnp.min(jnp.where(lg == v1, lane, LANES), axis=-1, keepdims=True)
        lg2 = jnp.where(lane == i1, -jnp.inf, lg)
        v2 = jnp.max(lg2, axis=-1, keepdims=True)
        i2 = jnp.min(jnp.where(lg2 == v2, lane, LANES), axis=-1, keepdims=True)
        e2 = jnp.exp(v2 - v1)
        comb_ref[...] = jnp.where(lane == i1, 1.0 / (1.0 + e2), 0.0) + jnp.where(lane == i2, e2 / (1.0 + e2), 0.0)
        picked = (lane == i1) | (lane == i2)
        picked_b = picked.astype(_BF)
        blk_r = lax.broadcasted_iota(jnp.int32, (LANES, LANES), 0)
        blk_c = lax.broadcasted_iota(jnp.int32, (LANES, LANES), 1)
        earlier = (blk_c < blk_r).astype(_BF)
        running = jnp.zeros((1, LANES), _F32)
        for b in range(tm // LANES):
            rows = slice(b * LANES, (b + 1) * LANES)
            pb = picked_b[rows, :]
            rank_b = jnp.where(picked[rows, :], _dot(earlier, pb) + running, -1.0)
            rank_ref[rows, :] = rank_b
            rank_t_ref[:, rows] = rank_b.T
            running = running + jnp.sum(pb.astype(_F32), axis=0, keepdims=True)
        cnt_ref[...] = running

    lane_row = lax.broadcasted_iota(jnp.int32, (1, LANES), 1)
    count = jnp.sum(jnp.where(lane_row == e, cnt_ref[...], 0.0)).astype(jnp.int32)

    def compact_path(cap):
        cap_pad = -(-cap // LANES) * LANES
        rank_row = rank_t_ref[pl.ds(e, 1), :]
        slot_col = lax.broadcasted_iota(jnp.int32, (cap, 1), 0).astype(_F32)
        pick = (rank_row == slot_col).astype(_BF)
        xs = _dot(pick, hf_ref[...]).astype(_BF)
        y = _swiglu_chunk(xs, wg_ref, wu_ref, wd_ref).astype(_BF)
        if cap_pad > cap:
            y = jnp.concatenate([y, jnp.zeros((cap_pad - cap, D_MODEL), _BF)], axis=0)
        rank_col = jnp.sum(jnp.where(lane == e, rank_ref[...], 0.0), axis=1, keepdims=True)
        comb_col = jnp.sum(jnp.where(lane == e, comb_ref[...], 0.0), axis=1, keepdims=True)
        slot_row = lax.broadcasted_iota(jnp.int32, (1, cap_pad), 1).astype(_F32)
        put = (rank_col == slot_row).astype(_BF)
        o_ref[...] += comb_col * _dot(put, y)

    lower = 0
    for cap in caps:
        pl.when((count > lower) & (count <= cap))(functools.partial(compact_path, cap))
        lower = cap

    @pl.when(count > caps[-1])
    def _():
        blk = min(tm, _MOE_DENSE_ROWS)
        blk_lane = lax.broadcasted_iota(jnp.int32, (blk, LANES), 1)

        def dense_rows(r, carry):
            rows = pl.ds(pl.multiple_of(r * blk, blk), blk)
            comb_col = jnp.sum(jnp.where(blk_lane == e, comb_ref[rows, :], 0.0), axis=1, keepdims=True)
            o_ref[rows, :] += comb_col * _swiglu_chunk(hf_ref[rows, :], wg_ref, wu_ref, wd_ref)
            return carry

        lax.fori_loop(0, tm // blk, dense_rows, 0)

    if final_norm:
        @pl.when(e == pl.num_programs(1) - 1)
        def _():
            o_ref[...] = _rms(o_ref[...], gf_ref[...])


def _moe(x2d, g, wr, wg, wu, wd, gf, final_norm, tm):
    m = x2d.shape[0]
    n_e, _, d_ff = wg.shape
    assert tm % LANES == 0 and m % tm == 0, (m, tm)
    caps = _moe_capacities(tm)
    row_once = pl.BlockSpec((tm, D_MODEL), lambda i, e: (i, 0), pipeline_mode=pl.Buffered(1))
    up = pl.BlockSpec((1, D_MODEL, d_ff), lambda i, e: (e, 0, 0))
    down = pl.BlockSpec((1, d_ff, D_MODEL), lambda i, e: (e, 0, 0))
    return pl.pallas_call(
        functools.partial(_moe_body, caps, final_norm), out_shape=jax.ShapeDtypeStruct(x2d.shape, _F32),
        grid=(m // tm, n_e),
        in_specs=[row_once, _resident(g.shape), _resident(wr.shape), up, up, down, _resident(gf.shape)],
        out_specs=pl.BlockSpec((tm, D_MODEL), lambda i, e: (i, 0)),
        scratch_shapes=[pltpu.VMEM((tm, D_MODEL), _BF), pltpu.VMEM((tm, LANES), _F32), pltpu.VMEM((tm, LANES), _F32),
                        pltpu.VMEM((LANES, tm), _F32), pltpu.VMEM((1, LANES), _F32)],
        compiler_params=_cparams(2, _MOE_VMEM_LIMIT), name="moe",
    )(x2d, g, wr, wg, wu, wd, gf)


_SC_CONV = 3


def _mix_c_body(per_seq, seq, x_ref, g_ref, win_ref, cw_ref, e_ref, wout_ref, gm_ref, wq_ref, xo_ref, q_ref, p_ref,
                pp_ref):
    tm = x_ref.shape[0]
    x = x_ref[...]
    hn = _rms(x, g_ref[...]).astype(_BF)
    bg = _dot(hn, win_ref[:, 0:D_MODEL])
    p = _dot(hn, win_ref[:, D_MODEL:2 * D_MODEL]) * _dot(hn, win_ref[:, 2 * D_MODEL:3 * D_MODEL])
    if per_seq:
        @pl.when(pl.program_id(1) == 0)
        def _():
            pp_ref[0:SUBLANES, :] = jnp.zeros((SUBLANES, D_MODEL), _F32)
    else:
        pp_ref[0:SUBLANES, :] = jnp.zeros((SUBLANES, D_MODEL), _F32)
    pp_ref[SUBLANES:SUBLANES + tm, :] = p
    p1 = pp_ref[SUBLANES - 1:SUBLANES - 1 + tm, :]
    p2 = pp_ref[SUBLANES - 2:SUBLANES - 2 + tm, :]
    if per_seq:
        pp_ref[0:SUBLANES, :] = p[tm - SUBLANES:tm, :]
    else:
        t, _ = _pos_and_seq(tm, seq)
        p1 = jnp.where(t >= 1, p1, 0.0) + e_ref[0]
        p2 = jnp.where(t >= 2, p2, 0.0) + e_ref[1]
    conv = p2 * cw_ref[0:1, :] + p1 * cw_ref[1:2, :] + p * cw_ref[2:3, :]
    xn = x + _dot((bg * conv).astype(_BF), wout_ref[...])
    xo_ref[...] = xn
    q_ref[...] = _dot(_rms(xn, gm_ref[...]).astype(_BF), wq_ref[...]).astype(q_ref.dtype)
    if per_seq:
        @pl.when(pl.program_id(1) == pl.num_programs(1) - 1)
        def _():
            p_ref[0] = pp_ref[SUBLANES + tm - (_SC_CONV - 1):SUBLANES + tm, :]
    else:
        p_ref[...] = p


def _mix_c(x2d, g, win, cw, e_buf, wout, gm, wq, q_dtype, per_seq, bsz, seq, tm):
    m = x2d.shape[0]
    if per_seq:
        nt = seq // tm
        grid = (bsz, nt)
        imap = lambda b, j: (b * nt + j, 0)
        e_spec = pl.BlockSpec((_SC_CONV - 1, SUBLANES, D_MODEL), lambda b, j: (0, 0, 0))
        p_shape = jax.ShapeDtypeStruct((bsz, _SC_CONV - 1, D_MODEL), _F32)
        p_spec = pl.BlockSpec((1, _SC_CONV - 1, D_MODEL), lambda b, j: (b, 0, 0))
    else:
        grid = (m // tm, 1)
        imap = lambda i, j: (i, 0)
        e_spec = pl.BlockSpec((_SC_CONV - 1, tm, D_MODEL), lambda i, j: (0, i, 0))
        p_shape = jax.ShapeDtypeStruct((m, D_MODEL), _F32)
        p_spec = pl.BlockSpec((tm, D_MODEL), imap)
    row = pl.BlockSpec((tm, D_MODEL), imap)
    outs = (jax.ShapeDtypeStruct((m, D_MODEL), _F32), jax.ShapeDtypeStruct((m, D_MODEL), q_dtype), p_shape)
    return pl.pallas_call(
        functools.partial(_mix_c_body, per_seq, seq), out_shape=outs, grid=grid,
        in_specs=[row, _resident(g.shape), _resident(win.shape), _resident(cw.shape), e_spec, _resident(wout.shape),
                  _resident(gm.shape), _resident(wq.shape)],
        out_specs=(row, row, p_spec),
        scratch_shapes=[pltpu.VMEM((SUBLANES + tm, D_MODEL), _F32)],
        compiler_params=_cparams(2), name="mix_c",
    )(x2d, g, win, cw, e_buf, wout, gm, wq)


def _expand_buf(buf, seq, n_back):
    bsz, km1, ch = buf.shape
    outs = []
    for j in range(1, n_back + 1):
        rows = [buf[:, km1 - j + t] if t < j else jnp.zeros((bsz, ch), buf.dtype) for t in range(seq)]
        outs.append(jnp.stack(rows, axis=1).reshape(bsz * seq, ch))
    return jnp.stack(outs)


def _row(vec, width=None):
    vec = vec.reshape(1, -1).astype(_F32)
    if width is not None and vec.shape[1] < width:
        vec = jnp.pad(vec, ((0, 0), (0, width - vec.shape[1])))
    return vec


def _trunk(x, k_all, v_all, prompt, states, w):
    bsz, seq, _ = x.shape
    m = bsz * seq
    x2d = x.reshape(m, D_MODEL)
    tm = min(_TOKEN_TILE, m)
    dskip = _row(jnp.repeat(w["d_skip"][0], SSM_HEAD_DIM))
    common = (w["conv_w_ssm"][0], _row(w["conv_b_ssm"][0]), _row(w["dt_bias"][0], LANES), _row(w["a_log"][0], LANES),
              dskip, _row(w["g_ssm_norm"][0]))

    act_dtype = _BF if prompt else _F32
    u, v, z, xbc, dt = _inproj_ab(x2d, _row(w["g_mix"][0]), w["w_in_ab"], act_dtype, tm)
    if prompt:
        y_ab, ssm_state, conv_state = _mix_ab_prompt(
            u, v, z, xbc, dt, bsz, seq, w["w_spatial"][0], w["b_spatial"][0][:, :, None], *common)
        v_out = None
    else:
        ws = w["w_spatial"][0][:, :seq, :seq]
        coef = []
        for d in range(seq):
            per_t = [ws[:, t, t - d] if t >= d else jnp.zeros((A_GROUPS,), _F32) for t in range(seq)]
            coef.append(jnp.tile(jnp.repeat(jnp.stack(per_t), A_HEAD, axis=1), (_SEQ_BLK, 1)))
        sb = jnp.tile(jnp.repeat(w["b_spatial"][0][:, :seq].T, A_HEAD, axis=1), (_SEQ_BLK, 1))
        e_buf = _expand_buf(states["ssm_conv"], seq, SSM_CONV - 1)
        y_ab, ssm_state = _mix_ab_sample(u, v, z, xbc, dt, e_buf, states["ssm"].reshape(bsz, SSM_DIM, SSM_STATE), seq,
                                         jnp.stack(coef), sb, *common)
        conv_state = xbc.reshape(bsz, seq, SSM_CONV_DIM)[:, seq - (SSM_CONV - 1):]
        v_out = v.reshape(bsz, seq, D_MODEL)
    x2d, q = _out_q(y_ab, x2d, w["w_out_ab"], _row(w["g_mem"][0]), w["w_mem_q"][0], act_dtype, tm)

    def attend(q, x2d, layer):
        if prompt:
            return _attn_prompt(q, k_all, v_all, x2d, w["w_mem_o"][layer], layer, bsz, seq, tm)
        return _attn_sample(q, k_all, v_all, x2d, w["w_mem_o"][layer], layer, bsz, seq)

    x2d = attend(q, x2d, 0)
    x2d = _ffn(x2d, _row(w["g_ffn"][0]), w["w_ffn_gate"], w["w_ffn_up"], w["w_ffn_down"], min(_MOE_TILE, m))

    if prompt:
        e_buf = jnp.zeros((_SC_CONV - 1, SUBLANES, D_MODEL), _F32)
        x2d, q, sconv_state = _mix_c(x2d, _row(w["g_mix"][1]), w["w_in_c"], w["conv_w_c"][0], e_buf, w["w_out_c"],
                                     _row(w["g_mem"][1]), w["w_mem_q"][1], act_dtype, True, bsz, seq, tm)
    else:
        e_buf = _expand_buf(states["sconv"], seq, _SC_CONV - 1)
        x2d, q, p = _mix_c(x2d, _row(w["g_mix"][1]), w["w_in_c"], w["conv_w_c"][0], e_buf, w["w_out_c"],
                           _row(w["g_mem"][1]), w["w_mem_q"][1], act_dtype, False, bsz, seq, tm)
        sconv_state = p.reshape(bsz, seq, D_MODEL)[:, seq - (_SC_CONV - 1):]
    x2d = attend(q, x2d, 1)
    x2d = _moe(x2d, _row(w["g_ffn"][1]), w["w_router"], w["w_exp_gate"], w["w_exp_up"], w["w_exp_down"],
               _row(w["g_final"]), True, min(_MOE_TILE, m))
    y = x2d.reshape(bsz, seq, D_MODEL)
    ssm_state = ssm_state.reshape(1, bsz, SSM_HEADS, SSM_HEAD_DIM, SSM_STATE)
    return y, ssm_state, conv_state[None], sconv_state[None], v_out


def kernel(x_prompt, x_sample, mem_prompt, state_ssm, state_ssm_conv, state_sconv, cache_mem_k, cache_mem_v, g_mix, g_mem, g_ffn, g_final, w_in_ab, w_spatial, b_spatial, conv_w_ssm, conv_b_ssm, dt_bias, a_log, d_skip, g_ssm_norm, w_out_ab, w_ffn_gate, w_ffn_up, w_ffn_down, w_in_c, conv_w_c, w_out_c, w_router, w_exp_gate, w_exp_up, w_exp_down, w_mem_q, w_mem_k, w_mem_v, w_mem_o):
    depth = w_mem_q.shape[0]
    bp = x_prompt.shape[0]
    bs = x_sample.shape[0]
    bf = lambda a: a.astype(_BF)
    w = dict(
        g_mix=g_mix, g_mem=g_mem, g_ffn=g_ffn, g_final=g_final, w_spatial=w_spatial, b_spatial=b_spatial,
        conv_w_ssm=conv_w_ssm, conv_b_ssm=conv_b_ssm, dt_bias=dt_bias, a_log=a_log, d_skip=d_skip, g_ssm_norm=g_ssm_norm,
        conv_w_c=conv_w_c,
        w_in_ab=jnp.pad(bf(w_in_ab[0]), ((0, 0), (0, LANES - SSM_HEADS))),
        w_out_ab=bf(w_out_ab[0]), w_ffn_gate=bf(w_ffn_gate), w_ffn_up=bf(w_ffn_up), w_ffn_down=bf(w_ffn_down),
        w_in_c=bf(w_in_c[0]), w_out_c=bf(w_out_c[0]),
        w_router=jnp.pad(w_router[0], ((0, 0), (0, LANES - N_EXPERTS))),
        w_exp_gate=bf(w_exp_gate[0]), w_exp_up=bf(w_exp_up[0]), w_exp_down=bf(w_exp_down[0]),
        w_mem_q=bf(w_mem_q), w_mem_o=bf(w_mem_o),
    )
    mem_k_p, mem_v_p, k_bf, v_bf = _mem_kv(mem_prompt, bf(w_mem_k), bf(w_mem_v))
    y_p, ssm_p, ssmconv_p, sconv_p, _ = _trunk(x_prompt, k_bf, v_bf, True, None, w)
    states = dict(ssm=state_ssm[0], ssm_conv=state_ssm_conv[0], sconv=state_sconv[0])
    y_s, ssm_s, ssmconv_s, sconv_s, v_s = _trunk(
        x_sample, cache_mem_k.reshape(depth * bs, N_MEM, MEM_HEADS, MEM_HEAD_DIM),
        cache_mem_v.reshape(depth * bs, N_MEM, MEM_HEADS, MEM_HEAD_DIM), False, states, w)
    return (y_p, y_s, ssm_p, ssm_s, ssmconv_p, ssmconv_s, sconv_p, sconv_s, mem_k_p, mem_v_p, v_s[None])
```

```python
import functools

import jax
import jax.numpy as jnp
from jax import lax
from jax.experimental import pallas as pl
from jax.experimental.pallas import tpu as pltpu

D_MODEL = 1024
EPS = 1e-6
CHUNK = 128
A_GROUPS = 8
A_HEAD = D_MODEL // A_GROUPS
SSM_HEADS = 16
SSM_HEAD_DIM = 64
SSM_GROUPS = 2
SSM_STATE = 128
SSM_CONV = 4
SSM_DIM = D_MODEL
SSM_CONV_DIM = SSM_DIM + 2 * SSM_GROUPS * SSM_STATE
N_MEM = 256
MEM_HEADS = 4
MEM_HEAD_DIM = D_MODEL // MEM_HEADS
N_EXPERTS = 8

LANES = 128
SUBLANES = 8
VMEM_LIMIT = 56 * 1024 * 1024
_MOE_VMEM_LIMIT = 58 * 1024 * 1024
_TOKEN_TILE = 1024

_BF = jnp.bfloat16
_F32 = jnp.float32
_NT = (((1,), (1,)), ((), ()))


def _cparams(n_axes, vmem_limit=VMEM_LIMIT):
    return pltpu.CompilerParams(dimension_semantics=("arbitrary",) * n_axes, vmem_limit_bytes=vmem_limit)


def _resident(shape):
    nd = len(shape)
    return pl.BlockSpec(shape, lambda *_: (0,) * nd, pipeline_mode=pl.Buffered(1))


def _rms(x, g):
    ms = jnp.mean(x * x, axis=-1, keepdims=True)
    return x * lax.rsqrt(ms + EPS) * g


def _silu(x):
    return x * jax.nn.sigmoid(x)


def _dot(a, b):
    return jnp.dot(a, b, preferred_element_type=_F32)


def _softplus(x):
    return jnp.maximum(x, 0.0) + jnp.log(1.0 + jnp.exp(-jnp.abs(x)))


def _split3(x):
    p1 = x.astype(_BF)
    r1 = x - p1.astype(_F32)
    p2 = r1.astype(_BF)
    p3 = (r1 - p2.astype(_F32)).astype(_BF)
    return p1, p2, p3


def _dot_sel_lhs(sel, x):
    p1, p2, p3 = _split3(x)
    return _dot(sel, p1) + _dot(sel, p2) + _dot(sel, p3)


def _dot_sel_rhs(x, sel):
    p1, p2, p3 = _split3(x)
    return _dot(p1, sel) + _dot(p2, sel) + _dot(p3, sel)


def _pos_and_seq(rows, seq):
    r = lax.broadcasted_iota(jnp.int32, (rows, 1), 0)
    if seq & (seq - 1) == 0:
        return r & (seq - 1), r >> (seq.bit_length() - 1)
    return lax.rem(r, seq), lax.div(r, seq)


def _kv_body(m_ref, wk_ref, wv_ref, k_ref, v_ref, kb_ref, vb_ref):
    m = m_ref[0].astype(_BF)
    for w_ref, o_ref, ob_ref in ((wk_ref, k_ref, kb_ref), (wv_ref, v_ref, vb_ref)):
        res = _dot(m, w_ref[0])
        ob_ref[0] = res.astype(_BF)
        for h in range(MEM_HEADS):
            o_ref[0, 0, :, h, :] = res[:, h * MEM_HEAD_DIM:(h + 1) * MEM_HEAD_DIM]


def _mem_kv(mem, wk, wv):
    bsz, n, d = mem.shape
    depth = wk.shape[0]
    out = jax.ShapeDtypeStruct((depth, bsz, n, MEM_HEADS, MEM_HEAD_DIM), _F32)
    out_b = jax.ShapeDtypeStruct((depth * bsz, n, d), _BF)
    wspec = pl.BlockSpec((1, d, d), lambda l, b: (l, 0, 0))
    ospec = pl.BlockSpec((1, 1, n, MEM_HEADS, MEM_HEAD_DIM), lambda l, b: (l, b, 0, 0, 0))
    bspec = pl.BlockSpec((1, n, d), lambda l, b: (l * bsz + b, 0, 0))
    return pl.pallas_call(
        _kv_body, out_shape=(out, out, out_b, out_b), grid=(depth, bsz),
        in_specs=[pl.BlockSpec((1, n, d), lambda l, b: (b, 0, 0)), wspec, wspec],
        out_specs=(ospec, ospec, bspec, bspec), compiler_params=_cparams(2), name="mem_kv",
    )(mem, wk, wv)


_PROJ_CHUNK = 512


def _inproj_ab_body(x_ref, g_ref, w_ref, u_ref, v_ref, z_ref, xbc_ref, dt_ref):
    hn = _rms(x_ref[...], g_ref[...]).astype(_BF)
    col = 0
    for ref, width, act in ((u_ref, D_MODEL, True), (v_ref, D_MODEL, True), (z_ref, D_MODEL, False),
                            (xbc_ref, SSM_CONV_DIM, False)):
        for c in range(width // _PROJ_CHUNK):
            r = _dot(hn, w_ref[:, col:col + _PROJ_CHUNK])
            if act:
                r = jax.nn.gelu(r)
            ref[:, c * _PROJ_CHUNK:(c + 1) * _PROJ_CHUNK] = r.astype(ref.dtype)
            col += _PROJ_CHUNK
    dt_ref[...] = _dot(hn, w_ref[:, col:col + LANES])


def _inproj_ab(x2d, g, w_in, act_dtype, tm):
    m = x2d.shape[0]
    row = lambda width: pl.BlockSpec((tm, width), lambda i: (i, 0))
    outs = (jax.ShapeDtypeStruct((m, D_MODEL), act_dtype),) * 3 + (
        jax.ShapeDtypeStruct((m, SSM_CONV_DIM), _F32), jax.ShapeDtypeStruct((m, LANES), _F32))
    return pl.pallas_call(
        _inproj_ab_body, out_shape=outs, grid=(m // tm,),
        in_specs=[row(D_MODEL), _resident(g.shape), _resident(w_in.shape)],
        out_specs=(row(D_MODEL), row(D_MODEL), row(D_MODEL), row(SSM_CONV_DIM), row(LANES)),
        compiler_params=_cparams(1), name="inproj_ab",
    )(x2d, g, w_in)


_PAIR = 2 * SSM_HEAD_DIM
_HPG = SSM_HEADS // SSM_GROUPS
_GRP_DIM = SSM_DIM // SSM_GROUPS


def _mix_ab_prompt_body(n_sub, u_ref, v_ref, z_ref, xbc_ref, dt_ref, ws_ref, bs_ref, cw_ref, cb_ref, dtb_ref, alog_ref,
                        dskip_ref, gn_ref, y_ref, st_ref, cst_ref, xp_ref, sT_ref, wsb_ref):
    q = CHUNK
    c = pl.program_id(1)
    tri = lax.broadcasted_iota(jnp.int32, (q, q), 0) >= lax.broadcasted_iota(jnp.int32, (q, q), 1)

    @pl.when((pl.program_id(0) == 0) & (c == 0))
    def _():
        for g in range(A_GROUPS):
            wsb_ref[g] = jnp.where(tri, ws_ref[g], 0.0).astype(_BF)

    @pl.when(c == 0)
    def _():
        xp_ref[0, 0:SUBLANES, :] = jnp.zeros((SUBLANES, SSM_CONV_DIM), _F32)
        sT_ref[...] = jnp.zeros_like(sT_ref)

    for sub in range(n_sub):
        _mix_ab_prompt_chunk(sub, n_sub, tri, u_ref, v_ref, z_ref, xbc_ref, dt_ref, bs_ref, cw_ref, cb_ref, dtb_ref,
                             alog_ref, dskip_ref, gn_ref, y_ref, xp_ref, sT_ref, wsb_ref)

    @pl.when(c == pl.num_programs(1) - 1)
    def _():
        for p in range(SSM_HEADS // 2):
            st_ref[0, p * _PAIR:(p + 1) * _PAIR, :] = sT_ref[:, p * _PAIR:(p + 1) * _PAIR].T
        cst_ref[0] = xp_ref[n_sub - 1, SUBLANES + q - (SSM_CONV - 1):SUBLANES + q, :]


def _mix_ab_prompt_chunk(sub, n_sub, tri, u_ref, v_ref, z_ref, xbc_ref, dt_ref, bs_ref, cw_ref, cb_ref, dtb_ref,
                         alog_ref, dskip_ref, gn_ref, y_ref, xp_ref, sT_ref, wsb_ref):
    q = CHUNK
    rows = slice(sub * q, (sub + 1) * q)

    for g in range(A_GROUPS):
        sl = slice(g * A_HEAD, (g + 1) * A_HEAD)
        s = _dot(wsb_ref[g], v_ref[rows, sl].astype(_BF)) + bs_ref[g]
        y_ref[rows, sl] = (u_ref[rows, sl].astype(_F32) * s).astype(y_ref.dtype)

    xbc = xbc_ref[rows, :]
    xp_ref[sub, SUBLANES:SUBLANES + q, :] = xbc
    base = SUBLANES - (SSM_CONV - 1)
    conv = xp_ref[sub, base:base + q, :] * cw_ref[0:1, :]
    for k in range(1, SSM_CONV):
        conv = conv + xp_ref[sub, base + k:base + k + q, :] * cw_ref[k:k + 1, :]
    xp_ref[(sub + 1) % n_sub, 0:SUBLANES, :] = xbc[q - SUBLANES:q, :]
    act = _silu(conv + cb_ref[...])
    xs = act[:, :SSM_DIM]
    bm = act[:, SSM_DIM:SSM_DIM + SSM_GROUPS * SSM_STATE]
    cm = act[:, SSM_DIM + SSM_GROUPS * SSM_STATE:]

    dt = _softplus(dt_ref[rows, :] + dtb_ref[...])
    d_a = dt * (-jnp.exp(alog_ref[...]))
    la = _dot_sel_lhs(tri.astype(_BF), d_a)
    la_t = la.T
    dt_t = dt.T
    e_la = jnp.exp(la)
    last = la[q - 1:q, :]
    wend = jnp.exp(last - la) * dt
    e_last = jnp.exp(last)
    lo = lax.broadcasted_iota(jnp.int32, (q, _PAIR), 1) < SSM_HEAD_DIM
    lo_row = lo[0:1, :]

    for g in range(SSM_GROUPS):
        bm_t = bm[:, g * SSM_STATE:(g + 1) * SSM_STATE].T.astype(_BF)
        cm_g = cm[:, g * SSM_STATE:(g + 1) * SSM_STATE].astype(_BF)
        cb = _dot(cm_g, bm_t)
        ys = []
        for pp in range(_HPG // 2):
            p = g * (_HPG // 2) + pp
            cols = slice(p * _PAIR, (p + 1) * _PAIR)
            wts = []
            for h in (2 * p, 2 * p + 1):
                seg = la[:, h:h + 1] - la_t[h:h + 1, :]
                wts.append(cb * jnp.exp(jnp.where(tri, seg, -jnp.inf)) * dt_t[h:h + 1, :])
            w2 = jnp.concatenate(wts, axis=1).astype(_BF)
            x_pair = xs[:, cols]
            x2 = jnp.concatenate([jnp.where(lo, x_pair, 0.0), jnp.where(lo, 0.0, x_pair)], axis=0).astype(_BF)
            s_t = sT_ref[:, cols]
            e_pair = jnp.where(lo, e_la[:, 2 * p:2 * p + 1], e_la[:, 2 * p + 1:2 * p + 2])
            y_pair = _dot(w2, x2) + _dot(cm_g, s_t.astype(_BF)) * e_pair
            wd_pair = jnp.where(lo, wend[:, 2 * p:2 * p + 1], wend[:, 2 * p + 1:2 * p + 2])
            el_pair = jnp.where(lo_row, e_last[:, 2 * p:2 * p + 1], e_last[:, 2 * p + 1:2 * p + 2])
            sT_ref[:, cols] = s_t * el_pair + _dot(bm_t, (x_pair * wd_pair).astype(_BF))
            ys.append(y_pair + x_pair * dskip_ref[:, cols])
        gcols = slice(g * _GRP_DIM, (g + 1) * _GRP_DIM)
        gated = jnp.concatenate(ys, axis=1) * _silu(z_ref[rows, gcols].astype(_F32))
        y_ref[rows, D_MODEL + g * _GRP_DIM:D_MODEL + (g + 1) * _GRP_DIM] = _rms(gated, gn_ref[:, gcols]).astype(y_ref.dtype)


_MIX_SUB_CHUNKS = 2


def _mix_ab_prompt(u, v, z, xbc, dt, bsz, seq, ws, bs, cw, cb, dtb, alog, dskip, gn):
    n_sub = _MIX_SUB_CHUNKS
    nc = seq // (CHUNK * n_sub)
    row = lambda width: pl.BlockSpec((CHUNK * n_sub, width), lambda b, c: (b * nc + c, 0))
    small = [ws, bs, cw, cb, dtb, alog, dskip, gn]
    outs = (jax.ShapeDtypeStruct((bsz * seq, 2 * D_MODEL), _BF),
            jax.ShapeDtypeStruct((bsz, SSM_DIM, SSM_STATE), _F32),
            jax.ShapeDtypeStruct((bsz, SSM_CONV - 1, SSM_CONV_DIM), _F32))
    return pl.pallas_call(
        functools.partial(_mix_ab_prompt_body, n_sub), out_shape=outs, grid=(bsz, nc),
        in_specs=[row(D_MODEL), row(D_MODEL), row(D_MODEL), row(SSM_CONV_DIM), row(LANES)]
        + [_resident(a.shape) for a in small],
        out_specs=(row(2 * D_MODEL),
                   pl.BlockSpec((1, SSM_DIM, SSM_STATE), lambda b, c: (b, 0, 0)),
                   pl.BlockSpec((1, SSM_CONV - 1, SSM_CONV_DIM), lambda b, c: (b, 0, 0))),
        scratch_shapes=[pltpu.VMEM((n_sub, SUBLANES + CHUNK, SSM_CONV_DIM), _F32),
                        pltpu.VMEM((SSM_STATE, SSM_DIM), _F32), pltpu.VMEM((A_GROUPS, CHUNK, CHUNK), _BF)],
        compiler_params=_cparams(2), name="mix_ab_prompt",
    )(u, v, z, xbc, dt, *small)


_SEQ_BLK = 8


def _mix_ab_sample_body(seq, u_ref, v_ref, z_ref, xbc_ref, dt_ref, e_ref, st_ref, coef_ref, sb_ref, cw_ref, cb_ref,
                        dtb_ref, alog_ref, dskip_ref, gn_ref, y_ref, sto_ref, sh_ref):
    rows = _SEQ_BLK * seq
    t, seq_of_row = _pos_and_seq(rows, seq)
    pad_rows = sh_ref.shape[0]
    sh_ref[...] = jnp.zeros((pad_rows, SSM_CONV_DIM), _F32)

    def shifts(val, deltas):
        width = val.shape[1]
        sh_ref[SUBLANES:SUBLANES + rows, 0:width] = val
        return [sh_ref[SUBLANES - d:SUBLANES - d + rows, 0:width] for d in deltas]

    back = tuple(range(1, seq))

    v = v_ref[...]
    s = coef_ref[0] * v + sb_ref[...]
    for d, vs in zip(back, shifts(v, back)):
        s = s + coef_ref[d] * jnp.where(t >= d, vs, 0.0)
    y_ref[:, 0:D_MODEL] = u_ref[...] * s

    xbc = xbc_ref[...]
    conv = xbc * cw_ref[SSM_CONV - 1:SSM_CONV, :]
    for j, xsft in zip(range(1, SSM_CONV), shifts(xbc, tuple(range(1, SSM_CONV)))):
        conv = conv + (jnp.where(t >= j, xsft, 0.0) + e_ref[j - 1]) * cw_ref[SSM_CONV - 1 - j:SSM_CONV - j, :]
    act = _silu(conv + cb_ref[...])
    xs = act[:, :SSM_DIM]
    bm = act[:, SSM_DIM:SSM_DIM + SSM_GROUPS * SSM_STATE]
    cm = act[:, SSM_DIM + SSM_GROUPS * SSM_STATE:]

    dt = _softplus(dt_ref[...] + dtb_ref[...])
    d_a = dt * (-jnp.exp(alog_ref[...]))
    la = d_a
    for d, sft in zip(back, shifts(d_a, back)):
        la = la + jnp.where(t >= d, sft, 0.0)
    la_back = shifts(la, back)
    dt_back = shifts(dt, back)
    last = jnp.where(t == seq - 1, la, 0.0)
    fwd = shifts(la, tuple(-d for d in back))
    for d, sft in zip(back, fwd):
        last = last + jnp.where(t == seq - 1 - d, sft, 0.0)
    wend = jnp.exp(last - la) * dt
    e_last = jnp.exp(last)
    e_la = jnp.exp(la)

    head_lane = lax.broadcasted_iota(jnp.int32, (rows, LANES), 1)
    bm_back = shifts(bm, back)

    def cb_heads(b_other):
        per_g = [jnp.sum(cm[:, g * SSM_STATE:(g + 1) * SSM_STATE] * b_other[:, g * SSM_STATE:(g + 1) * SSM_STATE],
                         axis=-1, keepdims=True) for g in range(SSM_GROUPS)]
        return jnp.where(head_lane < _HPG, per_g[0], per_g[1])

    gs = [cb_heads(bm) * dt]
    for i, d in enumerate(back):
        gd = cb_heads(bm_back[i]) * jnp.exp(la - la_back[i]) * dt_back[i]
        gs.append(jnp.where(t >= d, gd, 0.0))

    head0 = lax.broadcasted_iota(jnp.int32, (LANES, SSM_DIM), 0) * SSM_HEAD_DIM
    chan = lax.broadcasted_iota(jnp.int32, (LANES, SSM_DIM), 1)
    expand = ((chan >= head0) & (chan < head0 + SSM_HEAD_DIM)).astype(_BF)
    stacked = jnp.concatenate([e_la, wend, e_last] + gs, axis=0)
    full = _dot_sel_rhs(stacked, expand)
    e_la_f, wend_f, e_last_f = full[0:rows], full[rows:2 * rows], full[2 * rows:3 * rows]
    y2 = full[3 * rows:4 * rows] * xs
    for i, (d, xsft) in enumerate(zip(back, shifts(xs, back))):
        y2 = y2 + full[(4 + i) * rows:(5 + i) * rows] * jnp.where(t >= d, xsft, 0.0)

    zpad = jnp.zeros((LANES - rows, SSM_DIM), _F32)
    xw_t = jnp.concatenate([xs * wend_f, zpad], axis=0).T.astype(_BF)
    el_t = jnp.concatenate([e_last_f, zpad], axis=0).T
    row_lane = lax.broadcasted_iota(jnp.int32, (_GRP_DIM, LANES), 1)
    bpad = jnp.zeros((LANES - rows, SSM_STATE), _F32)

    y1 = [jnp.zeros((rows, _GRP_DIM), _F32) for _ in range(SSM_GROUPS)]
    for i in range(_SEQ_BLK):
        mine = seq_of_row == i
        for g in range(SSM_GROUPS):
            gr = slice(g * _GRP_DIM, (g + 1) * _GRP_DIM)
            st = st_ref[i, gr, :]
            b_g = jnp.where(mine, bm[:, g * SSM_STATE:(g + 1) * SSM_STATE], 0.0)
            b_g = jnp.concatenate([b_g, bpad], axis=0).astype(_BF)
            decay = jnp.sum(jnp.where(row_lane == i * seq, el_t[gr, :], 0.0), axis=1, keepdims=True)
            sto_ref[i, gr, :] = st * decay + _dot(xw_t[gr, :], b_g)
            c_g = cm[:, g * SSM_STATE:(g + 1) * SSM_STATE].astype(_BF)
            ch = lax.dot_general(c_g, st.astype(_BF), _NT, preferred_element_type=_F32)
            y1[g] = y1[g] + jnp.where(mine, ch, 0.0)

    for g in range(SSM_GROUPS):
        gr = slice(g * _GRP_DIM, (g + 1) * _GRP_DIM)
        ys = y1[g] * e_la_f[:, gr] + y2[:, gr] + xs[:, gr] * dskip_ref[:, gr]
        gated = ys * _silu(z_ref[:, gr])
        y_ref[:, D_MODEL + g * _GRP_DIM:D_MODEL + (g + 1) * _GRP_DIM] = _rms(gated, gn_ref[:, gr])


def _mix_ab_sample(u, v, z, xbc, dt, e_buf, state, seq, coef, sb, cw, cb, dtb, alog, dskip, gn):
    bsz = state.shape[0]
    rows = _SEQ_BLK * seq
    row = lambda width: pl.BlockSpec((rows, width), lambda i: (i, 0))
    small = [coef, sb, cw, cb, dtb, alog, dskip, gn]
    st_spec = pl.BlockSpec((_SEQ_BLK, SSM_DIM, SSM_STATE), lambda i: (i, 0, 0))
    outs = (jax.ShapeDtypeStruct((bsz * seq, 2 * D_MODEL), _F32), jax.ShapeDtypeStruct(state.shape, _F32))
    return pl.pallas_call(
        functools.partial(_mix_ab_sample_body, seq), out_shape=outs, grid=(bsz // _SEQ_BLK,),
        in_specs=[row(D_MODEL), row(D_MODEL), row(D_MODEL), row(SSM_CONV_DIM), row(LANES),
                  pl.BlockSpec((SSM_CONV - 1, rows, SSM_CONV_DIM), lambda i: (0, i, 0)), st_spec]
        + [_resident(a.shape) for a in small],
        out_specs=(row(2 * D_MODEL), st_spec),
        scratch_shapes=[pltpu.VMEM((rows + 2 * SUBLANES, SSM_CONV_DIM), _F32)],
        compiler_params=_cparams(1), name="mix_ab_sample",
    )(u, v, z, xbc, dt, e_buf, state, *small)


def _out_q_body(a_ref, x_ref, wo_ref, gm_ref, wq_ref, xo_ref, q_ref):
    xn = x_ref[...] + _dot(a_ref[...].astype(_BF), wo_ref[...])
    xo_ref[...] = xn
    q_ref[...] = _dot(_rms(xn, gm_ref[...]).astype(_BF), wq_ref[...]).astype(q_ref.dtype)


def _out_q(a, x2d, wo, gm, wq, q_dtype, tm):
    m = x2d.shape[0]
    row = lambda width: pl.BlockSpec((tm, width), lambda i: (i, 0))
    outs = (jax.ShapeDtypeStruct((m, D_MODEL), _F32), jax.ShapeDtypeStruct((m, D_MODEL), q_dtype))
    return pl.pallas_call(
        _out_q_body, out_shape=outs, grid=(m // tm,),
        in_specs=[row(a.shape[1]), row(D_MODEL), _resident(wo.shape), _resident(gm.shape), _resident(wq.shape)],
        out_specs=(row(D_MODEL), row(D_MODEL)), compiler_params=_cparams(1), name="out_q",
    )(a, x2d, wo, gm, wq)


_ATT_SCALE = MEM_HEAD_DIM ** -0.5


def _attend(q, k, v):
    outs = []
    for h in range(MEM_HEADS):
        sl = slice(h * MEM_HEAD_DIM, (h + 1) * MEM_HEAD_DIM)
        s = lax.dot_general(q[:, sl], k[:, sl], _NT, preferred_element_type=_F32) * _ATT_SCALE
        p = jnp.exp(s - jnp.max(s, axis=-1, keepdims=True))
        p = p / jnp.sum(p, axis=-1, keepdims=True)
        outs.append(_dot(p.astype(_BF), v[:, sl]))
    return jnp.concatenate(outs, axis=1)


def _attn_prompt_body(q_ref, k_ref, v_ref, x_ref, wo_ref, o_ref):
    o = _attend(q_ref[...], k_ref[0], v_ref[0])
    o_ref[...] = x_ref[...] + _dot(o.astype(_BF), wo_ref[...])


def _attn_prompt(q, k, v, x2d, wo, layer, bsz, seq, tq):
    nt = seq // tq
    row = pl.BlockSpec((tq, D_MODEL), lambda b, j: (b * nt + j, 0))
    kv = pl.BlockSpec((1, N_MEM, D_MODEL), lambda b, j: (layer * bsz + b, 0, 0))
    return pl.pallas_call(
        _attn_prompt_body, out_shape=jax.ShapeDtypeStruct(x2d.shape, _F32), grid=(bsz, nt),
        in_specs=[row, kv, kv, row, _resident(wo.shape)], out_specs=row,
        compiler_params=_cparams(2), name="attn_prompt",
    )(q, k, v, x2d, wo)


def _attn_sample_body(seq, q_ref, k_ref, v_ref, x_ref, wo_ref, o_ref, o_scr):
    per_tile = SUBLANES // seq
    _, seq_of_row = _pos_and_seq(SUBLANES, seq)
    n_lt = MEM_HEAD_DIM // LANES
    rows_mh = N_MEM * MEM_HEADS
    col = lax.broadcasted_iota(jnp.int32, (MEM_HEADS * SUBLANES, rows_mh), 1)
    row = lax.broadcasted_iota(jnp.int32, (MEM_HEADS * SUBLANES, rows_mh), 0)
    same_head = (col & (MEM_HEADS - 1)) == (row >> (SUBLANES.bit_length() - 1))
    for tile in range(_SEQ_BLK // per_tile):
        q8 = q_ref[tile * SUBLANES:(tile + 1) * SUBLANES, :].astype(_BF)
        o = None
        for j in range(per_tile):
            i = tile * per_tile + j
            s = None
            for lt in range(n_lt):
                qh = jnp.concatenate([q8[:, h * MEM_HEAD_DIM + lt * LANES:h * MEM_HEAD_DIM + (lt + 1) * LANES]
                                      for h in range(MEM_HEADS)], axis=0)
                k_lt = k_ref[i, :, :, lt * LANES:(lt + 1) * LANES].reshape(rows_mh, LANES).astype(_BF)
                part = lax.dot_general(qh, k_lt, _NT, preferred_element_type=_F32)
                s = part if s is None else s + part
            s = jnp.where(same_head, s * _ATT_SCALE, -jnp.inf)
            p = jnp.exp(s - jnp.max(s, axis=-1, keepdims=True))
            p = (p / jnp.sum(p, axis=-1, keepdims=True)).astype(_BF)
            pieces = [None] * (MEM_HEADS * n_lt)
            for lt in range(n_lt):
                v_lt = v_ref[i, :, :, lt * LANES:(lt + 1) * LANES].reshape(rows_mh, LANES).astype(_BF)
                o_lt = _dot(p, v_lt)
                for h in range(MEM_HEADS):
                    pieces[h * n_lt + lt] = o_lt[h * SUBLANES:(h + 1) * SUBLANES, :]
            oi = jnp.concatenate(pieces, axis=1)
            o = oi if o is None else jnp.where(seq_of_row == j, oi, o)
        o_scr[tile * SUBLANES:(tile + 1) * SUBLANES, :] = o
    o_ref[...] = x_ref[...] + _dot(o_scr[...].astype(_BF), wo_ref[...])


def _attn_sample(q, k, v, x2d, wo, layer, bsz, seq):
    rows = _SEQ_BLK * seq
    nb = bsz // _SEQ_BLK
    row = pl.BlockSpec((rows, D_MODEL), lambda i: (i, 0))
    kv = pl.BlockSpec((_SEQ_BLK, N_MEM, MEM_HEADS, MEM_HEAD_DIM), lambda i: (layer * nb + i, 0, 0, 0))
    return pl.pallas_call(
        functools.partial(_attn_sample_body, seq), out_shape=jax.ShapeDtypeStruct(x2d.shape, _F32), grid=(nb,),
        in_specs=[row, kv, kv, row, _resident(wo.shape)], out_specs=row,
        scratch_shapes=[pltpu.VMEM((rows, D_MODEL), _F32)],
        compiler_params=_cparams(1), name="attn_sample",
    )(q, k, v, x2d, wo)


_FF_SUB = 512
_MOE_TILE = 1024


def _swiglu_chunk(h, wg_ref, wu_ref, wd_ref):
    width = wg_ref.shape[2]
    hids = []
    for lo in range(0, width, _FF_SUB):
        hi = min(lo + _FF_SUB, width)
        hids.append((_silu(_dot(h, wg_ref[0, :, lo:hi])) * _dot(h, wu_ref[0, :, lo:hi])).astype(_BF))
    return _dot(jnp.concatenate(hids, axis=1), wd_ref[0])


def _ffn_body(x_ref, g_ref, wg_ref, wu_ref, wd_ref, o_ref):
    x = x_ref[...]
    o_ref[...] = x + _swiglu_chunk(_rms(x, g_ref[...]).astype(_BF), wg_ref, wu_ref, wd_ref)


def _ffn(x2d, g, wg, wu, wd, tm):
    m = x2d.shape[0]
    row = pl.BlockSpec((tm, D_MODEL), lambda i: (i, 0))
    return pl.pallas_call(
        _ffn_body, out_shape=jax.ShapeDtypeStruct(x2d.shape, _F32), grid=(m // tm,),
        in_specs=[row, _resident(g.shape), _resident(wg.shape), _resident(wu.shape), _resident(wd.shape)],
        out_specs=row, compiler_params=_cparams(1), name="ffn_dense",
    )(x2d, g, wg, wu, wd)


_MOE_PICK_PROB = 2.0 / N_EXPERTS
_MOE_CAP_STEPS = (1.0, 1.25, 1.5)


_BF16_ROWS = 16
_MOE_DENSE_ROWS = 256


def _moe_capacities(tm):
    mean = tm * _MOE_PICK_PROB
    return tuple(int(-(-(mean * s) // _BF16_ROWS) * _BF16_ROWS) for s in _MOE_CAP_STEPS)


def _moe_body(caps, final_norm, x_ref, g_ref, wr_ref, wg_ref, wu_ref, wd_ref, gf_ref, o_ref, hf_ref, comb_ref,
              rank_ref, rank_t_ref, cnt_ref):
    e = pl.program_id(1)
    tm = x_ref.shape[0]
    lane = lax.broadcasted_iota(jnp.int32, (tm, LANES), 1)

    @pl.when(e == 0)
    def _():
        x = x_ref[...]
        hf = _rms(x, g_ref[...])
        hf_hi = hf.astype(_BF)
        hf_ref[...] = hf_hi
        o_ref[...] = x
        hf_lo = (hf - hf_hi.astype(_F32)).astype(_BF)
        wr = wr_ref[...]
        wr_hi = wr.astype(_BF)
        wr_lo = (wr - wr_hi.astype(_F32)).astype(_BF)
        logits = _dot(hf_hi, wr_hi) + (_dot(hf_hi, wr_lo) + _dot(hf_lo, wr_hi))
        lg = jnp.where(lane < N_EXPERTS, logits, -jnp.inf)
        v1 = jnp.max(lg, axis=-1, keepdims=True)
        i1 = jnp.min(jnp.where(lg == v1, lane, LANES), axis=-1, keepdims=True)
        lg2 = jnp.where(lane == i1, -jnp.inf, lg)
        v2 = jnp.max(lg2, axis=-1, keepdims=True)
        i2 = jnp.min(jnp.where(lg2 == v2, lane, LANES), axis=-1, keepdims=True)
        e2 = jnp.exp(v2 - v1)
        comb_ref[...] = jnp.where(lane == i1, 1.0 / (1.0 + e2), 0.0) + jnp.where(lane == i2, e2 / (1.0 + e2), 0.0)
        picked = (lane == i1) | (lane == i2)
        picked_b = picked.astype(_BF)
        blk_r = lax.broadcasted_iota(jnp.int32, (LANES, LANES), 0)
        blk_c = lax.broadcasted_iota(jnp.int32, (LANES, LANES), 1)
        earlier = (blk_c < blk_r).astype(_BF)
        running = jnp.zeros((1, LANES), _F32)
        for b in range(tm // LANES):
            rows = slice(b * LANES, (b + 1) * LANES)
            pb = picked_b[rows, :]
            rank_b = jnp.where(picked[rows, :], _dot(earlier, pb) + running, -1.0)
            rank_ref[rows, :] = rank_b
            rank_t_ref[:, rows] = rank_b.T
            running = running + jnp.sum(pb.astype(_F32), axis=0, keepdims=True)
        cnt_ref[...] = running

    lane_row = lax.broadcasted_iota(jnp.int32, (1, LANES), 1)
    count = jnp.sum(jnp.where(lane_row == e, cnt_ref[...], 0.0)).astype(jnp.int32)

    def compact_path(cap):
        cap_pad = -(-cap // LANES) * LANES
        rank_row = rank_t_ref[pl.ds(e, 1), :]
        slot_col = lax.broadcasted_iota(jnp.int32, (cap, 1), 0).astype(_F32)
        pick = (rank_row == slot_col).astype(_BF)
        xs = _dot(pick, hf_ref[...]).astype(_BF)
        y = _swiglu_chunk(xs, wg_ref, wu_ref, wd_ref).astype(_BF)
        if cap_pad > cap:
            y = jnp.concatenate([y, jnp.zeros((cap_pad - cap, D_MODEL), _BF)], axis=0)
        rank_col = jnp.sum(jnp.where(lane == e, rank_ref[...], 0.0), axis=1, keepdims=True)
        comb_col = jnp.sum(jnp.where(lane == e, comb_ref[...], 0.0), axis=1, keepdims=True)
        slot_row = lax.broadcasted_iota(jnp.int32, (1, cap_pad), 1).astype(_F32)
        put = (rank_col == slot_row).astype(_BF)
        o_ref[...] += comb_col * _dot(put, y)

    lower = 0
    for cap in caps:
        pl.when((count > lower) & (count <= cap))(functools.partial(compact_path, cap))
        lower = cap

    @pl.when(count > caps[-1])
    def _():
        blk = min(tm, _MOE_DENSE_ROWS)
        blk_lane = lax.broadcasted_iota(jnp.int32, (blk, LANES), 1)

        def dense_rows(r, carry):
            rows = pl.ds(pl.multiple_of(r * blk, blk), blk)
            comb_col = jnp.sum(jnp.where(blk_lane == e, comb_ref[rows, :], 0.0), axis=1, keepdims=True)
            o_ref[rows, :] += comb_col * _swiglu_chunk(hf_ref[rows, :], wg_ref, wu_ref, wd_ref)
            return carry

        lax.fori_loop(0, tm // blk, dense_rows, 0)

    if final_norm:
        @pl.when(e == pl.num_programs(1) - 1)
        def _():
            o_ref[...] = _rms(o_ref[...], gf_ref[...])


def _moe(x2d, g, wr, wg, wu, wd, gf, final_norm, tm):
    m = x2d.shape[0]
    n_e, _, d_ff = wg.shape
    assert tm % LANES == 0 and m % tm == 0, (m, tm)
    caps = _moe_capacities(tm)
    row_once = pl.BlockSpec((tm, D_MODEL), lambda i, e: (i, 0), pipeline_mode=pl.Buffered(1))
    up = pl.BlockSpec((1, D_MODEL, d_ff), lambda i, e: (e, 0, 0))
    down = pl.BlockSpec((1, d_ff, D_MODEL), lambda i, e: (e, 0, 0))
    return pl.pallas_call(
        functools.partial(_moe_body, caps, final_norm), out_shape=jax.ShapeDtypeStruct(x2d.shape, _F32),
        grid=(m // tm, n_e),
        in_specs=[row_once, _resident(g.shape), _resident(wr.shape), up, up, down, _resident(gf.shape)],
        out_specs=pl.BlockSpec((tm, D_MODEL), lambda i, e: (i, 0)),
        scratch_shapes=[pltpu.VMEM((tm, D_MODEL), _BF), pltpu.VMEM((tm, LANES), _F32), pltpu.VMEM((tm, LANES), _F32),
                        pltpu.VMEM((LANES, tm), _F32), pltpu.VMEM((1, LANES), _F32)],
        compiler_params=_cparams(2, _MOE_VMEM_LIMIT), name="moe",
    )(x2d, g, wr, wg, wu, wd, gf)


_SC_CONV = 3


def _mix_c_body(per_seq, seq, x_ref, g_ref, win_ref, cw_ref, e_ref, wout_ref, gm_ref, wq_ref, xo_ref, q_ref, p_ref,
                pp_ref):
    tm = x_ref.shape[0]
    x = x_ref[...]
    hn = _rms(x, g_ref[...]).astype(_BF)
    bg = _dot(hn, win_ref[:, 0:D_MODEL])
    p = _dot(hn, win_ref[:, D_MODEL:2 * D_MODEL]) * _dot(hn, win_ref[:, 2 * D_MODEL:3 * D_MODEL])
    if per_seq:
        @pl.when(pl.program_id(1) == 0)
        def _():
            pp_ref[0:SUBLANES, :] = jnp.zeros((SUBLANES, D_MODEL), _F32)
    else:
        pp_ref[0:SUBLANES, :] = jnp.zeros((SUBLANES, D_MODEL), _F32)
    pp_ref[SUBLANES:SUBLANES + tm, :] = p
    p1 = pp_ref[SUBLANES - 1:SUBLANES - 1 + tm, :]
    p2 = pp_ref[SUBLANES - 2:SUBLANES - 2 + tm, :]
    if per_seq:
        pp_ref[0:SUBLANES, :] = p[tm - SUBLANES:tm, :]
    else:
        t, _ = _pos_and_seq(tm, seq)
        p1 = jnp.where(t >= 1, p1, 0.0) + e_ref[0]
        p2 = jnp.where(t >= 2, p2, 0.0) + e_ref[1]
    conv = p2 * cw_ref[0:1, :] + p1 * cw_ref[1:2, :] + p * cw_ref[2:3, :]
    xn = x + _dot((bg * conv).astype(_BF), wout_ref[...])
    xo_ref[...] = xn
    q_ref[...] = _dot(_rms(xn, gm_ref[...]).astype(_BF), wq_ref[...]).astype(q_ref.dtype)
    if per_seq:
        @pl.when(pl.program_id(1) == pl.num_programs(1) - 1)
        def _():
            p_ref[0] = pp_ref[SUBLANES + tm - (_SC_CONV - 1):SUBLANES + tm, :]
    else:
        p_ref[...] = p


def _mix_c(x2d, g, win, cw, e_buf, wout, gm, wq, q_dtype, per_seq, bsz, seq, tm):
    m = x2d.shape[0]
    if per_seq:
        nt = seq // tm
        grid = (bsz, nt)
        imap = lambda b, j: (b * nt + j, 0)
        e_spec = pl.BlockSpec((_SC_CONV - 1, SUBLANES, D_MODEL), lambda b, j: (0, 0, 0))
        p_shape = jax.ShapeDtypeStruct((bsz, _SC_CONV - 1, D_MODEL), _F32)
        p_spec = pl.BlockSpec((1, _SC_CONV - 1, D_MODEL), lambda b, j: (b, 0, 0))
    else:
        grid = (m // tm, 1)
        imap = lambda i, j: (i, 0)
        e_spec = pl.BlockSpec((_SC_CONV - 1, tm, D_MODEL), lambda i, j: (0, i, 0))
        p_shape = jax.ShapeDtypeStruct((m, D_MODEL), _F32)
        p_spec = pl.BlockSpec((tm, D_MODEL), imap)
    row = pl.BlockSpec((tm, D_MODEL), imap)
    outs = (jax.ShapeDtypeStruct((m, D_MODEL), _F32), jax.ShapeDtypeStruct((m, D_MODEL), q_dtype), p_shape)
    return pl.pallas_call(
        functools.partial(_mix_c_body, per_seq, seq), out_shape=outs, grid=grid,
        in_specs=[row, _resident(g.shape), _resident(win.shape), _resident(cw.shape), e_spec, _resident(wout.shape),
                  _resident(gm.shape), _resident(wq.shape)],
        out_specs=(row, row, p_spec),
        scratch_shapes=[pltpu.VMEM((SUBLANES + tm, D_MODEL), _F32)],
        compiler_params=_cparams(2), name="mix_c",
    )(x2d, g, win, cw, e_buf, wout, gm, wq)


def _expand_buf(buf, seq, n_back):
    bsz, km1, ch = buf.shape
    outs = []
    for j in range(1, n_back + 1):
        rows = [buf[:, km1 - j + t] if t < j else jnp.zeros((bsz, ch), buf.dtype) for t in range(seq)]
        outs.append(jnp.stack(rows, axis=1).reshape(bsz * seq, ch))
    return jnp.stack(outs)


def _row(vec, width=None):
    vec = vec.reshape(1, -1).astype(_F32)
    if width is not None and vec.shape[1] < width:
        vec = jnp.pad(vec, ((0, 0), (0, width - vec.shape[1])))
    return vec


def _trunk(x, k_all, v_all, prompt, states, w):
    bsz, seq, _ = x.shape
    m = bsz * seq
    x2d = x.reshape(m, D_MODEL)
    tm = min(_TOKEN_TILE, m)
    dskip = _row(jnp.repeat(w["d_skip"][0], SSM_HEAD_DIM))
    common = (w["conv_w_ssm"][0], _row(w["conv_b_ssm"][0]), _row(w["dt_bias"][0], LANES), _row(w["a_log"][0], LANES),
              dskip, _row(w["g_ssm_norm"][0]))

    act_dtype = _BF if prompt else _F32
    u, v, z, xbc, dt = _inproj_ab(x2d, _row(w["g_mix"][0]), w["w_in_ab"], act_dtype, tm)
    if prompt:
        y_ab, ssm_state, conv_state = _mix_ab_prompt(
            u, v, z, xbc, dt, bsz, seq, w["w_spatial"][0], w["b_spatial"][0][:, :, None], *common)
        v_out = None
    else:
        ws = w["w_spatial"][0][:, :seq, :seq]
        coef = []
        for d in range(seq):
            per_t = [ws[:, t, t - d] if t >= d else jnp.zeros((A_GROUPS,), _F32) for t in range(seq)]
            coef.append(jnp.tile(jnp.repeat(jnp.stack(per_t), A_HEAD, axis=1), (_SEQ_BLK, 1)))
        sb = jnp.tile(jnp.repeat(w["b_spatial"][0][:, :seq].T, A_HEAD, axis=1), (_SEQ_BLK, 1))
        e_buf = _expand_buf(states["ssm_conv"], seq, SSM_CONV - 1)
        y_ab, ssm_state = _mix_ab_sample(u, v, z, xbc, dt, e_buf, states["ssm"].reshape(bsz, SSM_DIM, SSM_STATE), seq,
                                         jnp.stack(coef), sb, *common)
        conv_state = xbc.reshape(bsz, seq, SSM_CONV_DIM)[:, seq - (SSM_CONV - 1):]
        v_out = v.reshape(bsz, seq, D_MODEL)
    x2d, q = _out_q(y_ab, x2d, w["w_out_ab"], _row(w["g_mem"][0]), w["w_mem_q"][0], act_dtype, tm)

    def attend(q, x2d, layer):
        if prompt:
            return _attn_prompt(q, k_all, v_all, x2d, w["w_mem_o"][layer], layer, bsz, seq, tm)
        return _attn_sample(q, k_all, v_all, x2d, w["w_mem_o"][layer], layer, bsz, seq)

    x2d = attend(q, x2d, 0)
    x2d = _ffn(x2d, _row(w["g_ffn"][0]), w["w_ffn_gate"], w["w_ffn_up"], w["w_ffn_down"], min(_MOE_TILE, m))

    if prompt:
        e_buf = jnp.zeros((_SC_CONV - 1, SUBLANES, D_MODEL), _F32)
        x2d, q, sconv_state = _mix_c(x2d, _row(w["g_mix"][1]), w["w_in_c"], w["conv_w_c"][0], e_buf, w["w_out_c"],
                                     _row(w["g_mem"][1]), w["w_mem_q"][1], act_dtype, True, bsz, seq, tm)
    else:
        e_buf = _expand_buf(states["sconv"], seq, _SC_CONV - 1)
        x2d, q, p = _mix_c(x2d, _row(w["g_mix"][1]), w["w_in_c"], w["conv_w_c"][0], e_buf, w["w_out_c"],
                           _row(w["g_mem"][1]), w["w_mem_q"][1], act_dtype, False, bsz, seq, tm)
        sconv_state = p.reshape(bsz, seq, D_MODEL)[:, seq - (_SC_CONV - 1):]
    x2d = attend(q, x2d, 1)
    x2d = _moe(x2d, _row(w["g_ffn"][1]), w["w_router"], w["w_exp_gate"], w["w_exp_up"], w["w_exp_down"],
               _row(w["g_final"]), True, min(_MOE_TILE, m))
    y = x2d.reshape(bsz, seq, D_MODEL)
    ssm_state = ssm_state.reshape(1, bsz, SSM_HEADS, SSM_HEAD_DIM, SSM_STATE)
    return y, ssm_state, conv_state[None], sconv_state[None], v_out


def kernel(x_prompt, x_sample, mem_prompt, state_ssm, state_ssm_conv, state_sconv, cache_mem_k, cache_mem_v, g_mix, g_mem, g_ffn, g_final, w_in_ab, w_spatial, b_spatial, conv_w_ssm, conv_b_ssm, dt_bias, a_log, d_skip, g_ssm_norm, w_out_ab, w_ffn_gate, w_ffn_up, w_ffn_down, w_in_c, conv_w_c, w_out_c, w_router, w_exp_gate, w_exp_up, w_exp_down, w_mem_q, w_mem_k, w_mem_v, w_mem_o):
    depth = w_mem_q.shape[0]
    bp = x_prompt.shape[0]
    bs = x_sample.shape[0]
    bf = lambda a: a.astype(_BF)
    w = dict(
        g_mix=g_mix, g_mem=g_mem, g_ffn=g_ffn, g_final=g_final, w_spatial=w_spatial, b_spatial=b_spatial,
        conv_w_ssm=conv_w_ssm, conv_b_ssm=conv_b_ssm, dt_bias=dt_bias, a_log=a_log, d_skip=d_skip, g_ssm_norm=g_ssm_norm,
        conv_w_c=conv_w_c,
        w_in_ab=jnp.pad(bf(w_in_ab[0]), ((0, 0), (0, LANES - SSM_HEADS))),
        w_out_ab=bf(w_out_ab[0]), w_ffn_gate=bf(w_ffn_gate), w_ffn_up=bf(w_ffn_up), w_ffn_down=bf(w_ffn_down),
        w_in_c=bf(w_in_c[0]), w_out_c=bf(w_out_c[0]),
        w_router=jnp.pad(w_router[0], ((0, 0), (0, LANES - N_EXPERTS))),
        w_exp_gate=bf(w_exp_gate[0]), w_exp_up=bf(w_exp_up[0]), w_exp_down=bf(w_exp_down[0]),
        w_mem_q=bf(w_mem_q), w_mem_o=bf(w_mem_o),
    )
    mem_k_p, mem_v_p, k_bf, v_bf = _mem_kv(mem_prompt, bf(w_mem_k), bf(w_mem_v))
    y_p, ssm_p, ssmconv_p, sconv_p, _ = _trunk(x_prompt, k_bf, v_bf, True, None, w)
    states = dict(ssm=state_ssm[0], ssm_conv=state_ssm_conv[0], sconv=state_sconv[0])
    y_s, ssm_s, ssmconv_s, sconv_s, v_s = _trunk(
        x_sample, cache_mem_k.reshape(depth * bs, N_MEM, MEM_HEADS, MEM_HEAD_DIM),
        cache_mem_v.reshape(depth * bs, N_MEM, MEM_HEADS, MEM_HEAD_DIM), False, states, w)
    return (y_p, y_s, ssm_p, ssm_s, ssmconv_p, ssmconv_s, sconv_p, sconv_s, mem_k_p, mem_v_p, v_s[None])
```

```python
import functools

import jax
import jax.numpy as jnp
from jax import lax
from jax.experimental import pallas as pl
from jax.experimental.pallas import tpu as pltpu

D_MODEL = 1024
EPS = 1e-6
CHUNK = 128
A_GROUPS = 8
A_HEAD = D_MODEL // A_GROUPS
SSM_HEADS = 16
SSM_HEAD_DIM = 64
SSM_GROUPS = 2
SSM_STATE = 128
SSM_CONV = 4
SSM_DIM = D_MODEL
SSM_CONV_DIM = SSM_DIM + 2 * SSM_GROUPS * SSM_STATE
N_MEM = 256
MEM_HEADS = 4
MEM_HEAD_DIM = D_MODEL // MEM_HEADS
N_EXPERTS = 8

LANES = 128
SUBLANES = 8
VMEM_LIMIT = 56 * 1024 * 1024
_MOE_VMEM_LIMIT = 58 * 1024 * 1024
_TOKEN_TILE = 1024

_BF = jnp.bfloat16
_F32 = jnp.float32
_NT = (((1,), (1,)), ((), ()))


def _cparams(n_axes, vmem_limit=VMEM_LIMIT):
    return pltpu.CompilerParams(dimension_semantics=("arbitrary",) * n_axes, vmem_limit_bytes=vmem_limit)


def _resident(shape):
    nd = len(shape)
    return pl.BlockSpec(shape, lambda *_: (0,) * nd, pipeline_mode=pl.Buffered(1))


def _rms(x, g):
    ms = jnp.mean(x * x, axis=-1, keepdims=True)
    return x * lax.rsqrt(ms + EPS) * g


def _silu(x):
    return x * jax.nn.sigmoid(x)


def _dot(a, b):
    return jnp.dot(a, b, preferred_element_type=_F32)


def _softplus(x):
    return jnp.maximum(x, 0.0) + jnp.log(1.0 + jnp.exp(-jnp.abs(x)))


def _split3(x):
    p1 = x.astype(_BF)
    r1 = x - p1.astype(_F32)
    p2 = r1.astype(_BF)
    p3 = (r1 - p2.astype(_F32)).astype(_BF)
    return p1, p2, p3


def _dot_sel_lhs(sel, x):
    p1, p2, p3 = _split3(x)
    return _dot(sel, p1) + _dot(sel, p2) + _dot(sel, p3)


def _dot_sel_rhs(x, sel):
    p1, p2, p3 = _split3(x)
    return _dot(p1, sel) + _dot(p2, sel) + _dot(p3, sel)


def _pos_and_seq(rows, seq):
    r = lax.broadcasted_iota(jnp.int32, (rows, 1), 0)
    if seq & (seq - 1) == 0:
        return r & (seq - 1), r >> (seq.bit_length() - 1)
    return lax.rem(r, seq), lax.div(r, seq)


def _kv_body(m_ref, wk_ref, wv_ref, k_ref, v_ref, kb_ref, vb_ref):
    m = m_ref[0].astype(_BF)
    for w_ref, o_ref, ob_ref in ((wk_ref, k_ref, kb_ref), (wv_ref, v_ref, vb_ref)):
        res = _dot(m, w_ref[0])
        ob_ref[0] = res.astype(_BF)
        for h in range(MEM_HEADS):
            o_ref[0, 0, :, h, :] = res[:, h * MEM_HEAD_DIM:(h + 1) * MEM_HEAD_DIM]


def _mem_kv(mem, wk, wv):
    bsz, n, d = mem.shape
    depth = wk.shape[0]
    out = jax.ShapeDtypeStruct((depth, bsz, n, MEM_HEADS, MEM_HEAD_DIM), _F32)
    out_b = jax.ShapeDtypeStruct((depth * bsz, n, d), _BF)
    wspec = pl.BlockSpec((1, d, d), lambda l, b: (l, 0, 0))
    ospec = pl.BlockSpec((1, 1, n, MEM_HEADS, MEM_HEAD_DIM), lambda l, b: (l, b, 0, 0, 0))
    bspec = pl.BlockSpec((1, n, d), lambda l, b: (l * bsz + b, 0, 0))
    return pl.pallas_call(
        _kv_body, out_shape=(out, out, out_b, out_b), grid=(depth, bsz),
        in_specs=[pl.BlockSpec((1, n, d), lambda l, b: (b, 0, 0)), wspec, wspec],
        out_specs=(ospec, ospec, bspec, bspec), compiler_params=_cparams(2), name="mem_kv",
    )(mem, wk, wv)


_PROJ_CHUNK = 512


def _inproj_ab_body(x_ref, g_ref, w_ref, u_ref, v_ref, z_ref, xbc_ref, dt_ref):
    hn = _rms(x_ref[...], g_ref[...]).astype(_BF)
    col = 0
    for ref, width, act in ((u_ref, D_MODEL, True), (v_ref, D_MODEL, True), (z_ref, D_MODEL, False),
                            (xbc_ref, SSM_CONV_DIM, False)):
        for c in range(width // _PROJ_CHUNK):
            r = _dot(hn, w_ref[:, col:col + _PROJ_CHUNK])
            if act:
                r = jax.nn.gelu(r)
            ref[:, c * _PROJ_CHUNK:(c + 1) * _PROJ_CHUNK] = r.astype(ref.dtype)
            col += _PROJ_CHUNK
    dt_ref[...] = _dot(hn, w_ref[:, col:col + LANES])


def _inproj_ab(x2d, g, w_in, act_dtype, tm):
    m = x2d.shape[0]
    row = lambda width: pl.BlockSpec((tm, width), lambda i: (i, 0))
    outs = (jax.ShapeDtypeStruct((m, D_MODEL), act_dtype),) * 3 + (
        jax.ShapeDtypeStruct((m, SSM_CONV_DIM), _F32), jax.ShapeDtypeStruct((m, LANES), _F32))
    return pl.pallas_call(
        _inproj_ab_body, out_shape=outs, grid=(m // tm,),
        in_specs=[row(D_MODEL), _resident(g.shape), _resident(w_in.shape)],
        out_specs=(row(D_MODEL), row(D_MODEL), row(D_MODEL), row(SSM_CONV_DIM), row(LANES)),
        compiler_params=_cparams(1), name="inproj_ab",
    )(x2d, g, w_in)


_PAIR = 2 * SSM_HEAD_DIM
_HPG = SSM_HEADS // SSM_GROUPS
_GRP_DIM = SSM_DIM // SSM_GROUPS


def _mix_ab_prompt_body(n_sub, u_ref, v_ref, z_ref, xbc_ref, dt_ref, ws_ref, bs_ref, cw_ref, cb_ref, dtb_ref, alog_ref,
                        dskip_ref, gn_ref, y_ref, st_ref, cst_ref, xp_ref, sT_ref, wsb_ref):
    q = CHUNK
    c = pl.program_id(1)
    tri = lax.broadcasted_iota(jnp.int32, (q, q), 0) >= lax.broadcasted_iota(jnp.int32, (q, q), 1)

    @pl.when((pl.program_id(0) == 0) & (c == 0))
    def _():
        for g in range(A_GROUPS):
            wsb_ref[g] = jnp.where(tri, ws_ref[g], 0.0).astype(_BF)

    @pl.when(c == 0)
    def _():
        xp_ref[0, 0:SUBLANES, :] = jnp.zeros((SUBLANES, SSM_CONV_DIM), _F32)
        sT_ref[...] = jnp.zeros_like(sT_ref)

    for sub in range(n_sub):
        _mix_ab_prompt_chunk(sub, n_sub, tri, u_ref, v_ref, z_ref, xbc_ref, dt_ref, bs_ref, cw_ref, cb_ref, dtb_ref,
                             alog_ref, dskip_ref, gn_ref, y_ref, xp_ref, sT_ref, wsb_ref)

    @pl.when(c == pl.num_programs(1) - 1)
    def _():
        for p in range(SSM_HEADS // 2):
            st_ref[0, p * _PAIR:(p + 1) * _PAIR, :] = sT_ref[:, p * _PAIR:(p + 1) * _PAIR].T
        cst_ref[0] = xp_ref[n_sub - 1, SUBLANES + q - (SSM_CONV - 1):SUBLANES + q, :]


def _mix_ab_prompt_chunk(sub, n_sub, tri, u_ref, v_ref, z_ref, xbc_ref, dt_ref, bs_ref, cw_ref, cb_ref, dtb_ref,
                         alog_ref, dskip_ref, gn_ref, y_ref, xp_ref, sT_ref, wsb_ref):
    q = CHUNK
    rows = slice(sub * q, (sub + 1) * q)

    for g in range(A_GROUPS):
        sl = slice(g * A_HEAD, (g + 1) * A_HEAD)
        s = _dot(wsb_ref[g], v_ref[rows, sl].astype(_BF)) + bs_ref[g]
        y_ref[rows, sl] = (u_ref[rows, sl].astype(_F32) * s).astype(y_ref.dtype)

    xbc = xbc_ref[rows, :]
    xp_ref[sub, SUBLANES:SUBLANES + q, :] = xbc
    base = SUBLANES - (SSM_CONV - 1)
    conv = xp_ref[sub, base:base + q, :] * cw_ref[0:1, :]
    for k in range(1, SSM_CONV):
        conv = conv + xp_ref[sub, base + k:base + k + q, :] * cw_ref[k:k + 1, :]
    xp_ref[(sub + 1) % n_sub, 0:SUBLANES, :] = xbc[q - SUBLANES:q, :]
    act = _silu(conv + cb_ref[...])
    xs = act[:, :SSM_DIM]
    bm = act[:, SSM_DIM:SSM_DIM + SSM_GROUPS * SSM_STATE]
    cm = act[:, SSM_DIM + SSM_GROUPS * SSM_STATE:]

    dt = _softplus(dt_ref[rows, :] + dtb_ref[...])
    d_a = dt * (-jnp.exp(alog_ref[...]))
    la = _dot_sel_lhs(tri.astype(_BF), d_a)
    la_t = la.T
    dt_t = dt.T
    e_la = jnp.exp(la)
    last = la[q - 1:q, :]
    wend = jnp.exp(last - la) * dt
    e_last = jnp.exp(last)
    lo = lax.broadcasted_iota(jnp.int32, (q, _PAIR), 1) < SSM_HEAD_DIM
    lo_row = lo[0:1, :]

    for g in range(SSM_GROUPS):
        bm_t = bm[:, g * SSM_STATE:(g + 1) * SSM_STATE].T.astype(_BF)
        cm_g = cm[:, g * SSM_STATE:(g + 1) * SSM_STATE].astype(_BF)
        cb = _dot(cm_g, bm_t)
        ys = []
        for pp in range(_HPG // 2):
            p = g * (_HPG // 2) + pp
            cols = slice(p * _PAIR, (p + 1) * _PAIR)
            wts = []
            for h in (2 * p, 2 * p + 1):
                seg = la[:, h:h + 1] - la_t[h:h + 1, :]
                wts.append(cb * jnp.exp(jnp.where(tri, seg, -jnp.inf)) * dt_t[h:h + 1, :])
            w2 = jnp.concatenate(wts, axis=1).astype(_BF)
            x_pair = xs[:, cols]
            x2 = jnp.concatenate([jnp.where(lo, x_pair, 0.0), jnp.where(lo, 0.0, x_pair)], axis=0).astype(_BF)
            s_t = sT_ref[:, cols]
            e_pair = jnp.where(lo, e_la[:, 2 * p:2 * p + 1], e_la[:, 2 * p + 1:2 * p + 2])
            y_pair = _dot(w2, x2) + _dot(cm_g, s_t.astype(_BF)) * e_pair
            wd_pair = jnp.where(lo, wend[:, 2 * p:2 * p + 1], wend[:, 2 * p + 1:2 * p + 2])
            el_pair = jnp.where(lo_row, e_last[:, 2 * p:2 * p + 1], e_last[:, 2 * p + 1:2 * p + 2])
            sT_ref[:, cols] = s_t * el_pair + _dot(bm_t, (x_pair * wd_pair).astype(_BF))
            ys.append(y_pair + x_pair * dskip_ref[:, cols])
        gcols = slice(g * _GRP_DIM, (g + 1) * _GRP_DIM)
        gated = jnp.concatenate(ys, axis=1) * _silu(z_ref[rows, gcols].astype(_F32))
        y_ref[rows, D_MODEL + g * _GRP_DIM:D_MODEL + (g + 1) * _GRP_DIM] = _rms(gated, gn_ref[:, gcols]).astype(y_ref.dtype)


_MIX_SUB_CHUNKS = 2


def _mix_ab_prompt(u, v, z, xbc, dt, bsz, seq, ws, bs, cw, cb, dtb, alog, dskip, gn):
    n_sub = _MIX_SUB_CHUNKS
    nc = seq // (CHUNK * n_sub)
    row = lambda width: pl.BlockSpec((CHUNK * n_sub, width), lambda b, c: (b * nc + c, 0))
    small = [ws, bs, cw, cb, dtb, alog, dskip, gn]
    outs = (jax.ShapeDtypeStruct((bsz * seq, 2 * D_MODEL), _BF),
            jax.ShapeDtypeStruct((bsz, SSM_DIM, SSM_STATE), _F32),
            jax.ShapeDtypeStruct((bsz, SSM_CONV - 1, SSM_CONV_DIM), _F32))
    return pl.pallas_call(
        functools.partial(_mix_ab_prompt_body, n_sub), out_shape=outs, grid=(bsz, nc),
        in_specs=[row(D_MODEL), row(D_MODEL), row(D_MODEL), row(SSM_CONV_DIM), row(LANES)]
        + [_resident(a.shape) for a in small],
        out_specs=(row(2 * D_MODEL),
                   pl.BlockSpec((1, SSM_DIM, SSM_STATE), lambda b, c: (b, 0, 0)),
                   pl.BlockSpec((1, SSM_CONV - 1, SSM_CONV_DIM), lambda b, c: (b, 0, 0))),
        scratch_shapes=[pltpu.VMEM((n_sub, SUBLANES + CHUNK, SSM_CONV_DIM), _F32),
                        pltpu.VMEM((SSM_STATE, SSM_DIM), _F32), pltpu.VMEM((A_GROUPS, CHUNK, CHUNK), _BF)],
        compiler_params=_cparams(2), name="mix_ab_prompt",
    )(u, v, z, xbc, dt, *small)


_SEQ_BLK = 8


def _mix_ab_sample_body(seq, u_ref, v_ref, z_ref, xbc_ref, dt_ref, e_ref, st_ref, coef_ref, sb_ref, cw_ref, cb_ref,
                        dtb_ref, alog_ref, dskip_ref, gn_ref, y_ref, sto_ref, sh_ref):
    rows = _SEQ_BLK * seq
    t, seq_of_row = _pos_and_seq(rows, seq)
    pad_rows = sh_ref.shape[0]
    sh_ref[...] = jnp.zeros((pad_rows, SSM_CONV_DIM), _F32)

    def shifts(val, deltas):
        width = val.shape[1]
        sh_ref[SUBLANES:SUBLANES + rows, 0:width] = val
        return [sh_ref[SUBLANES - d:SUBLANES - d + rows, 0:width] for d in deltas]

    back = tuple(range(1, seq))

    v = v_ref[...]
    s = coef_ref[0] * v + sb_ref[...]
    for d, vs in zip(back, shifts(v, back)):
        s = s + coef_ref[d] * jnp.where(t >= d, vs, 0.0)
    y_ref[:, 0:D_MODEL] = u_ref[...] * s

    xbc = xbc_ref[...]
    conv = xbc * cw_ref[SSM_CONV - 1:SSM_CONV, :]
    for j, xsft in zip(range(1, SSM_CONV), shifts(xbc, tuple(range(1, SSM_CONV)))):
        conv = conv + (jnp.where(t >= j, xsft, 0.0) + e_ref[j - 1]) * cw_ref[SSM_CONV - 1 - j:SSM_CONV - j, :]
    act = _silu(conv + cb_ref[...])
    xs = act[:, :SSM_DIM]
    bm = act[:, SSM_DIM:SSM_DIM + SSM_GROUPS * SSM_STATE]
    cm = act[:, SSM_DIM + SSM_GROUPS * SSM_STATE:]

    dt = _softplus(dt_ref[...] + dtb_ref[...])
    d_a = dt * (-jnp.exp(alog_ref[...]))
    la = d_a
    for d, sft in zip(back, shifts(d_a, back)):
        la = la + jnp.where(t >= d, sft, 0.0)
    la_back = shifts(la, back)
    dt_back = shifts(dt, back)
    last = jnp.where(t == seq - 1, la, 0.0)
    fwd = shifts(la, tuple(-d for d in back))
    for d, sft in zip(back, fwd):
        last = last + jnp.where(t == seq - 1 - d, sft, 0.0)
    wend = jnp.exp(last - la) * dt
    e_last = jnp.exp(last)
    e_la = jnp.exp(la)

    head_lane = lax.broadcasted_iota(jnp.int32, (rows, LANES), 1)
    bm_back = shifts(bm, back)

    def cb_heads(b_other):
        per_g = [jnp.sum(cm[:, g * SSM_STATE:(g + 1) * SSM_STATE] * b_other[:, g * SSM_STATE:(g + 1) * SSM_STATE],
                         axis=-1, keepdims=True) for g in range(SSM_GROUPS)]
        return jnp.where(head_lane < _HPG, per_g[0], per_g[1])

    gs = [cb_heads(bm) * dt]
    for i, d in enumerate(back):
        gd = cb_heads(bm_back[i]) * jnp.exp(la - la_back[i]) * dt_back[i]
        gs.append(jnp.where(t >= d, gd, 0.0))

    head0 = lax.broadcasted_iota(jnp.int32, (LANES, SSM_DIM), 0) * SSM_HEAD_DIM
    chan = lax.broadcasted_iota(jnp.int32, (LANES, SSM_DIM), 1)
    expand = ((chan >= head0) & (chan < head0 + SSM_HEAD_DIM)).astype(_BF)
    stacked = jnp.concatenate([e_la, wend, e_last] + gs, axis=0)
    full = _dot_sel_rhs(stacked, expand)
    e_la_f, wend_f, e_last_f = full[0:rows], full[rows:2 * rows], full[2 * rows:3 * rows]
    y2 = full[3 * rows:4 * rows] * xs
    for i, (d, xsft) in enumerate(zip(back, shifts(xs, back))):
        y2 = y2 + full[(4 + i) * rows:(5 + i) * rows] * jnp.where(t >= d, xsft, 0.0)

    zpad = jnp.zeros((LANES - rows, SSM_DIM), _F32)
    xw_t = jnp.concatenate([xs * wend_f, zpad], axis=0).T.astype(_BF)
    el_t = jnp.concatenate([e_last_f, zpad], axis=0).T
    row_lane = lax.broadcasted_iota(jnp.int32, (_GRP_DIM, LANES), 1)
    bpad = jnp.zeros((LANES - rows, SSM_STATE), _F32)

    y1 = [jnp.zeros((rows, _GRP_DIM), _F32) for _ in range(SSM_GROUPS)]
    for i in range(_SEQ_BLK):
        mine = seq_of_row == i
        for g in range(SSM_GROUPS):
            gr = slice(g * _GRP_DIM, (g + 1) * _GRP_DIM)
            st = st_ref[i, gr, :]
            b_g = jnp.where(mine, bm[:, g * SSM_STATE:(g + 1) * SSM_STATE], 0.0)
            b_g = jnp.concatenate([b_g, bpad], axis=0).astype(_BF)
            decay = jnp.sum(jnp.where(row_lane == i * seq, el_t[gr, :], 0.0), axis=1, keepdims=True)
            sto_ref[i, gr, :] = st * decay + _dot(xw_t[gr, :], b_g)
            c_g = cm[:, g * SSM_STATE:(g + 1) * SSM_STATE].astype(_BF)
            ch = lax.dot_general(c_g, st.astype(_BF), _NT, preferred_element_type=_F32)
            y1[g] = y1[g] + jnp.where(mine, ch, 0.0)

    for g in range(SSM_GROUPS):
        gr = slice(g * _GRP_DIM, (g + 1) * _GRP_DIM)
        ys = y1[g] * e_la_f[:, gr] + y2[:, gr] + xs[:, gr] * dskip_ref[:, gr]
        gated = ys * _silu(z_ref[:, gr])
        y_ref[:, D_MODEL + g * _GRP_DIM:D_MODEL + (g + 1) * _GRP_DIM] = _rms(gated, gn_ref[:, gr])


def _mix_ab_sample(u, v, z, xbc, dt, e_buf, state, seq, coef, sb, cw, cb, dtb, alog, dskip, gn):
    bsz = state.shape[0]
    rows = _SEQ_BLK * seq
    row = lambda width: pl.BlockSpec((rows, width), lambda i: (i, 0))
    small = [coef, sb, cw, cb, dtb, alog, dskip, gn]
    st_spec = pl.BlockSpec((_SEQ_BLK, SSM_DIM, SSM_STATE), lambda i: (i, 0, 0))
    outs = (jax.ShapeDtypeStruct((bsz * seq, 2 * D_MODEL), _F32), jax.ShapeDtypeStruct(state.shape, _F32))
    return pl.pallas_call(
        functools.partial(_mix_ab_sample_body, seq), out_shape=outs, grid=(bsz // _SEQ_BLK,),
        in_specs=[row(D_MODEL), row(D_MODEL), row(D_MODEL), row(SSM_CONV_DIM), row(LANES),
                  pl.BlockSpec((SSM_CONV - 1, rows, SSM_CONV_DIM), lambda i: (0, i, 0)), st_spec]
        + [_resident(a.shape) for a in small],
        out_specs=(row(2 * D_MODEL), st_spec),
        scratch_shapes=[pltpu.VMEM((rows + 2 * SUBLANES, SSM_CONV_DIM), _F32)],
        compiler_params=_cparams(1), name="mix_ab_sample",
    )(u, v, z, xbc, dt, e_buf, state, *small)


def _side_cast(src, n_steps, step_of):
    n_e, rows, cols = src.shape
    per_e = n_steps // n_e
    if per_e == 0 or n_steps % n_e or rows % per_e or (rows // per_e) % _BF16_ROWS:
        return None
    blk = (1, rows // per_e, cols)
    imap = lambda *g: (step_of(*g) // per_e, step_of(*g) % per_e, 0)
    return pl.BlockSpec(blk, imap), jax.ShapeDtypeStruct(src.shape, _BF)


def _out_q_body(n_out, a_ref, x_ref, wo_ref, gm_ref, wq_ref, *rest):
    side = rest[:len(rest) - n_out]
    xo_ref, q_ref = rest[len(side):len(side) + 2]
    xn = x_ref[...] + _dot(a_ref[...].astype(_BF), wo_ref[...])
    xo_ref[...] = xn
    q_ref[...] = _dot(_rms(xn, gm_ref[...]).astype(_BF), wq_ref[...]).astype(q_ref.dtype)
    if side:
        rest[-1][...] = side[0][...].astype(_BF)


def _out_q(a, x2d, wo, gm, wq, q_dtype, tm, cast_src=None):
    m = x2d.shape[0]
    row = lambda width: pl.BlockSpec((tm, width), lambda i: (i, 0))
    outs = [jax.ShapeDtypeStruct((m, D_MODEL), _F32), jax.ShapeDtypeStruct((m, D_MODEL), q_dtype)]
    in_specs = [row(a.shape[1]), row(D_MODEL), _resident(wo.shape), _resident(gm.shape), _resident(wq.shape)]
    out_specs = [row(D_MODEL), row(D_MODEL)]
    args = [a, x2d, wo, gm, wq]
    side = _side_cast(cast_src, m // tm, lambda i: i) if cast_src is not None else None
    if side:
        in_specs.append(side[0]), out_specs.append(side[0]), outs.append(side[1]), args.append(cast_src)
    res = pl.pallas_call(
        functools.partial(_out_q_body, len(outs)), out_shape=tuple(outs), grid=(m // tm,),
        in_specs=in_specs, out_specs=tuple(out_specs), compiler_params=_cparams(1), name="out_q",
    )(*args)
    if cast_src is None:
        return res
    return res[0], res[1], (res[2] if side else cast_src.astype(_BF))


_ATT_SCALE = MEM_HEAD_DIM ** -0.5


def _attend(q, k, v):
    outs = []
    for h in range(MEM_HEADS):
        sl = slice(h * MEM_HEAD_DIM, (h + 1) * MEM_HEAD_DIM)
        s = lax.dot_general(q[:, sl], k[:, sl], _NT, preferred_element_type=_F32) * _ATT_SCALE
        p = jnp.exp(s - jnp.max(s, axis=-1, keepdims=True))
        p = p / jnp.sum(p, axis=-1, keepdims=True)
        outs.append(_dot(p.astype(_BF), v[:, sl]))
    return jnp.concatenate(outs, axis=1)


def _attn_prompt_body(q_ref, k_ref, v_ref, x_ref, wo_ref, *rest):
    o_ref = rest[len(rest) // 2]
    o = _attend(q_ref[...], k_ref[0], v_ref[0])
    o_ref[...] = x_ref[...] + _dot(o.astype(_BF), wo_ref[...])
    if len(rest) > 1:
        rest[-1][...] = rest[0][...].astype(_BF)


def _attn_prompt(q, k, v, x2d, wo, layer, bsz, seq, tq, cast_src=None):
    nt = seq // tq
    row = pl.BlockSpec((tq, D_MODEL), lambda b, j: (b * nt + j, 0))
    kv = pl.BlockSpec((1, N_MEM, D_MODEL), lambda b, j: (layer * bsz + b, 0, 0))
    outs = [jax.ShapeDtypeStruct(x2d.shape, _F32)]
    in_specs, out_specs, args = [row, kv, kv, row, _resident(wo.shape)], [row], [q, k, v, x2d, wo]
    side = _side_cast(cast_src, bsz * nt, lambda b, j: b * nt + j) if cast_src is not None else None
    if side:
        in_specs.append(side[0]), out_specs.append(side[0]), outs.append(side[1]), args.append(cast_src)
    res = pl.pallas_call(
        _attn_prompt_body, out_shape=tuple(outs), grid=(bsz, nt),
        in_specs=in_specs, out_specs=tuple(out_specs), compiler_params=_cparams(2), name="attn_prompt",
    )(*args)
    if cast_src is None:
        return res[0]
    return res[0], (res[1] if side else cast_src.astype(_BF))


def _attn_sample_body(seq, q_ref, k_ref, v_ref, x_ref, wo_ref, o_ref, o_scr):
    per_tile = SUBLANES // seq
    _, seq_of_row = _pos_and_seq(SUBLANES, seq)
    n_lt = MEM_HEAD_DIM // LANES
    rows_mh = N_MEM * MEM_HEADS
    col = lax.broadcasted_iota(jnp.int32, (MEM_HEADS * SUBLANES, rows_mh), 1)
    row = lax.broadcasted_iota(jnp.int32, (MEM_HEADS * SUBLANES, rows_mh), 0)
    same_head = (col & (MEM_HEADS - 1)) == (row >> (SUBLANES.bit_length() - 1))
    for tile in range(_SEQ_BLK // per_tile):
        q8 = q_ref[tile * SUBLANES:(tile + 1) * SUBLANES, :].astype(_BF)
        o = None
        for j in range(per_tile):
            i = tile * per_tile + j
            s = None
            for lt in range(n_lt):
                qh = jnp.concatenate([q8[:, h * MEM_HEAD_DIM + lt * LANES:h * MEM_HEAD_DIM + (lt + 1) * LANES]
                                      for h in range(MEM_HEADS)], axis=0)
                k_lt = k_ref[i, :, :, lt * LANES:(lt + 1) * LANES].reshape(rows_mh, LANES).astype(_BF)
                part = lax.dot_general(qh, k_lt, _NT, preferred_element_type=_F32)
                s = part if s is None else s + part
            s = jnp.where(same_head, s * _ATT_SCALE, -jnp.inf)
            p = jnp.exp(s - jnp.max(s, axis=-1, keepdims=True))
            p = (p / jnp.sum(p, axis=-1, keepdims=True)).astype(_BF)
            pieces = [None] * (MEM_HEADS * n_lt)
            for lt in range(n_lt):
                v_lt = v_ref[i, :, :, lt * LANES:(lt + 1) * LANES].reshape(rows_mh, LANES).astype(_BF)
                o_lt = _dot(p, v_lt)
                for h in range(MEM_HEADS):
                    pieces[h * n_lt + lt] = o_lt[h * SUBLANES:(h + 1) * SUBLANES, :]
            oi = jnp.concatenate(pieces, axis=1)
            o = oi if o is None else jnp.where(seq_of_row == j, oi, o)
        o_scr[tile * SUBLANES:(tile + 1) * SUBLANES, :] = o
    o_ref[...] = x_ref[...] + _dot(o_scr[...].astype(_BF), wo_ref[...])


def _attn_sample(q, k, v, x2d, wo, layer, bsz, seq):
    rows = _SEQ_BLK * seq
    nb = bsz // _SEQ_BLK
    row = pl.BlockSpec((rows, D_MODEL), lambda i: (i, 0))
    kv = pl.BlockSpec((_SEQ_BLK, N_MEM, MEM_HEADS, MEM_HEAD_DIM), lambda i: (layer * nb + i, 0, 0, 0))
    return pl.pallas_call(
        functools.partial(_attn_sample_body, seq), out_shape=jax.ShapeDtypeStruct(x2d.shape, _F32), grid=(nb,),
        in_specs=[row, kv, kv, row, _resident(wo.shape)], out_specs=row,
        scratch_shapes=[pltpu.VMEM((rows, D_MODEL), _F32)],
        compiler_params=_cparams(1), name="attn_sample",
    )(q, k, v, x2d, wo)


_FF_SUB = 512
_MOE_TILE = 1024


def _swiglu_chunk(h, wg_ref, wu_ref, wd_ref):
    width = wg_ref.shape[2]
    hids = []
    for lo in range(0, width, _FF_SUB):
        hi = min(lo + _FF_SUB, width)
        hids.append((_silu(_dot(h, wg_ref[0, :, lo:hi])) * _dot(h, wu_ref[0, :, lo:hi])).astype(_BF))
    return _dot(jnp.concatenate(hids, axis=1), wd_ref[0])


def _ffn_body(x_ref, g_ref, wg_ref, wu_ref, wd_ref, o_ref):
    x = x_ref[...]
    o_ref[...] = x + _swiglu_chunk(_rms(x, g_ref[...]).astype(_BF), wg_ref, wu_ref, wd_ref)


def _ffn(x2d, g, wg, wu, wd, tm):
    m = x2d.shape[0]
    row = pl.BlockSpec((tm, D_MODEL), lambda i: (i, 0))
    return pl.pallas_call(
        _ffn_body, out_shape=jax.ShapeDtypeStruct(x2d.shape, _F32), grid=(m // tm,),
        in_specs=[row, _resident(g.shape), _resident(wg.shape), _resident(wu.shape), _resident(wd.shape)],
        out_specs=row, compiler_params=_cparams(1), name="ffn_dense",
    )(x2d, g, wg, wu, wd)


_MOE_PICK_PROB = 2.0 / N_EXPERTS
_MOE_CAP_STEPS = (1.0, 1.25, 1.5)


_BF16_ROWS = 16
_MOE_DENSE_ROWS = 256


def _moe_capacities(tm):
    mean = tm * _MOE_PICK_PROB
    return tuple(int(-(-(mean * s) // _BF16_ROWS) * _BF16_ROWS) for s in _MOE_CAP_STEPS)


def _moe_body(caps, final_norm, x_ref, g_ref, wr_ref, wg_ref, wu_ref, wd_ref, gf_ref, o_ref, hf_ref, comb_ref,
              rank_ref, rank_t_ref, cnt_ref):
    e = pl.program_id(1)
    tm = x_ref.shape[0]
    lane = lax.broadcasted_iota(jnp.int32, (tm, LANES), 1)

    @pl.when(e == 0)
    def _():
        x = x_ref[...]
        hf = _rms(x, g_ref[...])
        hf_hi = hf.astype(_BF)
        hf_ref[...] = hf_hi
        o_ref[...] = x
        hf_lo = (hf - hf_hi.astype(_F32)).astype(_BF)
        wr = wr_ref[...]
        wr_hi = wr.astype(_BF)
        wr_lo = (wr - wr_hi.astype(_F32)).astype(_BF)
        logits = _dot(hf_hi, wr_hi) + (_dot(hf_hi, wr_lo) + _dot(hf_lo, wr_hi))
        lg = jnp.where(lane < N_EXPERTS, logits, -jnp.inf)
        v1 = jnp.max(lg, axis=-1, keepdims=True)
        i1 = jnp.min(jnp.where(lg == v1, lane, LANES), axis=-1, keepdims=True)
        lg2 = jnp.where(lane == i1, -jnp.inf, lg)
        v2 = jnp.max(lg2, axis=-1, keepdims=True)
        i2 = jnp.min(jnp.where(lg2 == v2, lane, LANES), axis=-1, keepdims=True)
        e2 = jnp.exp(v2 - v1)
        comb_ref[...] = jnp.where(lane == i1, 1.0 / (1.0 + e2), 0.0) + jnp.where(lane == i2, e2 / (1.0 + e2), 0.0)
        picked = (lane == i1) | (lane == i2)
        picked_b = picked.astype(_BF)
        blk_r = lax.broadcasted_iota(jnp.int32, (LANES, LANES), 0)
        blk_c = lax.broadcasted_iota(jnp.int32, (LANES, LANES), 1)
        earlier = (blk_c < blk_r).astype(_BF)
        running = jnp.zeros((1, LANES), _F32)
        for b in range(tm // LANES):
            rows = slice(b * LANES, (b + 1) * LANES)
            pb = picked_b[rows, :]
            rank_b = jnp.where(picked[rows, :], _dot(earlier, pb) + running, -1.0)
            rank_ref[rows, :] = rank_b
            rank_t_ref[:, rows] = rank_b.T
            running = running + jnp.sum(pb.astype(_F32), axis=0, keepdims=True)
        cnt_ref[...] = running

    lane_row = lax.broadcasted_iota(jnp.int32, (1, LANES), 1)
    count = jnp.sum(jnp.where(lane_row == e, cnt_ref[...], 0.0)).astype(jnp.int32)

    def compact_path(cap):
        cap_pad = -(-cap // LANES) * LANES
        rank_row = rank_t_ref[pl.ds(e, 1), :]
        slot_col = lax.broadcasted_iota(jnp.int32, (cap, 1), 0).astype(_F32)
        pick = (rank_row == slot_col).astype(_BF)
        xs = _dot(pick, hf_ref[...]).astype(_BF)
        y = _swiglu_chunk(xs, wg_ref, wu_ref, wd_ref).astype(_BF)
        if cap_pad > cap:
            y = jnp.concatenate([y, jnp.zeros((cap_pad - cap, D_MODEL), _BF)], axis=0)
        rank_col = jnp.sum(jnp.where(lane == e, rank_ref[...], 0.0), axis=1, keepdims=True)
        comb_col = jnp.sum(jnp.where(lane == e, comb_ref[...], 0.0), axis=1, keepdims=True)
        slot_row = lax.broadcasted_iota(jnp.int32, (1, cap_pad), 1).astype(_F32)
        put = (rank_col == slot_row).astype(_BF)
        o_ref[...] += comb_col * _dot(put, y)

    lower = 0
    for cap in caps:
        pl.when((count > lower) & (count <= cap))(functools.partial(compact_path, cap))
        lower = cap

    @pl.when(count > caps[-1])
    def _():
        blk = min(tm, _MOE_DENSE_ROWS)
        blk_lane = lax.broadcasted_iota(jnp.int32, (blk, LANES), 1)

        def dense_rows(r, carry):
            rows = pl.ds(pl.multiple_of(r * blk, blk), blk)
            comb_col = jnp.sum(jnp.where(blk_lane == e, comb_ref[rows, :], 0.0), axis=1, keepdims=True)
            o_ref[rows, :] += comb_col * _swiglu_chunk(hf_ref[rows, :], wg_ref, wu_ref, wd_ref)
            return carry

        lax.fori_loop(0, tm // blk, dense_rows, 0)

    if final_norm:
        @pl.when(e == pl.num_programs(1) - 1)
        def _():
            o_ref[...] = _rms(o_ref[...], gf_ref[...])


def _moe(x2d, g, wr, wg, wu, wd, gf, final_norm, tm):
    m = x2d.shape[0]
    n_e, _, d_ff = wg.shape
    assert tm % LANES == 0 and m % tm == 0, (m, tm)
    caps = _moe_capacities(tm)
    row_once = pl.BlockSpec((tm, D_MODEL), lambda i, e: (i, 0), pipeline_mode=pl.Buffered(1))
    up = pl.BlockSpec((1, D_MODEL, d_ff), lambda i, e: (e, 0, 0))
    down = pl.BlockSpec((1, d_ff, D_MODEL), lambda i, e: (e, 0, 0))
    return pl.pallas_call(
        functools.partial(_moe_body, caps, final_norm), out_shape=jax.ShapeDtypeStruct(x2d.shape, _F32),
        grid=(m // tm, n_e),
        in_specs=[row_once, _resident(g.shape), _resident(wr.shape), up, up, down, _resident(gf.shape)],
        out_specs=pl.BlockSpec((tm, D_MODEL), lambda i, e: (i, 0)),
        scratch_shapes=[pltpu.VMEM((tm, D_MODEL), _BF), pltpu.VMEM((tm, LANES), _F32), pltpu.VMEM((tm, LANES), _F32),
                        pltpu.VMEM((LANES, tm), _F32), pltpu.VMEM((1, LANES), _F32)],
        compiler_params=_cparams(2, _MOE_VMEM_LIMIT), name="moe",
    )(x2d, g, wr, wg, wu, wd, gf)


_SC_CONV = 3


def _mix_c_body(per_seq, seq, x_ref, g_ref, win_ref, cw_ref, e_ref, wout_ref, gm_ref, wq_ref, xo_ref, q_ref, p_ref,
                pp_ref):
    tm = x_ref.shape[0]
    x = x_ref[...]
    hn = _rms(x, g_ref[...]).astype(_BF)
    bg = _dot(hn, win_ref[:, 0:D_MODEL])
    p = _dot(hn, win_ref[:, D_MODEL:2 * D_MODEL]) * _dot(hn, win_ref[:, 2 * D_MODEL:3 * D_MODEL])
    if per_seq:
        @pl.when(pl.program_id(1) == 0)
        def _():
            pp_ref[0:SUBLANES, :] = jnp.zeros((SUBLANES, D_MODEL), _F32)
    else:
        pp_ref[0:SUBLANES, :] = jnp.zeros((SUBLANES, D_MODEL), _F32)
    pp_ref[SUBLANES:SUBLANES + tm, :] = p
    p1 = pp_ref[SUBLANES - 1:SUBLANES - 1 + tm, :]
    p2 = pp_ref[SUBLANES - 2:SUBLANES - 2 + tm, :]
    if per_seq:
        pp_ref[0:SUBLANES, :] = p[tm - SUBLANES:tm, :]
    else:
        t, _ = _pos_and_seq(tm, seq)
        p1 = jnp.where(t >= 1, p1, 0.0) + e_ref[0]
        p2 = jnp.where(t >= 2, p2, 0.0) + e_ref[1]
    conv = p2 * cw_ref[0:1, :] + p1 * cw_ref[1:2, :] + p * cw_ref[2:3, :]
    xn = x + _dot((bg * conv).astype(_BF), wout_ref[...])
    xo_ref[...] = xn
    q_ref[...] = _dot(_rms(xn, gm_ref[...]).astype(_BF), wq_ref[...]).astype(q_ref.dtype)
    if per_seq:
        @pl.when(pl.program_id(1) == pl.num_programs(1) - 1)
        def _():
            p_ref[0] = pp_ref[SUBLANES + tm - (_SC_CONV - 1):SUBLANES + tm, :]
    else:
        p_ref[...] = p


def _mix_c(x2d, g, win, cw, e_buf, wout, gm, wq, q_dtype, per_seq, bsz, seq, tm):
    m = x2d.shape[0]
    if per_seq:
        nt = seq // tm
        grid = (bsz, nt)
        imap = lambda b, j: (b * nt + j, 0)
        e_spec = pl.BlockSpec((_SC_CONV - 1, SUBLANES, D_MODEL), lambda b, j: (0, 0, 0))
        p_shape = jax.ShapeDtypeStruct((bsz, _SC_CONV - 1, D_MODEL), _F32)
        p_spec = pl.BlockSpec((1, _SC_CONV - 1, D_MODEL), lambda b, j: (b, 0, 0))
    else:
        grid = (m // tm, 1)
        imap = lambda i, j: (i, 0)
        e_spec = pl.BlockSpec((_SC_CONV - 1, tm, D_MODEL), lambda i, j: (0, i, 0))
        p_shape = jax.ShapeDtypeStruct((m, D_MODEL), _F32)
        p_spec = pl.BlockSpec((tm, D_MODEL), imap)
    row = pl.BlockSpec((tm, D_MODEL), imap)
    outs = (jax.ShapeDtypeStruct((m, D_MODEL), _F32), jax.ShapeDtypeStruct((m, D_MODEL), q_dtype), p_shape)
    return pl.pallas_call(
        functools.partial(_mix_c_body, per_seq, seq), out_shape=outs, grid=grid,
        in_specs=[row, _resident(g.shape), _resident(win.shape), _resident(cw.shape), e_spec, _resident(wout.shape),
                  _resident(gm.shape), _resident(wq.shape)],
        out_specs=(row, row, p_spec),
        scratch_shapes=[pltpu.VMEM((SUBLANES + tm, D_MODEL), _F32)],
        compiler_params=_cparams(2), name="mix_c",
    )(x2d, g, win, cw, e_buf, wout, gm, wq)


def _expand_buf(buf, seq, n_back):
    bsz, km1, ch = buf.shape
    outs = []
    for j in range(1, n_back + 1):
        rows = [buf[:, km1 - j + t] if t < j else jnp.zeros((bsz, ch), buf.dtype) for t in range(seq)]
        outs.append(jnp.stack(rows, axis=1).reshape(bsz * seq, ch))
    return jnp.stack(outs)


def _row(vec, width=None):
    vec = vec.reshape(1, -1).astype(_F32)
    if width is not None and vec.shape[1] < width:
        vec = jnp.pad(vec, ((0, 0), (0, width - vec.shape[1])))
    return vec


def _trunk(x, k_all, v_all, prompt, states, w):
    bsz, seq, _ = x.shape
    m = bsz * seq
    x2d = x.reshape(m, D_MODEL)
    tm = min(_TOKEN_TILE, m)
    dskip = _row(jnp.repeat(w["d_skip"][0], SSM_HEAD_DIM))
    common = (w["conv_w_ssm"][0], _row(w["conv_b_ssm"][0]), _row(w["dt_bias"][0], LANES), _row(w["a_log"][0], LANES),
              dskip, _row(w["g_ssm_norm"][0]))

    act_dtype = _BF if prompt else _F32
    u, v, z, xbc, dt = _inproj_ab(x2d, _row(w["g_mix"][0]), w["w_in_ab"], act_dtype, tm)
    if prompt:
        y_ab, ssm_state, conv_state = _mix_ab_prompt(
            u, v, z, xbc, dt, bsz, seq, w["w_spatial"][0], w["b_spatial"][0][:, :, None], *common)
        v_out = None
    else:
        ws = w["w_spatial"][0][:, :seq, :seq]
        coef = []
        for d in range(seq):
            per_t = [ws[:, t, t - d] if t >= d else jnp.zeros((A_GROUPS,), _F32) for t in range(seq)]
            coef.append(jnp.tile(jnp.repeat(jnp.stack(per_t), A_HEAD, axis=1), (_SEQ_BLK, 1)))
        sb = jnp.tile(jnp.repeat(w["b_spatial"][0][:, :seq].T, A_HEAD, axis=1), (_SEQ_BLK, 1))
        e_buf = _expand_buf(states["ssm_conv"], seq, SSM_CONV - 1)
        y_ab, ssm_state = _mix_ab_sample(u, v, z, xbc, dt, e_buf, states["ssm"].reshape(bsz, SSM_DIM, SSM_STATE), seq,
                                         jnp.stack(coef), sb, *common)
        conv_state = xbc.reshape(bsz, seq, SSM_CONV_DIM)[:, seq - (SSM_CONV - 1):]
        v_out = v.reshape(bsz, seq, D_MODEL)
    experts = experts_f32 = w["experts"]
    if prompt:
        x2d, q, gate_bf = _out_q(y_ab, x2d, w["w_out_ab"], _row(w["g_mem"][0]), w["w_mem_q"][0], act_dtype, tm,
                                 cast_src=experts_f32[0])
        x2d, up_bf = _attn_prompt(q, k_all, v_all, x2d, w["w_mem_o"][0], 0, bsz, seq, tm, cast_src=experts_f32[1])
    else:
        x2d, q = _out_q(y_ab, x2d, w["w_out_ab"], _row(w["g_mem"][0]), w["w_mem_q"][0], act_dtype, tm)
        x2d = _attn_sample(q, k_all, v_all, x2d, w["w_mem_o"][0], 0, bsz, seq)
    x2d = _ffn(x2d, _row(w["g_ffn"][0]), w["w_ffn_gate"], w["w_ffn_up"], w["w_ffn_down"], min(_MOE_TILE, m))

    if prompt:
        e_buf = jnp.zeros((_SC_CONV - 1, SUBLANES, D_MODEL), _F32)
        x2d, q, sconv_state = _mix_c(x2d, _row(w["g_mix"][1]), w["w_in_c"], w["conv_w_c"][0], e_buf, w["w_out_c"],
                                     _row(w["g_mem"][1]), w["w_mem_q"][1], act_dtype, True, bsz, seq, tm)
    else:
        e_buf = _expand_buf(states["sconv"], seq, _SC_CONV - 1)
        x2d, q, p = _mix_c(x2d, _row(w["g_mix"][1]), w["w_in_c"], w["conv_w_c"][0], e_buf, w["w_out_c"],
                           _row(w["g_mem"][1]), w["w_mem_q"][1], act_dtype, False, bsz, seq, tm)
        sconv_state = p.reshape(bsz, seq, D_MODEL)[:, seq - (_SC_CONV - 1):]
    if prompt:
        x2d, down_bf = _attn_prompt(q, k_all, v_all, x2d, w["w_mem_o"][1], 1, bsz, seq, tm, cast_src=experts_f32[2])
        experts = (gate_bf, up_bf, down_bf)
    else:
        x2d = _attn_sample(q, k_all, v_all, x2d, w["w_mem_o"][1], 1, bsz, seq)
    x2d = _moe(x2d, _row(w["g_ffn"][1]), w["w_router"], *experts, _row(w["g_final"]), True, min(_MOE_TILE, m))
    y = x2d.reshape(bsz, seq, D_MODEL)
    ssm_state = ssm_state.reshape(1, bsz, SSM_HEADS, SSM_HEAD_DIM, SSM_STATE)
    return y, ssm_state, conv_state[None], sconv_state[None], v_out, experts


def kernel(x_prompt, x_sample, mem_prompt, state_ssm, state_ssm_conv, state_sconv, cache_mem_k, cache_mem_v, g_mix, g_mem, g_ffn, g_final, w_in_ab, w_spatial, b_spatial, conv_w_ssm, conv_b_ssm, dt_bias, a_log, d_skip, g_ssm_norm, w_out_ab, w_ffn_gate, w_ffn_up, w_ffn_down, w_in_c, conv_w_c, w_out_c, w_router, w_exp_gate, w_exp_up, w_exp_down, w_mem_q, w_mem_k, w_mem_v, w_mem_o):
    depth = w_mem_q.shape[0]
    bp = x_prompt.shape[0]
    bs = x_sample.shape[0]
    bf = lambda a: a.astype(_BF)
    w = dict(
        g_mix=g_mix, g_mem=g_mem, g_ffn=g_ffn, g_final=g_final, w_spatial=w_spatial, b_spatial=b_spatial,
        conv_w_ssm=conv_w_ssm, conv_b_ssm=conv_b_ssm, dt_bias=dt_bias, a_log=a_log, d_skip=d_skip, g_ssm_norm=g_ssm_norm,
        conv_w_c=conv_w_c,
        w_in_ab=jnp.pad(bf(w_in_ab[0]), ((0, 0), (0, LANES - SSM_HEADS))),
        w_out_ab=bf(w_out_ab[0]), w_ffn_gate=bf(w_ffn_gate), w_ffn_up=bf(w_ffn_up), w_ffn_down=bf(w_ffn_down),
        w_in_c=bf(w_in_c[0]), w_out_c=bf(w_out_c[0]),
        w_router=jnp.pad(w_router[0], ((0, 0), (0, LANES - N_EXPERTS))),
        experts=(w_exp_gate[0], w_exp_up[0], w_exp_down[0]),
        w_mem_q=bf(w_mem_q), w_mem_o=bf(w_mem_o),
    )
    mem_k_p, mem_v_p, k_bf, v_bf = _mem_kv(mem_prompt, bf(w_mem_k), bf(w_mem_v))
    y_p, ssm_p, ssmconv_p, sconv_p, _, experts_bf = _trunk(x_prompt, k_bf, v_bf, True, None, w)
    w = dict(w, experts=experts_bf)
    states = dict(ssm=state_ssm[0], ssm_conv=state_ssm_conv[0], sconv=state_sconv[0])
    y_s, ssm_s, ssmconv_s, sconv_s, v_s, _ = _trunk(
        x_sample, cache_mem_k.reshape(depth * bs, N_MEM, MEM_HEADS, MEM_HEAD_DIM),
        cache_mem_v.reshape(depth * bs, N_MEM, MEM_HEADS, MEM_HEAD_DIM), False, states, w)
    return (y_p, y_s, ssm_p, ssm_s, ssmconv_p, ssmconv_s, sconv_p, sconv_s, mem_k_p, mem_v_p, v_s[None])
```

```python
import functools

import jax
import jax.numpy as jnp
from jax import lax
from jax.experimental import pallas as pl
from jax.experimental.pallas import tpu as pltpu

D_MODEL = 1024
EPS = 1e-6
CHUNK = 128
A_GROUPS = 8
A_HEAD = D_MODEL // A_GROUPS
SSM_HEADS = 16
SSM_HEAD_DIM = 64
SSM_GROUPS = 2
SSM_STATE = 128
SSM_CONV = 4
SSM_DIM = D_MODEL
SSM_CONV_DIM = SSM_DIM + 2 * SSM_GROUPS * SSM_STATE
N_MEM = 256
MEM_HEADS = 4
MEM_HEAD_DIM = D_MODEL // MEM_HEADS
N_EXPERTS = 8

LANES = 128
SUBLANES = 8
VMEM_LIMIT = 56 * 1024 * 1024
_MOE_VMEM_LIMIT = 58 * 1024 * 1024
_TOKEN_TILE = 1024

_BF = jnp.bfloat16
_F32 = jnp.float32
_NT = (((1,), (1,)), ((), ()))


def _cparams(n_axes, vmem_limit=VMEM_LIMIT):
    return pltpu.CompilerParams(dimension_semantics=("arbitrary",) * n_axes, vmem_limit_bytes=vmem_limit)


def _resident(shape):
    nd = len(shape)
    return pl.BlockSpec(shape, lambda *_: (0,) * nd, pipeline_mode=pl.Buffered(1))


def _rms(x, g):
    ms = jnp.mean(x * x, axis=-1, keepdims=True)
    return x * lax.rsqrt(ms + EPS) * g


def _silu(x):
    return x * jax.nn.sigmoid(x)


def _dot(a, b):
    return jnp.dot(a, b, preferred_element_type=_F32)


def _softplus(x):
    return jnp.maximum(x, 0.0) + jnp.log(1.0 + jnp.exp(-jnp.abs(x)))


def _split3(x):
    p1 = x.astype(_BF)
    r1 = x - p1.astype(_F32)
    p2 = r1.astype(_BF)
    p3 = (r1 - p2.astype(_F32)).astype(_BF)
    return p1, p2, p3


def _dot_sel_lhs(sel, x):
    p1, p2, p3 = _split3(x)
    return _dot(sel, p1) + _dot(sel, p2) + _dot(sel, p3)


def _dot_sel_rhs(x, sel):
    p1, p2, p3 = _split3(x)
    return _dot(p1, sel) + _dot(p2, sel) + _dot(p3, sel)


def _pos_and_seq(rows, seq):
    r = lax.broadcasted_iota(jnp.int32, (rows, 1), 0)
    if seq & (seq - 1) == 0:
        return r & (seq - 1), r >> (seq.bit_length() - 1)
    return lax.rem(r, seq), lax.div(r, seq)


def _kv_body(m_ref, wk_ref, wv_ref, k_ref, v_ref, kb_ref, vb_ref):
    m = m_ref[0].astype(_BF)
    for w_ref, o_ref, ob_ref in ((wk_ref, k_ref, kb_ref), (wv_ref, v_ref, vb_ref)):
        res = _dot(m, w_ref[0])
        ob_ref[0] = res.astype(_BF)
        for h in range(MEM_HEADS):
            o_ref[0, 0, :, h, :] = res[:, h * MEM_HEAD_DIM:(h + 1) * MEM_HEAD_DIM]


def _mem_kv(mem, wk, wv):
    bsz, n, d = mem.shape
    depth = wk.shape[0]
    out = jax.ShapeDtypeStruct((depth, bsz, n, MEM_HEADS, MEM_HEAD_DIM), _F32)
    out_b = jax.ShapeDtypeStruct((depth * bsz, n, d), _BF)
    wspec = pl.BlockSpec((1, d, d), lambda l, b: (l, 0, 0))
    ospec = pl.BlockSpec((1, 1, n, MEM_HEADS, MEM_HEAD_DIM), lambda l, b: (l, b, 0, 0, 0))
    bspec = pl.BlockSpec((1, n, d), lambda l, b: (l * bsz + b, 0, 0))
    return pl.pallas_call(
        _kv_body, out_shape=(out, out, out_b, out_b), grid=(depth, bsz),
        in_specs=[pl.BlockSpec((1, n, d), lambda l, b: (b, 0, 0)), wspec, wspec],
        out_specs=(ospec, ospec, bspec, bspec), compiler_params=_cparams(2), name="mem_kv",
    )(mem, wk, wv)


_PROJ_CHUNK = 512


def _inproj_ab_body(x_ref, g_ref, w_ref, u_ref, v_ref, z_ref, xbc_ref, dt_ref):
    hn = _rms(x_ref[...], g_ref[...]).astype(_BF)
    col = 0
    for ref, width, act in ((u_ref, D_MODEL, True), (v_ref, D_MODEL, True), (z_ref, D_MODEL, False),
                            (xbc_ref, SSM_CONV_DIM, False)):
        for c in range(width // _PROJ_CHUNK):
            r = _dot(hn, w_ref[:, col:col + _PROJ_CHUNK])
            if act:
                r = jax.nn.gelu(r)
            ref[:, c * _PROJ_CHUNK:(c + 1) * _PROJ_CHUNK] = r.astype(ref.dtype)
            col += _PROJ_CHUNK
    dt_ref[...] = _dot(hn, w_ref[:, col:col + LANES])


def _inproj_ab(x2d, g, w_in, act_dtype, tm):
    m = x2d.shape[0]
    row = lambda width: pl.BlockSpec((tm, width), lambda i: (i, 0))
    outs = (jax.ShapeDtypeStruct((m, D_MODEL), act_dtype),) * 3 + (
        jax.ShapeDtypeStruct((m, SSM_CONV_DIM), _F32), jax.ShapeDtypeStruct((m, LANES), _F32))
    return pl.pallas_call(
        _inproj_ab_body, out_shape=outs, grid=(m // tm,),
        in_specs=[row(D_MODEL), _resident(g.shape), _resident(w_in.shape)],
        out_specs=(row(D_MODEL), row(D_MODEL), row(D_MODEL), row(SSM_CONV_DIM), row(LANES)),
        compiler_params=_cparams(1), name="inproj_ab",
    )(x2d, g, w_in)


_PAIR = 2 * SSM_HEAD_DIM
_HPG = SSM_HEADS // SSM_GROUPS
_GRP_DIM = SSM_DIM // SSM_GROUPS


def _mix_ab_prompt_body(n_sub, n_side, u_ref, v_ref, z_ref, xbc_ref, dt_ref, ws_ref, bs_ref, cw_ref, cb_ref, dtb_ref,
                        alog_ref, dskip_ref, gn_ref, *rest):
    y_ref, st_ref, cst_ref = rest[n_side:n_side + 3]
    xp_ref, sT_ref, wsb_ref = rest[2 * n_side + 3:]
    for src_ref, dst_ref in zip(rest[:n_side], rest[n_side + 3:2 * n_side + 3]):
        dst_ref[...] = src_ref[...].astype(_BF)
    q = CHUNK
    c = pl.program_id(1)
    tri = lax.broadcasted_iota(jnp.int32, (q, q), 0) >= lax.broadcasted_iota(jnp.int32, (q, q), 1)

    @pl.when((pl.program_id(0) == 0) & (c == 0))
    def _():
        for g in range(A_GROUPS):
            wsb_ref[g] = jnp.where(tri, ws_ref[g], 0.0).astype(_BF)

    @pl.when(c == 0)
    def _():
        xp_ref[0, 0:SUBLANES, :] = jnp.zeros((SUBLANES, SSM_CONV_DIM), _F32)
        sT_ref[...] = jnp.zeros_like(sT_ref)

    for sub in range(n_sub):
        _mix_ab_prompt_chunk(sub, n_sub, tri, u_ref, v_ref, z_ref, xbc_ref, dt_ref, bs_ref, cw_ref, cb_ref, dtb_ref,
                             alog_ref, dskip_ref, gn_ref, y_ref, xp_ref, sT_ref, wsb_ref)

    @pl.when(c == pl.num_programs(1) - 1)
    def _():
        for p in range(SSM_HEADS // 2):
            st_ref[0, p * _PAIR:(p + 1) * _PAIR, :] = sT_ref[:, p * _PAIR:(p + 1) * _PAIR].T
        cst_ref[0] = xp_ref[n_sub - 1, SUBLANES + q - (SSM_CONV - 1):SUBLANES + q, :]


def _mix_ab_prompt_chunk(sub, n_sub, tri, u_ref, v_ref, z_ref, xbc_ref, dt_ref, bs_ref, cw_ref, cb_ref, dtb_ref,
                         alog_ref, dskip_ref, gn_ref, y_ref, xp_ref, sT_ref, wsb_ref):
    q = CHUNK
    rows = slice(sub * q, (sub + 1) * q)

    for g in range(A_GROUPS):
        sl = slice(g * A_HEAD, (g + 1) * A_HEAD)
        s = _dot(wsb_ref[g], v_ref[rows, sl].astype(_BF)) + bs_ref[g]
        y_ref[rows, sl] = (u_ref[rows, sl].astype(_F32) * s).astype(y_ref.dtype)

    xbc = xbc_ref[rows, :]
    xp_ref[sub, SUBLANES:SUBLANES + q, :] = xbc
    base = SUBLANES - (SSM_CONV - 1)
    conv = xp_ref[sub, base:base + q, :] * cw_ref[0:1, :]
    for k in range(1, SSM_CONV):
        conv = conv + xp_ref[sub, base + k:base + k + q, :] * cw_ref[k:k + 1, :]
    xp_ref[(sub + 1) % n_sub, 0:SUBLANES, :] = xbc[q - SUBLANES:q, :]
    act = _silu(conv + cb_ref[...])
    xs = act[:, :SSM_DIM]
    bm = act[:, SSM_DIM:SSM_DIM + SSM_GROUPS * SSM_STATE]
    cm = act[:, SSM_DIM + SSM_GROUPS * SSM_STATE:]

    dt = _softplus(dt_ref[rows, :] + dtb_ref[...])
    d_a = dt * (-jnp.exp(alog_ref[...]))
    la = _dot_sel_lhs(tri.astype(_BF), d_a)
    la_t = la.T
    dt_t = dt.T
    e_la = jnp.exp(la)
    last = la[q - 1:q, :]
    wend = jnp.exp(last - la) * dt
    e_last = jnp.exp(last)
    lo = lax.broadcasted_iota(jnp.int32, (q, _PAIR), 1) < SSM_HEAD_DIM
    lo_row = lo[0:1, :]

    for g in range(SSM_GROUPS):
        bm_t = bm[:, g * SSM_STATE:(g + 1) * SSM_STATE].T.astype(_BF)
        cm_g = cm[:, g * SSM_STATE:(g + 1) * SSM_STATE].astype(_BF)
        cb = _dot(cm_g, bm_t)
        ys = []
        for pp in range(_HPG // 2):
            p = g * (_HPG // 2) + pp
            cols = slice(p * _PAIR, (p + 1) * _PAIR)
            wts = []
            for h in (2 * p, 2 * p + 1):
                seg = la[:, h:h + 1] - la_t[h:h + 1, :]
                wts.append(cb * jnp.exp(jnp.where(tri, seg, -jnp.inf)) * dt_t[h:h + 1, :])
            w2 = jnp.concatenate(wts, axis=1).astype(_BF)
            x_pair = xs[:, cols]
            x2 = jnp.concatenate([jnp.where(lo, x_pair, 0.0), jnp.where(lo, 0.0, x_pair)], axis=0).astype(_BF)
            s_t = sT_ref[:, cols]
            e_pair = jnp.where(lo, e_la[:, 2 * p:2 * p + 1], e_la[:, 2 * p + 1:2 * p + 2])
            y_pair = _dot(w2, x2) + _dot(cm_g, s_t.astype(_BF)) * e_pair
            wd_pair = jnp.where(lo, wend[:, 2 * p:2 * p + 1], wend[:, 2 * p + 1:2 * p + 2])
            el_pair = jnp.where(lo_row, e_last[:, 2 * p:2 * p + 1], e_last[:, 2 * p + 1:2 * p + 2])
            sT_ref[:, cols] = s_t * el_pair + _dot(bm_t, (x_pair * wd_pair).astype(_BF))
            ys.append(y_pair + x_pair * dskip_ref[:, cols])
        gcols = slice(g * _GRP_DIM, (g + 1) * _GRP_DIM)
        gated = jnp.concatenate(ys, axis=1) * _silu(z_ref[rows, gcols].astype(_F32))
        y_ref[rows, D_MODEL + g * _GRP_DIM:D_MODEL + (g + 1) * _GRP_DIM] = _rms(gated, gn_ref[:, gcols]).astype(y_ref.dtype)


_MIX_SUB_CHUNKS = 2


def _mix_ab_prompt(u, v, z, xbc, dt, bsz, seq, ws, bs, cw, cb, dtb, alog, dskip, gn, cast_srcs):
    n_sub = _MIX_SUB_CHUNKS
    nc = seq // (CHUNK * n_sub)
    row = lambda width: pl.BlockSpec((CHUNK * n_sub, width), lambda b, c: (b * nc + c, 0))
    small = [ws, bs, cw, cb, dtb, alog, dskip, gn]
    outs = [jax.ShapeDtypeStruct((bsz * seq, 2 * D_MODEL), _BF),
            jax.ShapeDtypeStruct((bsz, SSM_DIM, SSM_STATE), _F32),
            jax.ShapeDtypeStruct((bsz, SSM_CONV - 1, SSM_CONV_DIM), _F32)]
    in_specs = [row(D_MODEL), row(D_MODEL), row(D_MODEL), row(SSM_CONV_DIM), row(LANES)] + [
        _resident(a.shape) for a in small]
    out_specs = [row(2 * D_MODEL),
                 pl.BlockSpec((1, SSM_DIM, SSM_STATE), lambda b, c: (b, 0, 0)),
                 pl.BlockSpec((1, SSM_CONV - 1, SSM_CONV_DIM), lambda b, c: (b, 0, 0))]
    args = [u, v, z, xbc, dt, *small]
    sides = [_side_cast(src, bsz * nc, lambda b, c: b * nc + c) for src in cast_srcs]
    hosted = [src for src, side in zip(cast_srcs, sides) if side]
    in_specs += [side[0] for side in sides if side]
    out_specs += [side[0] for side in sides if side]
    outs += [side[1] for side in sides if side]
    res = pl.pallas_call(
        functools.partial(_mix_ab_prompt_body, n_sub, len(hosted)), out_shape=tuple(outs), grid=(bsz, nc),
        in_specs=in_specs, out_specs=tuple(out_specs),
        scratch_shapes=[pltpu.VMEM((n_sub, SUBLANES + CHUNK, SSM_CONV_DIM), _F32),
                        pltpu.VMEM((SSM_STATE, SSM_DIM), _F32), pltpu.VMEM((A_GROUPS, CHUNK, CHUNK), _BF)],
        compiler_params=_cparams(2), name="mix_ab_prompt",
    )(*args, *hosted)
    done = iter(res[3:])
    return res[0], res[1], res[2], [next(done) if side else src.astype(_BF) for src, side in zip(cast_srcs, sides)]


_SEQ_BLK = 8


def _mix_ab_sample_body(seq, u_ref, v_ref, z_ref, xbc_ref, dt_ref, e_ref, st_ref, coef_ref, sb_ref, cw_ref, cb_ref,
                        dtb_ref, alog_ref, dskip_ref, gn_ref, y_ref, sto_ref, sh_ref):
    rows = _SEQ_BLK * seq
    t, seq_of_row = _pos_and_seq(rows, seq)
    pad_rows = sh_ref.shape[0]
    sh_ref[...] = jnp.zeros((pad_rows, SSM_CONV_DIM), _F32)

    def shifts(val, deltas):
        width = val.shape[1]
        sh_ref[SUBLANES:SUBLANES + rows, 0:width] = val
        return [sh_ref[SUBLANES - d:SUBLANES - d + rows, 0:width] for d in deltas]

    back = tuple(range(1, seq))

    v = v_ref[...]
    s = coef_ref[0] * v + sb_ref[...]
    for d, vs in zip(back, shifts(v, back)):
        s = s + coef_ref[d] * jnp.where(t >= d, vs, 0.0)
    y_ref[:, 0:D_MODEL] = u_ref[...] * s

    xbc = xbc_ref[...]
    conv = xbc * cw_ref[SSM_CONV - 1:SSM_CONV, :]
    for j, xsft in zip(range(1, SSM_CONV), shifts(xbc, tuple(range(1, SSM_CONV)))):
        conv = conv + (jnp.where(t >= j, xsft, 0.0) + e_ref[j - 1]) * cw_ref[SSM_CONV - 1 - j:SSM_CONV - j, :]
    act = _silu(conv + cb_ref[...])
    xs = act[:, :SSM_DIM]
    bm = act[:, SSM_DIM:SSM_DIM + SSM_GROUPS * SSM_STATE]
    cm = act[:, SSM_DIM + SSM_GROUPS * SSM_STATE:]

    dt = _softplus(dt_ref[...] + dtb_ref[...])
    d_a = dt * (-jnp.exp(alog_ref[...]))
    la = d_a
    for d, sft in zip(back, shifts(d_a, back)):
        la = la + jnp.where(t >= d, sft, 0.0)
    la_back = shifts(la, back)
    dt_back = shifts(dt, back)
    last = jnp.where(t == seq - 1, la, 0.0)
    fwd = shifts(la, tuple(-d for d in back))
    for d, sft in zip(back, fwd):
        last = last + jnp.where(t == seq - 1 - d, sft, 0.0)
    wend = jnp.exp(last - la) * dt
    e_last = jnp.exp(last)
    e_la = jnp.exp(la)

    head_lane = lax.broadcasted_iota(jnp.int32, (rows, LANES), 1)
    bm_back = shifts(bm, back)

    def cb_heads(b_other):
        per_g = [jnp.sum(cm[:, g * SSM_STATE:(g + 1) * SSM_STATE] * b_other[:, g * SSM_STATE:(g + 1) * SSM_STATE],
                         axis=-1, keepdims=True) for g in range(SSM_GROUPS)]
        return jnp.where(head_lane < _HPG, per_g[0], per_g[1])

    gs = [cb_heads(bm) * dt]
    for i, d in enumerate(back):
        gd = cb_heads(bm_back[i]) * jnp.exp(la - la_back[i]) * dt_back[i]
        gs.append(jnp.where(t >= d, gd, 0.0))

    head0 = lax.broadcasted_iota(jnp.int32, (LANES, SSM_DIM), 0) * SSM_HEAD_DIM
    chan = lax.broadcasted_iota(jnp.int32, (LANES, SSM_DIM), 1)
    expand = ((chan >= head0) & (chan < head0 + SSM_HEAD_DIM)).astype(_BF)
    stacked = jnp.concatenate([e_la, wend, e_last] + gs, axis=0)
    full = _dot_sel_rhs(stacked, expand)
    e_la_f, wend_f, e_last_f = full[0:rows], full[rows:2 * rows], full[2 * rows:3 * rows]
    y2 = full[3 * rows:4 * rows] * xs
    for i, (d, xsft) in enumerate(zip(back, shifts(xs, back))):
        y2 = y2 + full[(4 + i) * rows:(5 + i) * rows] * jnp.where(t >= d, xsft, 0.0)

    zpad = jnp.zeros((LANES - rows, SSM_DIM), _F32)
    xw_t = jnp.concatenate([xs * wend_f, zpad], axis=0).T.astype(_BF)
    el_t = jnp.concatenate([e_last_f, zpad], axis=0).T
    row_lane = lax.broadcasted_iota(jnp.int32, (_GRP_DIM, LANES), 1)
    bpad = jnp.zeros((LANES - rows, SSM_STATE), _F32)

    y1 = [jnp.zeros((rows, _GRP_DIM), _F32) for _ in range(SSM_GROUPS)]
    for i in range(_SEQ_BLK):
        mine = seq_of_row == i
        for g in range(SSM_GROUPS):
            gr = slice(g * _GRP_DIM, (g + 1) * _GRP_DIM)
            st = st_ref[i, gr, :]
            b_g = jnp.where(mine, bm[:, g * SSM_STATE:(g + 1) * SSM_STATE], 0.0)
            b_g = jnp.concatenate([b_g, bpad], axis=0).astype(_BF)
            decay = jnp.sum(jnp.where(row_lane == i * seq, el_t[gr, :], 0.0), axis=1, keepdims=True)
            sto_ref[i, gr, :] = st * decay + _dot(xw_t[gr, :], b_g)
            c_g = cm[:, g * SSM_STATE:(g + 1) * SSM_STATE].astype(_BF)
            ch = lax.dot_general(c_g, st.astype(_BF), _NT, preferred_element_type=_F32)
            y1[g] = y1[g] + jnp.where(mine, ch, 0.0)

    for g in range(SSM_GROUPS):
        gr = slice(g * _GRP_DIM, (g + 1) * _GRP_DIM)
        ys = y1[g] * e_la_f[:, gr] + y2[:, gr] + xs[:, gr] * dskip_ref[:, gr]
        gated = ys * _silu(z_ref[:, gr])
        y_ref[:, D_MODEL + g * _GRP_DIM:D_MODEL + (g + 1) * _GRP_DIM] = _rms(gated, gn_ref[:, gr])


def _mix_ab_sample(u, v, z, xbc, dt, e_buf, state, seq, coef, sb, cw, cb, dtb, alog, dskip, gn):
    bsz = state.shape[0]
    rows = _SEQ_BLK * seq
    row = lambda width: pl.BlockSpec((rows, width), lambda i: (i, 0))
    small = [coef, sb, cw, cb, dtb, alog, dskip, gn]
    st_spec = pl.BlockSpec((_SEQ_BLK, SSM_DIM, SSM_STATE), lambda i: (i, 0, 0))
    outs = (jax.ShapeDtypeStruct((bsz * seq, 2 * D_MODEL), _F32), jax.ShapeDtypeStruct(state.shape, _F32))
    return pl.pallas_call(
        functools.partial(_mix_ab_sample_body, seq), out_shape=outs, grid=(bsz // _SEQ_BLK,),
        in_specs=[row(D_MODEL), row(D_MODEL), row(D_MODEL), row(SSM_CONV_DIM), row(LANES),
                  pl.BlockSpec((SSM_CONV - 1, rows, SSM_CONV_DIM), lambda i: (0, i, 0)), st_spec]
        + [_resident(a.shape) for a in small],
        out_specs=(row(2 * D_MODEL), st_spec),
        scratch_shapes=[pltpu.VMEM((rows + 2 * SUBLANES, SSM_CONV_DIM), _F32)],
        compiler_params=_cparams(1), name="mix_ab_sample",
    )(u, v, z, xbc, dt, e_buf, state, *small)


def _side_cast(src, n_steps, step_of):
    n_e, rows, cols = src.shape
    per_e = n_steps // n_e
    if per_e == 0 or n_steps % n_e or rows % per_e or (rows // per_e) % _BF16_ROWS:
        return None
    blk = (1, rows // per_e, cols)
    imap = lambda *g: (step_of(*g) // per_e, step_of(*g) % per_e, 0)
    return pl.BlockSpec(blk, imap), jax.ShapeDtypeStruct(src.shape, _BF)


def _out_q_body(a_ref, x_ref, wo_ref, gm_ref, wq_ref, xo_ref, q_ref):
    xn = x_ref[...] + _dot(a_ref[...].astype(_BF), wo_ref[...])
    xo_ref[...] = xn
    q_ref[...] = _dot(_rms(xn, gm_ref[...]).astype(_BF), wq_ref[...]).astype(q_ref.dtype)


def _out_q(a, x2d, wo, gm, wq, q_dtype, tm):
    m = x2d.shape[0]
    row = lambda width: pl.BlockSpec((tm, width), lambda i: (i, 0))
    outs = (jax.ShapeDtypeStruct((m, D_MODEL), _F32), jax.ShapeDtypeStruct((m, D_MODEL), q_dtype))
    return pl.pallas_call(
        _out_q_body, out_shape=outs, grid=(m // tm,),
        in_specs=[row(a.shape[1]), row(D_MODEL), _resident(wo.shape), _resident(gm.shape), _resident(wq.shape)],
        out_specs=(row(D_MODEL), row(D_MODEL)), compiler_params=_cparams(1), name="out_q",
    )(a, x2d, wo, gm, wq)


_ATT_SCALE = MEM_HEAD_DIM ** -0.5


def _attend(q, k, v):
    outs = []
    for h in range(MEM_HEADS):
        sl = slice(h * MEM_HEAD_DIM, (h + 1) * MEM_HEAD_DIM)
        s = lax.dot_general(q[:, sl], k[:, sl], _NT, preferred_element_type=_F32) * _ATT_SCALE
        p = jnp.exp(s - jnp.max(s, axis=-1, keepdims=True))
        p = p / jnp.sum(p, axis=-1, keepdims=True)
        outs.append(_dot(p.astype(_BF), v[:, sl]))
    return jnp.concatenate(outs, axis=1)


def _attn_prompt_body(q_ref, k_ref, v_ref, x_ref, wo_ref, o_ref):
    o = _attend(q_ref[...], k_ref[0], v_ref[0])
    o_ref[...] = x_ref[...] + _dot(o.astype(_BF), wo_ref[...])


def _attn_prompt(q, k, v, x2d, wo, layer, bsz, seq, tq):
    nt = seq // tq
    row = pl.BlockSpec((tq, D_MODEL), lambda b, j: (b * nt + j, 0))
    kv = pl.BlockSpec((1, N_MEM, D_MODEL), lambda b, j: (layer * bsz + b, 0, 0))
    return pl.pallas_call(
        _attn_prompt_body, out_shape=jax.ShapeDtypeStruct(x2d.shape, _F32), grid=(bsz, nt),
        in_specs=[row, kv, kv, row, _resident(wo.shape)], out_specs=row,
        compiler_params=_cparams(2), name="attn_prompt",
    )(q, k, v, x2d, wo)


def _attn_sample_body(seq, q_ref, k_ref, v_ref, x_ref, wo_ref, o_ref, o_scr):
    per_tile = SUBLANES // seq
    _, seq_of_row = _pos_and_seq(SUBLANES, seq)
    n_lt = MEM_HEAD_DIM // LANES
    rows_mh = N_MEM * MEM_HEADS
    col = lax.broadcasted_iota(jnp.int32, (MEM_HEADS * SUBLANES, rows_mh), 1)
    row = lax.broadcasted_iota(jnp.int32, (MEM_HEADS * SUBLANES, rows_mh), 0)
    same_head = (col & (MEM_HEADS - 1)) == (row >> (SUBLANES.bit_length() - 1))
    for tile in range(_SEQ_BLK // per_tile):
        q8 = q_ref[tile * SUBLANES:(tile + 1) * SUBLANES, :].astype(_BF)
        o = None
        for j in range(per_tile):
            i = tile * per_tile + j
            s = None
            for lt in range(n_lt):
                qh = jnp.concatenate([q8[:, h * MEM_HEAD_DIM + lt * LANES:h * MEM_HEAD_DIM + (lt + 1) * LANES]
                                      for h in range(MEM_HEADS)], axis=0)
                k_lt = k_ref[i, :, :, lt * LANES:(lt + 1) * LANES].reshape(rows_mh, LANES).astype(_BF)
                part = lax.dot_general(qh, k_lt, _NT, preferred_element_type=_F32)
                s = part if s is None else s + part
            s = jnp.where(same_head, s * _ATT_SCALE, -jnp.inf)
            p = jnp.exp(s - jnp.max(s, axis=-1, keepdims=True))
            p = (p / jnp.sum(p, axis=-1, keepdims=True)).astype(_BF)
            pieces = [None] * (MEM_HEADS * n_lt)
            for lt in range(n_lt):
                v_lt = v_ref[i, :, :, lt * LANES:(lt + 1) * LANES].reshape(rows_mh, LANES).astype(_BF)
                o_lt = _dot(p, v_lt)
                for h in range(MEM_HEADS):
                    pieces[h * n_lt + lt] = o_lt[h * SUBLANES:(h + 1) * SUBLANES, :]
            oi = jnp.concatenate(pieces, axis=1)
            o = oi if o is None else jnp.where(seq_of_row == j, oi, o)
        o_scr[tile * SUBLANES:(tile + 1) * SUBLANES, :] = o
    o_ref[...] = x_ref[...] + _dot(o_scr[...].astype(_BF), wo_ref[...])


def _attn_sample(q, k, v, x2d, wo, layer, bsz, seq):
    rows = _SEQ_BLK * seq
    nb = bsz // _SEQ_BLK
    row = pl.BlockSpec((rows, D_MODEL), lambda i: (i, 0))
    kv = pl.BlockSpec((_SEQ_BLK, N_MEM, MEM_HEADS, MEM_HEAD_DIM), lambda i: (layer * nb + i, 0, 0, 0))
    return pl.pallas_call(
        functools.partial(_attn_sample_body, seq), out_shape=jax.ShapeDtypeStruct(x2d.shape, _F32), grid=(nb,),
        in_specs=[row, kv, kv, row, _resident(wo.shape)], out_specs=row,
        scratch_shapes=[pltpu.VMEM((rows, D_MODEL), _F32)],
        compiler_params=_cparams(1), name="attn_sample",
    )(q, k, v, x2d, wo)


_FF_SUB = 512
_MOE_TILE = 1024


def _swiglu_chunk(h, wg_ref, wu_ref, wd_ref):
    width = wg_ref.shape[2]
    hids = []
    for lo in range(0, width, _FF_SUB):
        hi = min(lo + _FF_SUB, width)
        hids.append((_silu(_dot(h, wg_ref[0, :, lo:hi])) * _dot(h, wu_ref[0, :, lo:hi])).astype(_BF))
    return _dot(jnp.concatenate(hids, axis=1), wd_ref[0])


def _ffn_body(x_ref, g_ref, wg_ref, wu_ref, wd_ref, o_ref):
    x = x_ref[...]
    o_ref[...] = x + _swiglu_chunk(_rms(x, g_ref[...]).astype(_BF), wg_ref, wu_ref, wd_ref)


def _ffn(x2d, g, wg, wu, wd, tm):
    m = x2d.shape[0]
    row = pl.BlockSpec((tm, D_MODEL), lambda i: (i, 0))
    return pl.pallas_call(
        _ffn_body, out_shape=jax.ShapeDtypeStruct(x2d.shape, _F32), grid=(m // tm,),
        in_specs=[row, _resident(g.shape), _resident(wg.shape), _resident(wu.shape), _resident(wd.shape)],
        out_specs=row, compiler_params=_cparams(1), name="ffn_dense",
    )(x2d, g, wg, wu, wd)


_MOE_PICK_PROB = 2.0 / N_EXPERTS
_MOE_CAP_STEPS = (1.0, 1.25, 1.5)


_BF16_ROWS = 16
_MOE_DENSE_ROWS = 256


def _moe_capacities(tm):
    mean = tm * _MOE_PICK_PROB
    return tuple(int(-(-(mean * s) // _BF16_ROWS) * _BF16_ROWS) for s in _MOE_CAP_STEPS)


def _moe_body(caps, final_norm, x_ref, g_ref, wr_ref, wg_ref, wu_ref, wd_ref, gf_ref, o_ref, hf_ref, comb_ref,
              rank_ref, rank_t_ref, cnt_ref):
    e = pl.program_id(1)
    tm = x_ref.shape[0]
    lane = lax.broadcasted_iota(jnp.int32, (tm, LANES), 1)

    @pl.when(e == 0)
    def _():
        x = x_ref[...]
        hf = _rms(x, g_ref[...])
        hf_hi = hf.astype(_BF)
        hf_ref[...] = hf_hi
        o_ref[...] = x
        hf_lo = (hf - hf_hi.astype(_F32)).astype(_BF)
        wr = wr_ref[...]
        wr_hi = wr.astype(_BF)
        wr_lo = (wr - wr_hi.astype(_F32)).astype(_BF)
        logits = _dot(hf_hi, wr_hi) + (_dot(hf_hi, wr_lo) + _dot(hf_lo, wr_hi))
        lg = jnp.where(lane < N_EXPERTS, logits, -jnp.inf)
        v1 = jnp.max(lg, axis=-1, keepdims=True)
        i1 = jnp.min(jnp.where(lg == v1, lane, LANES), axis=-1, keepdims=True)
        lg2 = jnp.where(lane == i1, -jnp.inf, lg)
        v2 = jnp.max(lg2, axis=-1, keepdims=True)
        i2 = jnp.min(jnp.where(lg2 == v2, lane, LANES), axis=-1, keepdims=True)
        e2 = jnp.exp(v2 - v1)
        comb_ref[...] = jnp.where(lane == i1, 1.0 / (1.0 + e2), 0.0) + jnp.where(lane == i2, e2 / (1.0 + e2), 0.0)
        picked = (lane == i1) | (lane == i2)
        picked_b = picked.astype(_BF)
        blk_r = lax.broadcasted_iota(jnp.int32, (LANES, LANES), 0)
        blk_c = lax.broadcasted_iota(jnp.int32, (LANES, LANES), 1)
        earlier = (blk_c < blk_r).astype(_BF)
        running = jnp.zeros((1, LANES), _F32)
        for b in range(tm // LANES):
            rows = slice(b * LANES, (b + 1) * LANES)
            pb = picked_b[rows, :]
            rank_b = jnp.where(picked[rows, :], _dot(earlier, pb) + running, -1.0)
            rank_ref[rows, :] = rank_b
            rank_t_ref[:, rows] = rank_b.T
            running = running + jnp.sum(pb.astype(_F32), axis=0, keepdims=True)
        cnt_ref[...] = running

    lane_row = lax.broadcasted_iota(jnp.int32, (1, LANES), 1)
    count = jnp.sum(jnp.where(lane_row == e, cnt_ref[...], 0.0)).astype(jnp.int32)

    def compact_path(cap):
        cap_pad = -(-cap // LANES) * LANES
        rank_row = rank_t_ref[pl.ds(e, 1), :]
        slot_col = lax.broadcasted_iota(jnp.int32, (cap, 1), 0).astype(_F32)
        pick = (rank_row == slot_col).astype(_BF)
        xs = _dot(pick, hf_ref[...]).astype(_BF)
        y = _swiglu_chunk(xs, wg_ref, wu_ref, wd_ref).astype(_BF)
        if cap_pad > cap:
            y = jnp.concatenate([y, jnp.zeros((cap_pad - cap, D_MODEL), _BF)], axis=0)
        rank_col = jnp.sum(jnp.where(lane == e, rank_ref[...], 0.0), axis=1, keepdims=True)
        comb_col = jnp.sum(jnp.where(lane == e, comb_ref[...], 0.0), axis=1, keepdims=True)
        slot_row = lax.broadcasted_iota(jnp.int32, (1, cap_pad), 1).astype(_F32)
        put = (rank_col == slot_row).astype(_BF)
        o_ref[...] += comb_col * _dot(put, y)

    lower = 0
    for cap in caps:
        pl.when((count > lower) & (count <= cap))(functools.partial(compact_path, cap))
        lower = cap

    @pl.when(count > caps[-1])
    def _():
        blk = min(tm, _MOE_DENSE_ROWS)
        blk_lane = lax.broadcasted_iota(jnp.int32, (blk, LANES), 1)

        def dense_rows(r, carry):
            rows = pl.ds(pl.multiple_of(r * blk, blk), blk)
            comb_col = jnp.sum(jnp.where(blk_lane == e, comb_ref[rows, :], 0.0), axis=1, keepdims=True)
            o_ref[rows, :] += comb_col * _swiglu_chunk(hf_ref[rows, :], wg_ref, wu_ref, wd_ref)
            return carry

        lax.fori_loop(0, tm // blk, dense_rows, 0)

    if final_norm:
        @pl.when(e == pl.num_programs(1) - 1)
        def _():
            o_ref[...] = _rms(o_ref[...], gf_ref[...])


def _moe(x2d, g, wr, wg, wu, wd, gf, final_norm, tm):
    m = x2d.shape[0]
    n_e, _, d_ff = wg.shape
    assert tm % LANES == 0 and m % tm == 0, (m, tm)
    caps = _moe_capacities(tm)
    row_once = pl.BlockSpec((tm, D_MODEL), lambda i, e: (i, 0), pipeline_mode=pl.Buffered(1))
    up = pl.BlockSpec((1, D_MODEL, d_ff), lambda i, e: (e, 0, 0))
    down = pl.BlockSpec((1, d_ff, D_MODEL), lambda i, e: (e, 0, 0))
    return pl.pallas_call(
        functools.partial(_moe_body, caps, final_norm), out_shape=jax.ShapeDtypeStruct(x2d.shape, _F32),
        grid=(m // tm, n_e),
        in_specs=[row_once, _resident(g.shape), _resident(wr.shape), up, up, down, _resident(gf.shape)],
        out_specs=pl.BlockSpec((tm, D_MODEL), lambda i, e: (i, 0)),
        scratch_shapes=[pltpu.VMEM((tm, D_MODEL), _BF), pltpu.VMEM((tm, LANES), _F32), pltpu.VMEM((tm, LANES), _F32),
                        pltpu.VMEM((LANES, tm), _F32), pltpu.VMEM((1, LANES), _F32)],
        compiler_params=_cparams(2, _MOE_VMEM_LIMIT), name="moe",
    )(x2d, g, wr, wg, wu, wd, gf)


_SC_CONV = 3


def _mix_c_body(per_seq, seq, x_ref, g_ref, win_ref, cw_ref, e_ref, wout_ref, gm_ref, wq_ref, xo_ref, q_ref, p_ref,
                pp_ref):
    tm = x_ref.shape[0]
    x = x_ref[...]
    hn = _rms(x, g_ref[...]).astype(_BF)
    bg = _dot(hn, win_ref[:, 0:D_MODEL])
    p = _dot(hn, win_ref[:, D_MODEL:2 * D_MODEL]) * _dot(hn, win_ref[:, 2 * D_MODEL:3 * D_MODEL])
    if per_seq:
        @pl.when(pl.program_id(1) == 0)
        def _():
            pp_ref[0:SUBLANES, :] = jnp.zeros((SUBLANES, D_MODEL), _F32)
    else:
        pp_ref[0:SUBLANES, :] = jnp.zeros((SUBLANES, D_MODEL), _F32)
    pp_ref[SUBLANES:SUBLANES + tm, :] = p
    p1 = pp_ref[SUBLANES - 1:SUBLANES - 1 + tm, :]
    p2 = pp_ref[SUBLANES - 2:SUBLANES - 2 + tm, :]
    if per_seq:
        pp_ref[0:SUBLANES, :] = p[tm - SUBLANES:tm, :]
    else:
        t, _ = _pos_and_seq(tm, seq)
        p1 = jnp.where(t >= 1, p1, 0.0) + e_ref[0]
        p2 = jnp.where(t >= 2, p2, 0.0) + e_ref[1]
    conv = p2 * cw_ref[0:1, :] + p1 * cw_ref[1:2, :] + p * cw_ref[2:3, :]
    xn = x + _dot((bg * conv).astype(_BF), wout_ref[...])
    xo_ref[...] = xn
    q_ref[...] = _dot(_rms(xn, gm_ref[...]).astype(_BF), wq_ref[...]).astype(q_ref.dtype)
    if per_seq:
        @pl.when(pl.program_id(1) == pl.num_programs(1) - 1)
        def _():
            p_ref[0] = pp_ref[SUBLANES + tm - (_SC_CONV - 1):SUBLANES + tm, :]
    else:
        p_ref[...] = p


def _mix_c(x2d, g, win, cw, e_buf, wout, gm, wq, q_dtype, per_seq, bsz, seq, tm):
    m = x2d.shape[0]
    if per_seq:
        nt = seq // tm
        grid = (bsz, nt)
        imap = lambda b, j: (b * nt + j, 0)
        e_spec = pl.BlockSpec((_SC_CONV - 1, SUBLANES, D_MODEL), lambda b, j: (0, 0, 0))
        p_shape = jax.ShapeDtypeStruct((bsz, _SC_CONV - 1, D_MODEL), _F32)
        p_spec = pl.BlockSpec((1, _SC_CONV - 1, D_MODEL), lambda b, j: (b, 0, 0))
    else:
        grid = (m // tm, 1)
        imap = lambda i, j: (i, 0)
        e_spec = pl.BlockSpec((_SC_CONV - 1, tm, D_MODEL), lambda i, j: (0, i, 0))
        p_shape = jax.ShapeDtypeStruct((m, D_MODEL), _F32)
        p_spec = pl.BlockSpec((tm, D_MODEL), imap)
    row = pl.BlockSpec((tm, D_MODEL), imap)
    outs = (jax.ShapeDtypeStruct((m, D_MODEL), _F32), jax.ShapeDtypeStruct((m, D_MODEL), q_dtype), p_shape)
    return pl.pallas_call(
        functools.partial(_mix_c_body, per_seq, seq), out_shape=outs, grid=grid,
        in_specs=[row, _resident(g.shape), _resident(win.shape), _resident(cw.shape), e_spec, _resident(wout.shape),
                  _resident(gm.shape), _resident(wq.shape)],
        out_specs=(row, row, p_spec),
        scratch_shapes=[pltpu.VMEM((SUBLANES + tm, D_MODEL), _F32)],
        compiler_params=_cparams(2), name="mix_c",
    )(x2d, g, win, cw, e_buf, wout, gm, wq)


def _expand_buf(buf, seq, n_back):
    bsz, km1, ch = buf.shape
    outs = []
    for j in range(1, n_back + 1):
        rows = [buf[:, km1 - j + t] if t < j else jnp.zeros((bsz, ch), buf.dtype) for t in range(seq)]
        outs.append(jnp.stack(rows, axis=1).reshape(bsz * seq, ch))
    return jnp.stack(outs)


def _row(vec, width=None):
    vec = vec.reshape(1, -1).astype(_F32)
    if width is not None and vec.shape[1] < width:
        vec = jnp.pad(vec, ((0, 0), (0, width - vec.shape[1])))
    return vec


def _trunk(x, k_all, v_all, prompt, states, w):
    bsz, seq, _ = x.shape
    m = bsz * seq
    x2d = x.reshape(m, D_MODEL)
    tm = min(_TOKEN_TILE, m)
    dskip = _row(jnp.repeat(w["d_skip"][0], SSM_HEAD_DIM))
    common = (w["conv_w_ssm"][0], _row(w["conv_b_ssm"][0]), _row(w["dt_bias"][0], LANES), _row(w["a_log"][0], LANES),
              dskip, _row(w["g_ssm_norm"][0]))

    act_dtype = _BF if prompt else _F32
    u, v, z, xbc, dt = _inproj_ab(x2d, _row(w["g_mix"][0]), w["w_in_ab"], act_dtype, tm)
    experts = w["experts"]
    if prompt:
        y_ab, ssm_state, conv_state, experts = _mix_ab_prompt(
            u, v, z, xbc, dt, bsz, seq, w["w_spatial"][0], w["b_spatial"][0][:, :, None], *common, w["experts"])
        v_out = None
    else:
        ws = w["w_spatial"][0][:, :seq, :seq]
        coef = []
        for d in range(seq):
            per_t = [ws[:, t, t - d] if t >= d else jnp.zeros((A_GROUPS,), _F32) for t in range(seq)]
            coef.append(jnp.tile(jnp.repeat(jnp.stack(per_t), A_HEAD, axis=1), (_SEQ_BLK, 1)))
        sb = jnp.tile(jnp.repeat(w["b_spatial"][0][:, :seq].T, A_HEAD, axis=1), (_SEQ_BLK, 1))
        e_buf = _expand_buf(states["ssm_conv"], seq, SSM_CONV - 1)
        y_ab, ssm_state = _mix_ab_sample(u, v, z, xbc, dt, e_buf, states["ssm"].reshape(bsz, SSM_DIM, SSM_STATE), seq,
                                         jnp.stack(coef), sb, *common)
        conv_state = xbc.reshape(bsz, seq, SSM_CONV_DIM)[:, seq - (SSM_CONV - 1):]
        v_out = v.reshape(bsz, seq, D_MODEL)
    x2d, q = _out_q(y_ab, x2d, w["w_out_ab"], _row(w["g_mem"][0]), w["w_mem_q"][0], act_dtype, tm)

    def attend(q, x2d, layer):
        if prompt:
            return _attn_prompt(q, k_all, v_all, x2d, w["w_mem_o"][layer], layer, bsz, seq, tm)
        return _attn_sample(q, k_all, v_all, x2d, w["w_mem_o"][layer], layer, bsz, seq)

    x2d = attend(q, x2d, 0)
    x2d = _ffn(x2d, _row(w["g_ffn"][0]), w["w_ffn_gate"], w["w_ffn_up"], w["w_ffn_down"], min(_MOE_TILE, m))

    if prompt:
        e_buf = jnp.zeros((_SC_CONV - 1, SUBLANES, D_MODEL), _F32)
        x2d, q, sconv_state = _mix_c(x2d, _row(w["g_mix"][1]), w["w_in_c"], w["conv_w_c"][0], e_buf, w["w_out_c"],
                                     _row(w["g_mem"][1]), w["w_mem_q"][1], act_dtype, True, bsz, seq, tm)
    else:
        e_buf = _expand_buf(states["sconv"], seq, _SC_CONV - 1)
        x2d, q, p = _mix_c(x2d, _row(w["g_mix"][1]), w["w_in_c"], w["conv_w_c"][0], e_buf, w["w_out_c"],
                           _row(w["g_mem"][1]), w["w_mem_q"][1], act_dtype, False, bsz, seq, tm)
        sconv_state = p.reshape(bsz, seq, D_MODEL)[:, seq - (_SC_CONV - 1):]
    x2d = attend(q, x2d, 1)
    x2d = _moe(x2d, _row(w["g_ffn"][1]), w["w_router"], *experts, _row(w["g_final"]), True, min(_MOE_TILE, m))
    y = x2d.reshape(bsz, seq, D_MODEL)
    ssm_state = ssm_state.reshape(1, bsz, SSM_HEADS, SSM_HEAD_DIM, SSM_STATE)
    return y, ssm_state, conv_state[None], sconv_state[None], v_out, experts


def kernel(x_prompt, x_sample, mem_prompt, state_ssm, state_ssm_conv, state_sconv, cache_mem_k, cache_mem_v, g_mix, g_mem, g_ffn, g_final, w_in_ab, w_spatial, b_spatial, conv_w_ssm, conv_b_ssm, dt_bias, a_log, d_skip, g_ssm_norm, w_out_ab, w_ffn_gate, w_ffn_up, w_ffn_down, w_in_c, conv_w_c, w_out_c, w_router, w_exp_gate, w_exp_up, w_exp_down, w_mem_q, w_mem_k, w_mem_v, w_mem_o):
    depth = w_mem_q.shape[0]
    bp = x_prompt.shape[0]
    bs = x_sample.shape[0]
    bf = lambda a: a.astype(_BF)
    w = dict(
        g_mix=g_mix, g_mem=g_mem, g_ffn=g_ffn, g_final=g_final, w_spatial=w_spatial, b_spatial=b_spatial,
        conv_w_ssm=conv_w_ssm, conv_b_ssm=conv_b_ssm, dt_bias=dt_bias, a_log=a_log, d_skip=d_skip, g_ssm_norm=g_ssm_norm,
        conv_w_c=conv_w_c,
        w_in_ab=jnp.pad(bf(w_in_ab[0]), ((0, 0), (0, LANES - SSM_HEADS))),
        w_out_ab=bf(w_out_ab[0]), w_ffn_gate=bf(w_ffn_gate), w_ffn_up=bf(w_ffn_up), w_ffn_down=bf(w_ffn_down),
        w_in_c=bf(w_in_c[0]), w_out_c=bf(w_out_c[0]),
        w_router=jnp.pad(w_router[0], ((0, 0), (0, LANES - N_EXPERTS))),
        experts=(w_exp_gate[0], w_exp_up[0], w_exp_down[0]),
        w_mem_q=bf(w_mem_q), w_mem_o=bf(w_mem_o),
    )
    mem_k_p, mem_v_p, k_bf, v_bf = _mem_kv(mem_prompt, bf(w_mem_k), bf(w_mem_v))
    y_p, ssm_p, ssmconv_p, sconv_p, _, experts_bf = _trunk(x_prompt, k_bf, v_bf, True, None, w)
    w = dict(w, experts=experts_bf)
    states = dict(ssm=state_ssm[0], ssm_conv=state_ssm_conv[0], sconv=state_sconv[0])
    y_s, ssm_s, ssmconv_s, sconv_s, v_s, _ = _trunk(
        x_sample, cache_mem_k.reshape(depth * bs, N_MEM, MEM_HEADS, MEM_HEAD_DIM),
        cache_mem_v.reshape(depth * bs, N_MEM, MEM_HEADS, MEM_HEAD_DIM), False, states, w)
    return (y_p, y_s, ssm_p, ssm_s, ssmconv_p, ssmconv_s, sconv_p, sconv_s, mem_k_p, mem_v_p, v_s[None])
```

```python
import functools

import jax
import jax.numpy as jnp
from jax import lax
from jax.experimental import pallas as pl
from jax.experimental.pallas import tpu as pltpu

D_MODEL = 1024
EPS = 1e-6
CHUNK = 128
A_GROUPS = 8
A_HEAD = D_MODEL // A_GROUPS
SSM_HEADS = 16
SSM_HEAD_DIM = 64
SSM_GROUPS = 2
SSM_STATE = 128
SSM_CONV = 4
SSM_DIM = D_MODEL
SSM_CONV_DIM = SSM_DIM + 2 * SSM_GROUPS * SSM_STATE
N_MEM = 256
MEM_HEADS = 4
MEM_HEAD_DIM = D_MODEL // MEM_HEADS
N_EXPERTS = 8

LANES = 128
SUBLANES = 8
VMEM_LIMIT = 56 * 1024 * 1024
_MOE_VMEM_LIMIT = 58 * 1024 * 1024
_TOKEN_TILE = 1024

_BF = jnp.bfloat16
_F32 = jnp.float32
_NT = (((1,), (1,)), ((), ()))


def _cparams(n_axes, vmem_limit=VMEM_LIMIT):
    return pltpu.CompilerParams(dimension_semantics=("arbitrary",) * n_axes, vmem_limit_bytes=vmem_limit)


def _resident(shape):
    nd = len(shape)
    return pl.BlockSpec(shape, lambda *_: (0,) * nd, pipeline_mode=pl.Buffered(1))


def _rms(x, g):
    ms = jnp.mean(x * x, axis=-1, keepdims=True)
    return x * lax.rsqrt(ms + EPS) * g


def _silu(x):
    return x * jax.nn.sigmoid(x)


def _dot(a, b):
    return jnp.dot(a, b, preferred_element_type=_F32)


def _softplus(x):
    return jnp.maximum(x, 0.0) + jnp.log(1.0 + jnp.exp(-jnp.abs(x)))


def _split3(x):
    p1 = x.astype(_BF)
    r1 = x - p1.astype(_F32)
    p2 = r1.astype(_BF)
    p3 = (r1 - p2.astype(_F32)).astype(_BF)
    return p1, p2, p3


def _dot_sel_lhs(sel, x):
    p1, p2, p3 = _split3(x)
    return _dot(sel, p1) + _dot(sel, p2) + _dot(sel, p3)


def _dot_sel_rhs(x, sel):
    p1, p2, p3 = _split3(x)
    return _dot(p1, sel) + _dot(p2, sel) + _dot(p3, sel)


def _pos_and_seq(rows, seq):
    r = lax.broadcasted_iota(jnp.int32, (rows, 1), 0)
    if seq & (seq - 1) == 0:
        return r & (seq - 1), r >> (seq.bit_length() - 1)
    return lax.rem(r, seq), lax.div(r, seq)


def _kv_body(m_ref, wk_ref, wv_ref, k_ref, v_ref, kb_ref, vb_ref):
    m = m_ref[0].astype(_BF)
    for w_ref, o_ref, ob_ref in ((wk_ref, k_ref, kb_ref), (wv_ref, v_ref, vb_ref)):
        res = _dot(m, w_ref[0])
        ob_ref[0] = res.astype(_BF)
        for h in range(MEM_HEADS):
            o_ref[0, 0, :, h, :] = res[:, h * MEM_HEAD_DIM:(h + 1) * MEM_HEAD_DIM]


def _mem_kv(mem, wk, wv):
    bsz, n, d = mem.shape
    depth = wk.shape[0]
    out = jax.ShapeDtypeStruct((depth, bsz, n, MEM_HEADS, MEM_HEAD_DIM), _F32)
    out_b = jax.ShapeDtypeStruct((depth * bsz, n, d), _BF)
    wspec = pl.BlockSpec((1, d, d), lambda l, b: (l, 0, 0))
    ospec = pl.BlockSpec((1, 1, n, MEM_HEADS, MEM_HEAD_DIM), lambda l, b: (l, b, 0, 0, 0))
    bspec = pl.BlockSpec((1, n, d), lambda l, b: (l * bsz + b, 0, 0))
    return pl.pallas_call(
        _kv_body, out_shape=(out, out, out_b, out_b), grid=(depth, bsz),
        in_specs=[pl.BlockSpec((1, n, d), lambda l, b: (b, 0, 0)), wspec, wspec],
        out_specs=(ospec, ospec, bspec, bspec), compiler_params=_cparams(2), name="mem_kv",
    )(mem, wk, wv)


_PROJ_CHUNK = 512


def _inproj_ab_body(x_ref, g_ref, w_ref, u_ref, v_ref, z_ref, xbc_ref, dt_ref):
    hn = _rms(x_ref[...], g_ref[...]).astype(_BF)
    col = 0
    for ref, width, act in ((u_ref, D_MODEL, True), (v_ref, D_MODEL, True), (z_ref, D_MODEL, False),
                            (xbc_ref, SSM_CONV_DIM, False)):
        for c in range(width // _PROJ_CHUNK):
            r = _dot(hn, w_ref[:, col:col + _PROJ_CHUNK])
            if act:
                r = jax.nn.gelu(r)
            ref[:, c * _PROJ_CHUNK:(c + 1) * _PROJ_CHUNK] = r.astype(ref.dtype)
            col += _PROJ_CHUNK
    dt_ref[...] = _dot(hn, w_ref[:, col:col + LANES])


def _inproj_ab(x2d, g, w_in, act_dtype, tm):
    m = x2d.shape[0]
    row = lambda width: pl.BlockSpec((tm, width), lambda i: (i, 0))
    outs = (jax.ShapeDtypeStruct((m, D_MODEL), act_dtype),) * 3 + (
        jax.ShapeDtypeStruct((m, SSM_CONV_DIM), _F32), jax.ShapeDtypeStruct((m, LANES), _F32))
    return pl.pallas_call(
        _inproj_ab_body, out_shape=outs, grid=(m // tm,),
        in_specs=[row(D_MODEL), _resident(g.shape), _resident(w_in.shape)],
        out_specs=(row(D_MODEL), row(D_MODEL), row(D_MODEL), row(SSM_CONV_DIM), row(LANES)),
        compiler_params=_cparams(1), name="inproj_ab",
    )(x2d, g, w_in)


_PAIR = 2 * SSM_HEAD_DIM
_HPG = SSM_HEADS // SSM_GROUPS
_GRP_DIM = SSM_DIM // SSM_GROUPS


def _mix_ab_prompt_body(n_sub, n_side, u_ref, v_ref, z_ref, xbc_ref, dt_ref, ws_ref, bs_ref, cw_ref, cb_ref, dtb_ref,
                        alog_ref, dskip_ref, gn_ref, *rest):
    y_ref, st_ref, cst_ref = rest[n_side:n_side + 3]
    xp_ref, sT_ref, wsb_ref = rest[2 * n_side + 3:]
    for src_ref, dst_ref in zip(rest[:n_side], rest[n_side + 3:2 * n_side + 3]):
        dst_ref[...] = src_ref[...].astype(_BF)
    q = CHUNK
    c = pl.program_id(1)
    tri = lax.broadcasted_iota(jnp.int32, (q, q), 0) >= lax.broadcasted_iota(jnp.int32, (q, q), 1)

    @pl.when((pl.program_id(0) == 0) & (c == 0))
    def _():
        for g in range(A_GROUPS):
            wsb_ref[g] = jnp.where(tri, ws_ref[g], 0.0).astype(_BF)

    @pl.when(c == 0)
    def _():
        xp_ref[0, 0:SUBLANES, :] = jnp.zeros((SUBLANES, SSM_CONV_DIM), _F32)
        sT_ref[...] = jnp.zeros_like(sT_ref)

    for sub in range(n_sub):
        _mix_ab_prompt_chunk(sub, n_sub, tri, u_ref, v_ref, z_ref, xbc_ref, dt_ref, bs_ref, cw_ref, cb_ref, dtb_ref,
                             alog_ref, dskip_ref, gn_ref, y_ref, xp_ref, sT_ref, wsb_ref)

    @pl.when(c == pl.num_programs(1) - 1)
    def _():
        for p in range(SSM_HEADS // 2):
            st_ref[0, p * _PAIR:(p + 1) * _PAIR, :] = sT_ref[:, p * _PAIR:(p + 1) * _PAIR].T
        cst_ref[0] = xp_ref[n_sub - 1, SUBLANES + q - (SSM_CONV - 1):SUBLANES + q, :]


def _mix_ab_prompt_chunk(sub, n_sub, tri, u_ref, v_ref, z_ref, xbc_ref, dt_ref, bs_ref, cw_ref, cb_ref, dtb_ref,
                         alog_ref, dskip_ref, gn_ref, y_ref, xp_ref, sT_ref, wsb_ref):
    q = CHUNK
    rows = slice(sub * q, (sub + 1) * q)

    for g in range(A_GROUPS):
        sl = slice(g * A_HEAD, (g + 1) * A_HEAD)
        s = _dot(wsb_ref[g], v_ref[rows, sl].astype(_BF)) + bs_ref[g]
        y_ref[rows, sl] = (u_ref[rows, sl].astype(_F32) * s).astype(y_ref.dtype)

    xbc = xbc_ref[rows, :]
    xp_ref[sub, SUBLANES:SUBLANES + q, :] = xbc
    base = SUBLANES - (SSM_CONV - 1)
    conv = xp_ref[sub, base:base + q, :] * cw_ref[0:1, :]
    for k in range(1, SSM_CONV):
        conv = conv + xp_ref[sub, base + k:base + k + q, :] * cw_ref[k:k + 1, :]
    xp_ref[(sub + 1) % n_sub, 0:SUBLANES, :] = xbc[q - SUBLANES:q, :]
    act = _silu(conv + cb_ref[...])
    xs = act[:, :SSM_DIM]
    bm = act[:, SSM_DIM:SSM_DIM + SSM_GROUPS * SSM_STATE]
    cm = act[:, SSM_DIM + SSM_GROUPS * SSM_STATE:]

    dt = _softplus(dt_ref[rows, :] + dtb_ref[...])
    d_a = dt * (-jnp.exp(alog_ref[...]))
    la = _dot_sel_lhs(tri.astype(_BF), d_a)
    la_t = la.T
    dt_t = dt.T
    e_la = jnp.exp(la)
    last = la[q - 1:q, :]
    wend = jnp.exp(last - la) * dt
    e_last = jnp.exp(last)
    lo = lax.broadcasted_iota(jnp.int32, (q, _PAIR), 1) < SSM_HEAD_DIM
    lo_row = lo[0:1, :]

    for g in range(SSM_GROUPS):
        bm_t = bm[:, g * SSM_STATE:(g + 1) * SSM_STATE].T.astype(_BF)
        cm_g = cm[:, g * SSM_STATE:(g + 1) * SSM_STATE].astype(_BF)
        cb = _dot(cm_g, bm_t)
        ys = []
        for pp in range(_HPG // 2):
            p = g * (_HPG // 2) + pp
            cols = slice(p * _PAIR, (p + 1) * _PAIR)
            wts = []
            for h in (2 * p, 2 * p + 1):
                seg = la[:, h:h + 1] - la_t[h:h + 1, :]
                wts.append(cb * jnp.exp(jnp.where(tri, seg, -jnp.inf)) * dt_t[h:h + 1, :])
            w2 = jnp.concatenate(wts, axis=1).astype(_BF)
            x_pair = xs[:, cols]
            x2 = jnp.concatenate([jnp.where(lo, x_pair, 0.0), jnp.where(lo, 0.0, x_pair)], axis=0).astype(_BF)
            s_t = sT_ref[:, cols]
            e_pair = jnp.where(lo, e_la[:, 2 * p:2 * p + 1], e_la[:, 2 * p + 1:2 * p + 2])
            y_pair = _dot(w2, x2) + _dot(cm_g, s_t.astype(_BF)) * e_pair
            wd_pair = jnp.where(lo, wend[:, 2 * p:2 * p + 1], wend[:, 2 * p + 1:2 * p + 2])
            el_pair = jnp.where(lo_row, e_last[:, 2 * p:2 * p + 1], e_last[:, 2 * p + 1:2 * p + 2])
            sT_ref[:, cols] = s_t * el_pair + _dot(bm_t, (x_pair * wd_pair).astype(_BF))
            ys.append(y_pair + x_pair * dskip_ref[:, cols])
        gcols = slice(g * _GRP_DIM, (g + 1) * _GRP_DIM)
        gated = jnp.concatenate(ys, axis=1) * _silu(z_ref[rows, gcols].astype(_F32))
        y_ref[rows, D_MODEL + g * _GRP_DIM:D_MODEL + (g + 1) * _GRP_DIM] = _rms(gated, gn_ref[:, gcols]).astype(y_ref.dtype)


_MIX_SUB_CHUNKS = 2


def _mix_ab_prompt(u, v, z, xbc, dt, bsz, seq, ws, bs, cw, cb, dtb, alog, dskip, gn, cast_srcs):
    n_sub = _MIX_SUB_CHUNKS
    nc = seq // (CHUNK * n_sub)
    row = lambda width: pl.BlockSpec((CHUNK * n_sub, width), lambda b, c: (b * nc + c, 0))
    small = [ws, bs, cw, cb, dtb, alog, dskip, gn]
    outs = [jax.ShapeDtypeStruct((bsz * seq, 2 * D_MODEL), _BF),
            jax.ShapeDtypeStruct((bsz, SSM_DIM, SSM_STATE), _F32),
            jax.ShapeDtypeStruct((bsz, SSM_CONV - 1, SSM_CONV_DIM), _F32)]
    in_specs = [row(D_MODEL), row(D_MODEL), row(D_MODEL), row(SSM_CONV_DIM), row(LANES)] + [
        _resident(a.shape) for a in small]
    out_specs = [row(2 * D_MODEL),
                 pl.BlockSpec((1, SSM_DIM, SSM_STATE), lambda b, c: (b, 0, 0)),
                 pl.BlockSpec((1, SSM_CONV - 1, SSM_CONV_DIM), lambda b, c: (b, 0, 0))]
    args = [u, v, z, xbc, dt, *small]
    sides = [_side_cast(src, bsz * nc, lambda b, c: b * nc + c) for src in cast_srcs]
    hosted = [src for src, side in zip(cast_srcs, sides) if side]
    in_specs += [side[0] for side in sides if side]
    out_specs += [side[0] for side in sides if side]
    outs += [side[1] for side in sides if side]
    res = pl.pallas_call(
        functools.partial(_mix_ab_prompt_body, n_sub, len(hosted)), out_shape=tuple(outs), grid=(bsz, nc),
        in_specs=in_specs, out_specs=tuple(out_specs),
        scratch_shapes=[pltpu.VMEM((n_sub, SUBLANES + CHUNK, SSM_CONV_DIM), _F32),
                        pltpu.VMEM((SSM_STATE, SSM_DIM), _F32), pltpu.VMEM((A_GROUPS, CHUNK, CHUNK), _BF)],
        compiler_params=_cparams(2), name="mix_ab_prompt",
    )(*args, *hosted)
    done = iter(res[3:])
    return res[0], res[1], res[2], [next(done) if side else src.astype(_BF) for src, side in zip(cast_srcs, sides)]


_SEQ_BLK = 8


def _mix_ab_sample_body(seq, u_ref, v_ref, z_ref, xbc_ref, dt_ref, e_ref, st_ref, coef_ref, sb_ref, cw_ref, cb_ref,
                        dtb_ref, alog_ref, dskip_ref, gn_ref, y_ref, sto_ref, sh_ref):
    rows = _SEQ_BLK * seq
    t, seq_of_row = _pos_and_seq(rows, seq)
    pad_rows = sh_ref.shape[0]
    sh_ref[...] = jnp.zeros((pad_rows, SSM_CONV_DIM), _F32)

    def shifts(val, deltas):
        width = val.shape[1]
        sh_ref[SUBLANES:SUBLANES + rows, 0:width] = val
        return [sh_ref[SUBLANES - d:SUBLANES - d + rows, 0:width] for d in deltas]

    back = tuple(range(1, seq))

    v = v_ref[...]
    s = coef_ref[0] * v + sb_ref[...]
    for d, vs in zip(back, shifts(v, back)):
        s = s + coef_ref[d] * jnp.where(t >= d, vs, 0.0)
    y_ref[:, 0:D_MODEL] = u_ref[...] * s

    xbc = xbc_ref[...]
    conv = xbc * cw_ref[SSM_CONV - 1:SSM_CONV, :]
    for j, xsft in zip(range(1, SSM_CONV), shifts(xbc, tuple(range(1, SSM_CONV)))):
        conv = conv + (jnp.where(t >= j, xsft, 0.0) + e_ref[j - 1]) * cw_ref[SSM_CONV - 1 - j:SSM_CONV - j, :]
    act = _silu(conv + cb_ref[...])
    xs = act[:, :SSM_DIM]
    bm = act[:, SSM_DIM:SSM_DIM + SSM_GROUPS * SSM_STATE]
    cm = act[:, SSM_DIM + SSM_GROUPS * SSM_STATE:]

    dt = _softplus(dt_ref[...] + dtb_ref[...])
    d_a = dt * (-jnp.exp(alog_ref[...]))
    la = d_a
    for d, sft in zip(back, shifts(d_a, back)):
        la = la + jnp.where(t >= d, sft, 0.0)
    la_back = shifts(la, back)
    dt_back = shifts(dt, back)
    last = jnp.where(t == seq - 1, la, 0.0)
    fwd = shifts(la, tuple(-d for d in back))
    for d, sft in zip(back, fwd):
        last = last + jnp.where(t == seq - 1 - d, sft, 0.0)
    wend = jnp.exp(last - la) * dt
    e_last = jnp.exp(last)
    e_la = jnp.exp(la)

    head_lane = lax.broadcasted_iota(jnp.int32, (rows, LANES), 1)
    bm_back = shifts(bm, back)

    def cb_heads(b_other):
        per_g = [jnp.sum(cm[:, g * SSM_STATE:(g + 1) * SSM_STATE] * b_other[:, g * SSM_STATE:(g + 1) * SSM_STATE],
                         axis=-1, keepdims=True) for g in range(SSM_GROUPS)]
        return jnp.where(head_lane < _HPG, per_g[0], per_g[1])

    gs = [cb_heads(bm) * dt]
    for i, d in enumerate(back):
        gd = cb_heads(bm_back[i]) * jnp.exp(la - la_back[i]) * dt_back[i]
        gs.append(jnp.where(t >= d, gd, 0.0))

    head0 = lax.broadcasted_iota(jnp.int32, (LANES, SSM_DIM), 0) * SSM_HEAD_DIM
    chan = lax.broadcasted_iota(jnp.int32, (LANES, SSM_DIM), 1)
    expand = ((chan >= head0) & (chan < head0 + SSM_HEAD_DIM)).astype(_BF)
    stacked = jnp.concatenate([e_la, wend, e_last] + gs, axis=0)
    full = _dot_sel_rhs(stacked, expand)
    e_la_f, wend_f, e_last_f = full[0:rows], full[rows:2 * rows], full[2 * rows:3 * rows]
    y2 = full[3 * rows:4 * rows] * xs
    for i, (d, xsft) in enumerate(zip(back, shifts(xs, back))):
        y2 = y2 + full[(4 + i) * rows:(5 + i) * rows] * jnp.where(t >= d, xsft, 0.0)

    zpad = jnp.zeros((LANES - rows, SSM_DIM), _F32)
    xw_t = jnp.concatenate([xs * wend_f, zpad], axis=0).T.astype(_BF)
    el_t = jnp.concatenate([e_last_f, zpad], axis=0).T
    row_lane = lax.broadcasted_iota(jnp.int32, (_GRP_DIM, LANES), 1)
    bpad = jnp.zeros((LANES - rows, SSM_STATE), _F32)

    y1 = [jnp.zeros((rows, _GRP_DIM), _F32) for _ in range(SSM_GROUPS)]
    for i in range(_SEQ_BLK):
        mine = seq_of_row == i
        for g in range(SSM_GROUPS):
            gr = slice(g * _GRP_DIM, (g + 1) * _GRP_DIM)
            st = st_ref[i, gr, :]
            b_g = jnp.where(mine, bm[:, g * SSM_STATE:(g + 1) * SSM_STATE], 0.0)
            b_g = jnp.concatenate([b_g, bpad], axis=0).astype(_BF)
            decay = jnp.sum(jnp.where(row_lane == i * seq, el_t[gr, :], 0.0), axis=1, keepdims=True)
            sto_ref[i, gr, :] = st * decay + _dot(xw_t[gr, :], b_g)
            c_g = cm[:, g * SSM_STATE:(g + 1) * SSM_STATE].astype(_BF)
            ch = lax.dot_general(c_g, st.astype(_BF), _NT, preferred_element_type=_F32)
            y1[g] = y1[g] + jnp.where(mine, ch, 0.0)

    for g in range(SSM_GROUPS):
        gr = slice(g * _GRP_DIM, (g + 1) * _GRP_DIM)
        ys = y1[g] * e_la_f[:, gr] + y2[:, gr] + xs[:, gr] * dskip_ref[:, gr]
        gated = ys * _silu(z_ref[:, gr])
        y_ref[:, D_MODEL + g * _GRP_DIM:D_MODEL + (g + 1) * _GRP_DIM] = _rms(gated, gn_ref[:, gr])


def _mix_ab_sample(u, v, z, xbc, dt, e_buf, state, seq, coef, sb, cw, cb, dtb, alog, dskip, gn):
    bsz = state.shape[0]
    rows = _SEQ_BLK * seq
    row = lambda width: pl.BlockSpec((rows, width), lambda i: (i, 0))
    small = [coef, sb, cw, cb, dtb, alog, dskip, gn]
    st_spec = pl.BlockSpec((_SEQ_BLK, SSM_DIM, SSM_STATE), lambda i: (i, 0, 0))
    outs = (jax.ShapeDtypeStruct((bsz * seq, 2 * D_MODEL), _F32), jax.ShapeDtypeStruct(state.shape, _F32))
    return pl.pallas_call(
        functools.partial(_mix_ab_sample_body, seq), out_shape=outs, grid=(bsz // _SEQ_BLK,),
        in_specs=[row(D_MODEL), row(D_MODEL), row(D_MODEL), row(SSM_CONV_DIM), row(LANES),
                  pl.BlockSpec((SSM_CONV - 1, rows, SSM_CONV_DIM), lambda i: (0, i, 0)), st_spec]
        + [_resident(a.shape) for a in small],
        out_specs=(row(2 * D_MODEL), st_spec),
        scratch_shapes=[pltpu.VMEM((rows + 2 * SUBLANES, SSM_CONV_DIM), _F32)],
        compiler_params=_cparams(1), name="mix_ab_sample",
    )(u, v, z, xbc, dt, e_buf, state, *small)


def _side_cast(src, n_steps, step_of):
    n_e, rows, cols = src.shape
    per_e = n_steps // n_e
    if per_e == 0 or n_steps % n_e or rows % per_e or (rows // per_e) % _BF16_ROWS:
        return None
    blk = (1, rows // per_e, cols)
    imap = lambda *g: (step_of(*g) // per_e, step_of(*g) % per_e, 0)
    return pl.BlockSpec(blk, imap), jax.ShapeDtypeStruct(src.shape, _BF)


def _out_q_body(a_ref, x_ref, wo_ref, gm_ref, wq_ref, xo_ref, q_ref):
    xn = x_ref[...] + _dot(a_ref[...].astype(_BF), wo_ref[...])
    xo_ref[...] = xn
    q_ref[...] = _dot(_rms(xn, gm_ref[...]).astype(_BF), wq_ref[...]).astype(q_ref.dtype)


def _out_q(a, x2d, wo, gm, wq, q_dtype, tm):
    m = x2d.shape[0]
    row = lambda width: pl.BlockSpec((tm, width), lambda i: (i, 0))
    outs = (jax.ShapeDtypeStruct((m, D_MODEL), _F32), jax.ShapeDtypeStruct((m, D_MODEL), q_dtype))
    return pl.pallas_call(
        _out_q_body, out_shape=outs, grid=(m // tm,),
        in_specs=[row(a.shape[1]), row(D_MODEL), _resident(wo.shape), _resident(gm.shape), _resident(wq.shape)],
        out_specs=(row(D_MODEL), row(D_MODEL)), compiler_params=_cparams(1), name="out_q",
    )(a, x2d, wo, gm, wq)


_ATT_SCALE = MEM_HEAD_DIM ** -0.5


def _attend(q, k, v):
    outs = []
    for h in range(MEM_HEADS):
        sl = slice(h * MEM_HEAD_DIM, (h + 1) * MEM_HEAD_DIM)
        s = lax.dot_general(q[:, sl], k[:, sl], _NT, preferred_element_type=_F32) * _ATT_SCALE
        p = jnp.exp(s - jnp.max(s, axis=-1, keepdims=True))
        p = p / jnp.sum(p, axis=-1, keepdims=True)
        outs.append(_dot(p.astype(_BF), v[:, sl]))
    return jnp.concatenate(outs, axis=1)


def _attn_prompt_body(q_ref, k_ref, v_ref, x_ref, wo_ref, o_ref):
    o = _attend(q_ref[...], k_ref[0], v_ref[0])
    o_ref[...] = x_ref[...] + _dot(o.astype(_BF), wo_ref[...])


def _attn_prompt(q, k, v, x2d, wo, layer, bsz, seq, tq):
    nt = seq // tq
    row = pl.BlockSpec((tq, D_MODEL), lambda b, j: (b * nt + j, 0))
    kv = pl.BlockSpec((1, N_MEM, D_MODEL), lambda b, j: (layer * bsz + b, 0, 0))
    return pl.pallas_call(
        _attn_prompt_body, out_shape=jax.ShapeDtypeStruct(x2d.shape, _F32), grid=(bsz, nt),
        in_specs=[row, kv, kv, row, _resident(wo.shape)], out_specs=row,
        compiler_params=_cparams(2), name="attn_prompt",
    )(q, k, v, x2d, wo)


def _attn_sample_body(seq, q_ref, k_ref, v_ref, x_ref, wo_ref, o_ref, o_scr):
    per_tile = SUBLANES // seq
    _, seq_of_row = _pos_and_seq(SUBLANES, seq)
    n_lt = MEM_HEAD_DIM // LANES
    rows_mh = N_MEM * MEM_HEADS
    col = lax.broadcasted_iota(jnp.int32, (MEM_HEADS * SUBLANES, rows_mh), 1)
    row = lax.broadcasted_iota(jnp.int32, (MEM_HEADS * SUBLANES, rows_mh), 0)
    same_head = (col & (MEM_HEADS - 1)) == (row >> (SUBLANES.bit_length() - 1))
    for tile in range(_SEQ_BLK // per_tile):
        q8 = q_ref[tile * SUBLANES:(tile + 1) * SUBLANES, :].astype(_BF)
        o = None
        for j in range(per_tile):
            i = tile * per_tile + j
            s = None
            for lt in range(n_lt):
                qh = jnp.concatenate([q8[:, h * MEM_HEAD_DIM + lt * LANES:h * MEM_HEAD_DIM + (lt + 1) * LANES]
                                      for h in range(MEM_HEADS)], axis=0)
                k_lt = k_ref[i, :, :, lt * LANES:(lt + 1) * LANES].reshape(rows_mh, LANES).astype(_BF)
                part = lax.dot_general(qh, k_lt, _NT, preferred_element_type=_F32)
                s = part if s is None else s + part
            s = jnp.where(same_head, s * _ATT_SCALE, -jnp.inf)
            p = jnp.exp(s - jnp.max(s, axis=-1, keepdims=True))
            p = (p / jnp.sum(p, axis=-1, keepdims=True)).astype(_BF)
            pieces = [None] * (MEM_HEADS * n_lt)
            for lt in range(n_lt):
                v_lt = v_ref[i, :, :, lt * LANES:(lt + 1) * LANES].reshape(rows_mh, LANES).astype(_BF)
                o_lt = _dot(p, v_lt)
                for h in range(MEM_HEADS):
                    pieces[h * n_lt + lt] = o_lt[h * SUBLANES:(h + 1) * SUBLANES, :]
            oi = jnp.concatenate(pieces, axis=1)
            o = oi if o is None else jnp.where(seq_of_row == j, oi, o)
        o_scr[tile * SUBLANES:(tile + 1) * SUBLANES, :] = o
    o_ref[...] = x_ref[...] + _dot(o_scr[...].astype(_BF), wo_ref[...])


def _attn_sample(q, k, v, x2d, wo, layer, bsz, seq):
    rows = _SEQ_BLK * seq
    nb = bsz // _SEQ_BLK
    row = pl.BlockSpec((rows, D_MODEL), lambda i: (i, 0))
    kv = pl.BlockSpec((_SEQ_BLK, N_MEM, MEM_HEADS, MEM_HEAD_DIM), lambda i: (layer * nb + i, 0, 0, 0))
    return pl.pallas_call(
        functools.partial(_attn_sample_body, seq), out_shape=jax.ShapeDtypeStruct(x2d.shape, _F32), grid=(nb,),
        in_specs=[row, kv, kv, row, _resident(wo.shape)], out_specs=row,
        scratch_shapes=[pltpu.VMEM((rows, D_MODEL), _F32)],
        compiler_params=_cparams(1), name="attn_sample",
    )(q, k, v, x2d, wo)


_FF_SUB = 512
_MOE_TILE = 1024


def _swiglu_chunk(h, wg_ref, wu_ref, wd_ref):
    width = wg_ref.shape[2]
    hids = []
    for lo in range(0, width, _FF_SUB):
        hi = min(lo + _FF_SUB, width)
        hids.append((_silu(_dot(h, wg_ref[0, :, lo:hi])) * _dot(h, wu_ref[0, :, lo:hi])).astype(_BF))
    return _dot(jnp.concatenate(hids, axis=1), wd_ref[0])


def _ffn_body(x_ref, g_ref, wg_ref, wu_ref, wd_ref, o_ref):
    x = x_ref[...]
    o_ref[...] = x + _swiglu_chunk(_rms(x, g_ref[...]).astype(_BF), wg_ref, wu_ref, wd_ref)


def _ffn(x2d, g, wg, wu, wd, tm):
    m = x2d.shape[0]
    row = pl.BlockSpec((tm, D_MODEL), lambda i: (i, 0))
    return pl.pallas_call(
        _ffn_body, out_shape=jax.ShapeDtypeStruct(x2d.shape, _F32), grid=(m // tm,),
        in_specs=[row, _resident(g.shape), _resident(wg.shape), _resident(wu.shape), _resident(wd.shape)],
        out_specs=row, compiler_params=_cparams(1), name="ffn_dense",
    )(x2d, g, wg, wu, wd)


_MOE_PICK_PROB = 2.0 / N_EXPERTS
_MOE_CAP_STEPS = (1.0, 1.25, 1.5)


_BF16_ROWS = 16
_MOE_DENSE_ROWS = 256


def _moe_capacities(tm):
    mean = tm * _MOE_PICK_PROB
    return tuple(int(-(-(mean * s) // _BF16_ROWS) * _BF16_ROWS) for s in _MOE_CAP_STEPS)


def _moe_body(caps, final_norm, x_ref, g_ref, wr_ref, wg_ref, wu_ref, wd_ref, gf_ref, o_ref, hf_ref, comb_ref,
              rank_ref, rank_t_ref, cnt_ref):
    e = pl.program_id(1)
    tm = x_ref.shape[0]
    lane = lax.broadcasted_iota(jnp.int32, (tm, LANES), 1)

    @pl.when(e == 0)
    def _():
        x = x_ref[...]
        hf = _rms(x, g_ref[...])
        hf_hi = hf.astype(_BF)
        hf_ref[...] = hf_hi
        o_ref[...] = x
        hf_lo = (hf - hf_hi.astype(_F32)).astype(_BF)
        wr = wr_ref[...]
        wr_hi = wr.astype(_BF)
        wr_lo = (wr - wr_hi.astype(_F32)).astype(_BF)
        logits = _dot(hf_hi, wr_hi) + (_dot(hf_hi, wr_lo) + _dot(hf_lo, wr_hi))
        lg = jnp.where(lane < N_EXPERTS, logits, -jnp.inf)
        v1 = jnp.max(lg, axis=-1, keepdims=True)
        i1 = jnp.min(jnp.where(lg == v1, lane, LANES), axis=-1, keepdims=True)
        lg2 = jnp.where(lane == i1, -jnp.inf, lg)
        v2 = jnp.max(lg2, axis=-1, keepdims=True)
        i2 = jnp.min(jnp.where(lg2 == v2, lane, LANES), axis=-1, keepdims=True)
        e2 = jnp.exp(v2 - v1)
        comb_ref[...] = jnp.where(lane == i1, 1.0 / (1.0 + e2), 0.0) + jnp.where(lane == i2, e2 / (1.0 + e2), 0.0)
        picked = (lane == i1) | (lane == i2)
        picked_b = picked.astype(_BF)
        blk_r = lax.broadcasted_iota(jnp.int32, (LANES, LANES), 0)
        blk_c = lax.broadcasted_iota(jnp.int32, (LANES, LANES), 1)
        earlier = (blk_c < blk_r).astype(_BF)
        running = jnp.zeros((1, LANES), _F32)
        for b in range(tm // LANES):
            rows = slice(b * LANES, (b + 1) * LANES)
            pb = picked_b[rows, :]
            rank_b = jnp.where(picked[rows, :], _dot(earlier, pb) + running, -1.0)
            rank_ref[rows, :] = rank_b
            rank_t_ref[:, rows] = rank_b.T
            running = running + jnp.sum(pb.astype(_F32), axis=0, keepdims=True)
        cnt_ref[...] = running

    lane_row = lax.broadcasted_iota(jnp.int32, (1, LANES), 1)
    count = jnp.sum(jnp.where(lane_row == e, cnt_ref[...], 0.0)).astype(jnp.int32)

    def compact_path(cap):
        cap_pad = -(-cap // LANES) * LANES
        rank_row = rank_t_ref[pl.ds(e, 1), :]
        slot_col = lax.broadcasted_iota(jnp.int32, (cap, 1), 0).astype(_F32)
        pick = (rank_row == slot_col).astype(_BF)
        xs = _dot(pick, hf_ref[...]).astype(_BF)
        y = _swiglu_chunk(xs, wg_ref, wu_ref, wd_ref).astype(_BF)
        if cap_pad > cap:
            y = jnp.concatenate([y, jnp.zeros((cap_pad - cap, D_MODEL), _BF)], axis=0)
        rank_col = jnp.sum(jnp.where(lane == e, rank_ref[...], 0.0), axis=1, keepdims=True)
        comb_col = jnp.sum(jnp.where(lane == e, comb_ref[...], 0.0), axis=1, keepdims=True)
        slot_row = lax.broadcasted_iota(jnp.int32, (1, cap_pad), 1).astype(_F32)
        put = (rank_col == slot_row).astype(_BF)
        o_ref[...] += comb_col * _dot(put, y)

    lower = 0
    for cap in caps:
        pl.when((count > lower) & (count <= cap))(functools.partial(compact_path, cap))
        lower = cap

    @pl.when(count > caps[-1])
    def _():
        blk = min(tm, _MOE_DENSE_ROWS)
        blk_lane = lax.broadcasted_iota(jnp.int32, (blk, LANES), 1)

        def dense_rows(r, carry):
            rows = pl.ds(pl.multiple_of(r * blk, blk), blk)
            comb_col = jnp.sum(jnp.where(blk_lane == e, comb_ref[rows, :], 0.0), axis=1, keepdims=True)
            o_ref[rows, :] += comb_col * _swiglu_chunk(hf_ref[rows, :], wg_ref, wu_ref, wd_ref)
            return carry

        lax.fori_loop(0, tm // blk, dense_rows, 0)

    if final_norm:
        @pl.when(e == pl.num_programs(1) - 1)
        def _():
            o_ref[...] = _rms(o_ref[...], gf_ref[...])


def _moe(x2d, g, wr, wg, wu, wd, gf, final_norm, tm):
    m = x2d.shape[0]
    n_e, _, d_ff = wg.shape
    assert tm % LANES == 0 and m % tm == 0, (m, tm)
    caps = _moe_capacities(tm)
    row_once = pl.BlockSpec((tm, D_MODEL), lambda i, e: (i, 0), pipeline_mode=pl.Buffered(1))
    up = pl.BlockSpec((1, D_MODEL, d_ff), lambda i, e: (e, 0, 0))
    down = pl.BlockSpec((1, d_ff, D_MODEL), lambda i, e: (e, 0, 0))
    return pl.pallas_call(
        functools.partial(_moe_body, caps, final_norm), out_shape=jax.ShapeDtypeStruct(x2d.shape, _F32),
        grid=(m // tm, n_e),
        in_specs=[row_once, _resident(g.shape), _resident(wr.shape), up, up, down, _resident(gf.shape)],
        out_specs=pl.BlockSpec((tm, D_MODEL), lambda i, e: (i, 0)),
        scratch_shapes=[pltpu.VMEM((tm, D_MODEL), _BF), pltpu.VMEM((tm, LANES), _F32), pltpu.VMEM((tm, LANES), _F32),
                        pltpu.VMEM((LANES, tm), _F32), pltpu.VMEM((1, LANES), _F32)],
        compiler_params=_cparams(2, _MOE_VMEM_LIMIT), name="moe",
    )(x2d, g, wr, wg, wu, wd, gf)


_SC_CONV = 3


def _mix_c_body(per_seq, seq, x_ref, g_ref, win_ref, cw_ref, e_ref, wout_ref, gm_ref, wq_ref, xo_ref, q_ref, p_ref,
                pp_ref):
    tm = x_ref.shape[0]
    x = x_ref[...]
    hn = _rms(x, g_ref[...]).astype(_BF)
    bg = _dot(hn, win_ref[:, 0:D_MODEL])
    p = _dot(hn, win_ref[:, D_MODEL:2 * D_MODEL]) * _dot(hn, win_ref[:, 2 * D_MODEL:3 * D_MODEL])
    if per_seq:
        @pl.when(pl.program_id(1) == 0)
        def _():
            pp_ref[0:SUBLANES, :] = jnp.zeros((SUBLANES, D_MODEL), _F32)
    else:
        pp_ref[0:SUBLANES, :] = jnp.zeros((SUBLANES, D_MODEL), _F32)
    pp_ref[SUBLANES:SUBLANES + tm, :] = p
    p1 = pp_ref[SUBLANES - 1:SUBLANES - 1 + tm, :]
    p2 = pp_ref[SUBLANES - 2:SUBLANES - 2 + tm, :]
    if per_seq:
        pp_ref[0:SUBLANES, :] = p[tm - SUBLANES:tm, :]
    else:
        t, _ = _pos_and_seq(tm, seq)
        p1 = jnp.where(t >= 1, p1, 0.0) + e_ref[0]
        p2 = jnp.where(t >= 2, p2, 0.0) + e_ref[1]
    conv = p2 * cw_ref[0:1, :] + p1 * cw_ref[1:2, :] + p * cw_ref[2:3, :]
    xn = x + _dot((bg * conv).astype(_BF), wout_ref[...])
    xo_ref[...] = xn
    q_ref[...] = _dot(_rms(xn, gm_ref[...]).astype(_BF), wq_ref[...]).astype(q_ref.dtype)
    if per_seq:
        @pl.when(pl.program_id(1) == pl.num_programs(1) - 1)
        def _():
            p_ref[0] = pp_ref[SUBLANES + tm - (_SC_CONV - 1):SUBLANES + tm, :]
    else:
        p_ref[...] = p


def _mix_c(x2d, g, win, cw, e_buf, wout, gm, wq, q_dtype, per_seq, bsz, seq, tm):
    m = x2d.shape[0]
    if per_seq:
        nt = seq // tm
        grid = (bsz, nt)
        imap = lambda b, j: (b * nt + j, 0)
        e_spec = pl.BlockSpec((_SC_CONV - 1, SUBLANES, D_MODEL), lambda b, j: (0, 0, 0))
        p_shape = jax.ShapeDtypeStruct((bsz, _SC_CONV - 1, D_MODEL), _F32)
        p_spec = pl.BlockSpec((1, _SC_CONV - 1, D_MODEL), lambda b, j: (b, 0, 0))
    else:
        grid = (m // tm, 1)
        imap = lambda i, j: (i, 0)
        e_spec = pl.BlockSpec((_SC_CONV - 1, tm, D_MODEL), lambda i, j: (0, i, 0))
        p_shape = jax.ShapeDtypeStruct((m, D_MODEL), _F32)
        p_spec = pl.BlockSpec((tm, D_MODEL), imap)
    row = pl.BlockSpec((tm, D_MODEL), imap)
    outs = (jax.ShapeDtypeStruct((m, D_MODEL), _F32), jax.ShapeDtypeStruct((m, D_MODEL), q_dtype), p_shape)
    return pl.pallas_call(
        functools.partial(_mix_c_body, per_seq, seq), out_shape=outs, grid=grid,
        in_specs=[row, _resident(g.shape), _resident(win.shape), _resident(cw.shape), e_spec, _resident(wout.shape),
                  _resident(gm.shape), _resident(wq.shape)],
        out_specs=(row, row, p_spec),
        scratch_shapes=[pltpu.VMEM((SUBLANES + tm, D_MODEL), _F32)],
        compiler_params=_cparams(2), name="mix_c",
    )(x2d, g, win, cw, e_buf, wout, gm, wq)


def _expand_buf(buf, seq, n_back):
    bsz, km1, ch = buf.shape
    outs = []
    for j in range(1, n_back + 1):
        rows = [buf[:, km1 - j + t] if t < j else jnp.zeros((bsz, ch), buf.dtype) for t in range(seq)]
        outs.append(jnp.stack(rows, axis=1).reshape(bsz * seq, ch))
    return jnp.stack(outs)


def _row(vec, width=None):
    vec = vec.reshape(1, -1).astype(_F32)
    if width is not None and vec.shape[1] < width:
        vec = jnp.pad(vec, ((0, 0), (0, width - vec.shape[1])))
    return vec


def _trunk(x, k_all, v_all, prompt, states, w):
    bsz, seq, _ = x.shape
    m = bsz * seq
    x2d = x.reshape(m, D_MODEL)
    tm = min(_TOKEN_TILE, m)
    dskip = _row(jnp.repeat(w["d_skip"][0], SSM_HEAD_DIM))
    common = (w["conv_w_ssm"][0], _row(w["conv_b_ssm"][0]), _row(w["dt_bias"][0], LANES), _row(w["a_log"][0], LANES),
              dskip, _row(w["g_ssm_norm"][0]))

    act_dtype = _BF if prompt else _F32
    u, v, z, xbc, dt = _inproj_ab(x2d, _row(w["g_mix"][0]), w["w_in_ab"], act_dtype, tm)
    late_bf = {}
    if prompt:
        y_ab, ssm_state, conv_state, casts = _mix_ab_prompt(
            u, v, z, xbc, dt, bsz, seq, w["w_spatial"][0], w["b_spatial"][0][:, :, None], *common,
            tuple(w["late"].values()))
        late_bf = dict(zip(w["late"], casts))
        for name in ("w_out_ab", "w_in_c", "w_out_c"):
            late_bf[name] = late_bf[name][0]
        w = dict(w, **late_bf)
        v_out = None
    else:
        ws = w["w_spatial"][0][:, :seq, :seq]
        coef = []
        for d in range(seq):
            per_t = [ws[:, t, t - d] if t >= d else jnp.zeros((A_GROUPS,), _F32) for t in range(seq)]
            coef.append(jnp.tile(jnp.repeat(jnp.stack(per_t), A_HEAD, axis=1), (_SEQ_BLK, 1)))
        sb = jnp.tile(jnp.repeat(w["b_spatial"][0][:, :seq].T, A_HEAD, axis=1), (_SEQ_BLK, 1))
        e_buf = _expand_buf(states["ssm_conv"], seq, SSM_CONV - 1)
        y_ab, ssm_state = _mix_ab_sample(u, v, z, xbc, dt, e_buf, states["ssm"].reshape(bsz, SSM_DIM, SSM_STATE), seq,
                                         jnp.stack(coef), sb, *common)
        conv_state = xbc.reshape(bsz, seq, SSM_CONV_DIM)[:, seq - (SSM_CONV - 1):]
        v_out = v.reshape(bsz, seq, D_MODEL)
    x2d, q = _out_q(y_ab, x2d, w["w_out_ab"], _row(w["g_mem"][0]), w["w_mem_q"][0], act_dtype, tm)

    def attend(q, x2d, layer):
        if prompt:
            return _attn_prompt(q, k_all, v_all, x2d, w["w_mem_o"][layer], layer, bsz, seq, tm)
        return _attn_sample(q, k_all, v_all, x2d, w["w_mem_o"][layer], layer, bsz, seq)

    x2d = attend(q, x2d, 0)
    x2d = _ffn(x2d, _row(w["g_ffn"][0]), w["w_ffn_gate"], w["w_ffn_up"], w["w_ffn_down"], min(_MOE_TILE, m))

    if prompt:
        e_buf = jnp.zeros((_SC_CONV - 1, SUBLANES, D_MODEL), _F32)
        x2d, q, sconv_state = _mix_c(x2d, _row(w["g_mix"][1]), w["w_in_c"], w["conv_w_c"][0], e_buf, w["w_out_c"],
                                     _row(w["g_mem"][1]), w["w_mem_q"][1], act_dtype, True, bsz, seq, tm)
    else:
        e_buf = _expand_buf(states["sconv"], seq, _SC_CONV - 1)
        x2d, q, p = _mix_c(x2d, _row(w["g_mix"][1]), w["w_in_c"], w["conv_w_c"][0], e_buf, w["w_out_c"],
                           _row(w["g_mem"][1]), w["w_mem_q"][1], act_dtype, False, bsz, seq, tm)
        sconv_state = p.reshape(bsz, seq, D_MODEL)[:, seq - (_SC_CONV - 1):]
    x2d = attend(q, x2d, 1)
    x2d = _moe(x2d, _row(w["g_ffn"][1]), w["w_router"], w["w_exp_gate"], w["w_exp_up"], w["w_exp_down"],
               _row(w["g_final"]), True, min(_MOE_TILE, m))
    y = x2d.reshape(bsz, seq, D_MODEL)
    ssm_state = ssm_state.reshape(1, bsz, SSM_HEADS, SSM_HEAD_DIM, SSM_STATE)
    return y, ssm_state, conv_state[None], sconv_state[None], v_out, late_bf


def kernel(x_prompt, x_sample, mem_prompt, state_ssm, state_ssm_conv, state_sconv, cache_mem_k, cache_mem_v, g_mix, g_mem, g_ffn, g_final, w_in_ab, w_spatial, b_spatial, conv_w_ssm, conv_b_ssm, dt_bias, a_log, d_skip, g_ssm_norm, w_out_ab, w_ffn_gate, w_ffn_up, w_ffn_down, w_in_c, conv_w_c, w_out_c, w_router, w_exp_gate, w_exp_up, w_exp_down, w_mem_q, w_mem_k, w_mem_v, w_mem_o):
    depth = w_mem_q.shape[0]
    bp = x_prompt.shape[0]
    bs = x_sample.shape[0]
    bf = lambda a: a.astype(_BF)
    w = dict(
        g_mix=g_mix, g_mem=g_mem, g_ffn=g_ffn, g_final=g_final, w_spatial=w_spatial, b_spatial=b_spatial,
        conv_w_ssm=conv_w_ssm, conv_b_ssm=conv_b_ssm, dt_bias=dt_bias, a_log=a_log, d_skip=d_skip, g_ssm_norm=g_ssm_norm,
        conv_w_c=conv_w_c,
        w_in_ab=jnp.pad(bf(w_in_ab[0]), ((0, 0), (0, LANES - SSM_HEADS))),
        w_router=jnp.pad(w_router[0], ((0, 0), (0, LANES - N_EXPERTS))),
        late=dict(w_exp_gate=w_exp_gate[0], w_exp_up=w_exp_up[0], w_exp_down=w_exp_down[0], w_out_ab=w_out_ab,
                  w_ffn_gate=w_ffn_gate, w_ffn_up=w_ffn_up, w_ffn_down=w_ffn_down, w_in_c=w_in_c, w_out_c=w_out_c,
                  w_mem_q=w_mem_q, w_mem_o=w_mem_o),
    )
    mem_k_p, mem_v_p, k_bf, v_bf = _mem_kv(mem_prompt, bf(w_mem_k), bf(w_mem_v))
    y_p, ssm_p, ssmconv_p, sconv_p, _, late_bf = _trunk(x_prompt, k_bf, v_bf, True, None, w)
    w = dict(w, **late_bf)
    states = dict(ssm=state_ssm[0], ssm_conv=state_ssm_conv[0], sconv=state_sconv[0])
    y_s, ssm_s, ssmconv_s, sconv_s, v_s, _ = _trunk(
        x_sample, cache_mem_k.reshape(depth * bs, N_MEM, MEM_HEADS, MEM_HEAD_DIM),
        cache_mem_v.reshape(depth * bs, N_MEM, MEM_HEADS, MEM_HEAD_DIM), False, states, w)
    return (y_p, y_s, ssm_p, ssm_s, ssmconv_p, ssmconv_s, sconv_p, sconv_s, mem_k_p, mem_v_p, v_s[None])
```
